```python
import math
import jax, jax.numpy as jnp
from jax import lax
import numpy as np

D_MODEL = 1024
BATCH = 8
SEQ = 2048
DEPTH = 4

M_HEADS = 4
M_HEAD_DIM = 64
M_CHUNK = 64
M_WIDTH = M_HEADS * M_HEAD_DIM
SB_HEADS = 4
SB_HEAD_DIM = 64
SB_BLOCK = 128
SB_WIDTH = SB_HEADS * SB_HEAD_DIM
G_HEADS = 4
G_HEAD_DIM = 128
G_CHUNK = 64
G_CONV = 4
G_WIDTH = G_HEADS * G_HEAD_DIM
D_MIX = M_WIDTH + SB_WIDTH + G_WIDTH
IN_SPLITS = (M_WIDTH, M_WIDTH, M_WIDTH, M_WIDTH, M_HEADS, M_HEADS,
             SB_WIDTH, SB_WIDTH, SB_WIDTH,
             G_WIDTH, G_WIDTH, G_WIDTH, G_WIDTH, G_HEADS, G_HEADS)
D_IN = sum(IN_SPLITS)
N_EXPERTS = 32
TOP_K = 4
D_FF = D_MODEL
SWIGLU_LIMIT = 7.0
SWIGLU_ALPHA = 1.702
MOE_BLOCK = 256
PLE_DIM = 256
DEEPNORM_ALPHA = (2 * DEPTH) ** 0.25
DEEPNORM_BETA = (8 * DEPTH) ** -0.25
LN_EPS = 1e-5
NORM_EPS = 1e-6

kernel_name = 'hybrid_mlstm_stickbreak_gdn_moe_deepnorm'

F32 = jnp.float32


def layer_norm(x, g, b):
    xf = x.astype(F32)
    mu = jnp.mean(xf, -1, keepdims=True)
    var = jnp.mean(jnp.square(xf - mu), -1, keepdims=True)
    return ((xf - mu) * lax.rsqrt(var + LN_EPS) * g + b).astype(x.dtype)


def rms_normalize(x):
    return x * lax.rsqrt(jnp.mean(x * x, -1, keepdims=True) + NORM_EPS)


def l2_normalize(x):
    return x * lax.rsqrt(jnp.sum(x * x, -1, keepdims=True) + NORM_EPS)


def split_heads(t, heads):
    B, S, W = t.shape
    return t.reshape(B, S, heads, W // heads).astype(F32)


def to_chunks(t, size):
    B, S, H = t.shape[:3]
    t = t.reshape((B, S // size, size, H) + t.shape[3:])
    return jnp.moveaxis(t, (1, 3), (0, 2))


def from_chunks(t):
    t = jnp.moveaxis(t, (0, 2), (1, 3))
    B, N, L, H, d = t.shape
    return t.reshape(B, N * L, H, d)


def causal_depthwise_conv(x, w):
    C = x.shape[-1]
    return lax.conv_general_dilated(x, w[:, None, :], window_strides=(1,),
                                    padding=[(w.shape[0] - 1, 0)],
                                    dimension_numbers=('NWC', 'WIO', 'NWC'),
                                    feature_group_count=C)


def mlstm_mixer(q, k, v, o_pre, i_pre, f_pre, i_bias, f_bias, norm_w):
    q = to_chunks(split_heads(q, M_HEADS), M_CHUNK)
    k = to_chunks(split_heads(k, M_HEADS), M_CHUNK) * (M_HEAD_DIM ** -0.5)
    v = to_chunks(split_heads(v, M_HEADS), M_CHUNK)
    log_i = to_chunks(i_pre.astype(F32) + i_bias.astype(F32), M_CHUNK)
    log_f = to_chunks(jax.nn.log_sigmoid(f_pre.astype(F32) + f_bias.astype(F32)), M_CHUNK)
    b = jnp.cumsum(log_f, axis=-1)
    causal = jnp.tril(jnp.ones((M_CHUNK, M_CHUNK), bool))
    log_w = jnp.where(causal, b[..., :, None] - b[..., None, :] + log_i[..., None, :], -jnp.inf)
    m_intra = jnp.max(log_w, -1)
    log_w_end = b[..., -1:] - b + log_i
    m_end = jnp.max(log_w_end, -1)

    def step(carry, xs):
        C, n, m = carry
        qc, kc, vc, bc, lw, mi, lwe, me = xs
        m_t = jnp.maximum(bc + m[..., None], mi)
        inter = jnp.exp(bc + m[..., None] - m_t)
        s = jnp.einsum('bhtd,bhsd->bhts', qc, kc) * jnp.exp(lw - m_t[..., None])
        num = inter[..., None] * jnp.einsum('bhtd,bhde->bhte', qc, C) + jnp.einsum('bhts,bhse->bhte', s, vc)
        den = inter * jnp.einsum('bhtd,bhd->bht', qc, n) + jnp.sum(s, -1)
        h = num / jnp.maximum(jnp.abs(den), jnp.exp(-m_t))[..., None]
        m_new = jnp.maximum(bc[..., -1] + m, me)
        carry_scale = jnp.exp(bc[..., -1] + m - m_new)
        wk = jnp.exp(lwe - m_new[..., None])[..., None] * kc
        C = carry_scale[..., None, None] * C + jnp.einsum('bhsd,bhse->bhde', wk, vc)
        n = carry_scale[..., None] * n + jnp.sum(wk, -2)
        return (C, n, m_new), h

    N, B, H, L, d = q.shape
    init = (jnp.zeros((B, H, d, d), F32), jnp.zeros((B, H, d), F32), jnp.full((B, H), -jnp.inf, F32))
    _, h = lax.scan(step, init, (q, k, v, b, log_w, m_intra, log_w_end, m_end))
    h = rms_normalize(from_chunks(h))
    h = h.reshape(B, N * L, H * d) * norm_w
    return jax.nn.sigmoid(o_pre.astype(F32)) * h


def stick_breaking_mixer(q, k, v, norm_w):
    q = split_heads(q, SB_HEADS) * (SB_HEAD_DIM ** -0.5)
    k = split_heads(k, SB_HEADS)
    v = split_heads(v, SB_HEADS)
    B, S, H, d = q.shape
    outs = []
    for start in range(0, S, SB_BLOCK):
        end = start + SB_BLOCK
        z = jnp.einsum('bqhd,bkhd->bhqk', q[:, start:end], k[:, :end])
        before = jnp.arange(end)[None, :] < jnp.arange(start, end)[:, None]
        log_keep = jnp.where(before, jax.nn.log_sigmoid(-z), 0.0)
        log_rest = lax.cumsum(log_keep, axis=3, reverse=True) - log_keep
        a = jnp.where(before, jnp.exp(jax.nn.log_sigmoid(z) + log_rest), 0.0)
        outs.append(jnp.einsum('bhqk,bkhd->bqhd', a, v[:, :end]))
    o = rms_normalize(jnp.concatenate(outs, axis=1))
    return o.reshape(B, S, H * d) * norm_w


def gated_deltanet_mixer(q, k, v, z, a, b, conv_w, A_log, dt_bias, norm_w):
    qkv = jnp.concatenate([q, k, v], -1).astype(F32)
    qkv = jax.nn.silu(causal_depthwise_conv(qkv, conv_w.astype(F32)))
    q, k, v = jnp.split(qkv, 3, axis=-1)
    q = l2_normalize(split_heads(q, G_HEADS)) * (G_HEAD_DIM ** -0.5)
    k = l2_normalize(split_heads(k, G_HEADS))
    v = split_heads(v, G_HEADS)
    B, S, H, d = q.shape
    beta = jax.nn.sigmoid(b.astype(F32))
    g = -jnp.exp(A_log.astype(F32)) * jax.nn.softplus(a.astype(F32) + dt_bias.astype(F32))
    q, k, v = to_chunks(q, G_CHUNK), to_chunks(k, G_CHUNK), to_chunks(v, G_CHUNK)
    beta = to_chunks(beta, G_CHUNK)
    gam = jnp.cumsum(to_chunks(g, G_CHUNK), axis=-1)
    incl = jnp.tril(jnp.ones((G_CHUNK, G_CHUNK), bool))
    strict = jnp.tril(jnp.ones((G_CHUNK, G_CHUNK), bool), -1)
    decay = jnp.exp(jnp.where(incl, gam[..., :, None] - gam[..., None, :], -jnp.inf))
    m = jnp.where(strict, beta[..., :, None] * jnp.einsum('nbhtd,nbhsd->nbhts', k, k) * decay, 0.0)
    rhs = jnp.concatenate([v * beta[..., None], k * (beta * jnp.exp(gam))[..., None]], -1)
    sol = lax.linalg.triangular_solve(m + jnp.eye(G_CHUNK, dtype=F32), rhs, left_side=True,
                                      lower=True, unit_diagonal=True)
    u, w = sol[..., :d], sol[..., d:]
    qk = jnp.einsum('nbhtd,nbhsd->nbhts', q, k) * decay
    q_dec = q * jnp.exp(gam)[..., None]
    k_dec = k * jnp.exp(gam[..., -1:] - gam)[..., None]
    chunk_decay = jnp.exp(gam[..., -1])

    def step(state, xs):
        uc, wc, qkc, qdc, kdc, cd = xs
        v_new = uc - jnp.einsum('bhtd,bhde->bhte', wc, state)
        o = jnp.einsum('bhtd,bhde->bhte', qdc, state) + jnp.einsum('bhts,bhse->bhte', qkc, v_new)
        state = cd[..., None, None] * state + jnp.einsum('bhtd,bhte->bhde', kdc, v_new)
        return state, o

    _, o = lax.scan(step, jnp.zeros((B, H, d, d), F32), (u, w, qk, q_dec, k_dec, chunk_decay))
    o = rms_normalize(from_chunks(o)) * norm_w
    return o.reshape(B, S, H * d) * jax.nn.silu(z.astype(F32))


def moe_ffn(x, w_router, b_router, w_gu, b_gu, w_down, b_down):
    B, S, D = x.shape
    T = B * S
    xf = x.reshape(T, D)
    logits = xf.astype(F32) @ w_router.astype(F32) + b_router.astype(F32)
    top_vals, top_idx = lax.top_k(logits, TOP_K)
    gates = jax.nn.softmax(top_vals, axis=-1)
    A = T * TOP_K
    flat_e = top_idx.reshape(A)
    flat_tok = jnp.arange(A, dtype=jnp.int32) // TOP_K
    flat_gate = gates.reshape(A)
    order = jnp.argsort(flat_e)
    e_sorted = flat_e[order]
    counts = jnp.bincount(flat_e, length=N_EXPERTS)
    padded = (counts + MOE_BLOCK - 1) // MOE_BLOCK * MOE_BLOCK
    pad_end = jnp.cumsum(padded)
    pad_start = pad_end - padded
    grp_start = jnp.cumsum(counts) - counts
    dest = pad_start[e_sorted] + jnp.arange(A, dtype=jnp.int32) - grp_start[e_sorted]
    n_blocks = -(-A // MOE_BLOCK) + N_EXPERTS
    P = n_blocks * MOE_BLOCK
    slot_tok = jnp.zeros((P,), jnp.int32).at[dest].set(flat_tok[order])
    slot_gate = jnp.zeros((P,), F32).at[dest].set(flat_gate[order])
    block_e = jnp.minimum(jnp.searchsorted(pad_end, jnp.arange(n_blocks) * MOE_BLOCK, side='right'),
                          N_EXPERTS - 1)
    x_slots = xf[slot_tok].reshape(n_blocks, MOE_BLOCK, D)

    def expert_block(args):
        xb, e = args
        h = xb @ w_gu[e] + b_gu[e]
        gate, up = jnp.split(h, 2, axis=-1)
        gate = jnp.minimum(gate, SWIGLU_LIMIT)
        up = jnp.clip(up, -SWIGLU_LIMIT, SWIGLU_LIMIT)
        act = (up + 1.0) * gate * jax.nn.sigmoid(SWIGLU_ALPHA * gate)
        return act @ w_down[e] + b_down[e]

    y = lax.map(expert_block, (x_slots, block_e)).reshape(P, D)
    y = y * slot_gate[:, None].astype(y.dtype)
    out = jnp.zeros((T, D), y.dtype).at[slot_tok].add(y)
    return out.reshape(B, S, D).astype(x.dtype)


def decoder_layer(h, p_i, w_in, m_i_bias, m_f_bias, m_norm_w, sb_norm_w, g_conv_w, g_A_log,
                  g_dt_bias, g_norm_w, w_out, ln1_g, ln1_b, w_router, b_router, w_gu, b_gu,
                  w_down, b_down, w_ple_gate, w_ple_proj, ln2_g, ln2_b):
    u = h @ w_in
    offsets = [int(o) for o in np.cumsum(IN_SPLITS)[:-1]]
    (mq, mk, mv, mo, mi, mf, sq, sk, sv, gq, gk, gv, gz, ga, gb) = jnp.split(u, offsets, axis=-1)
    y_m = mlstm_mixer(mq, mk, mv, mo, mi, mf, m_i_bias, m_f_bias, m_norm_w)
    y_s = stick_breaking_mixer(sq, sk, sv, sb_norm_w)
    y_g = gated_deltanet_mixer(gq, gk, gv, gz, ga, gb, g_conv_w, g_A_log, g_dt_bias, g_norm_w)
    y = jnp.concatenate([y_m, y_s, y_g], axis=-1).astype(h.dtype)
    h = layer_norm(DEEPNORM_ALPHA * h + y @ w_out, ln1_g, ln1_b)
    ple = jax.nn.sigmoid(h @ w_ple_gate) * (p_i @ w_ple_proj)
    ffn = moe_ffn(h, w_router, b_router, w_gu, b_gu, w_down, b_down)
    return layer_norm(DEEPNORM_ALPHA * h + ffn + ple, ln2_g, ln2_b)


def setup_inputs(seed: int = 0) -> dict:
    key = jax.random.key(seed)
    ks = iter(jax.random.split(key, 32))
    L = DEPTH

    def normal(shape, scale):
        return jax.random.normal(next(ks), shape, F32) * scale

    x = normal((BATCH, SEQ, D_MODEL), 1.0)
    p = normal((DEPTH, BATCH, SEQ, PLE_DIM), 1.0)
    ln0_g = 1.0 + normal((D_MODEL,), 0.02)
    ln0_b = normal((D_MODEL,), 0.02)
    w_in = normal((L, D_MODEL, D_IN), D_MODEL ** -0.5)
    m_i_bias = -2.0 + normal((L, M_HEADS), 0.1)
    m_f_bias = jnp.linspace(3.0, 6.0, M_HEADS, dtype=F32)[None, :] + normal((L, M_HEADS), 0.1)
    m_norm_w = 1.0 + normal((L, M_WIDTH), 0.02)
    sb_norm_w = 1.0 + normal((L, SB_WIDTH), 0.02)
    g_conv_w = normal((L, G_CONV, 3 * G_WIDTH), G_CONV ** -0.5)
    g_A_log = jnp.log(jax.random.uniform(next(ks), (L, G_HEADS), F32, 1.0, 16.0))
    dt = jnp.exp(jax.random.uniform(next(ks), (L, G_HEADS), F32, math.log(1e-3), math.log(1e-1)))
    g_dt_bias = dt + jnp.log(-jnp.expm1(-dt))
    g_norm_w = 1.0 + normal((L, G_HEAD_DIM), 0.02)
    w_out = normal((L, D_MIX, D_MODEL), D_MIX ** -0.5 * DEEPNORM_BETA)
    ln1_g = 1.0 + normal((L, D_MODEL), 0.02)
    ln1_b = normal((L, D_MODEL), 0.02)
    w_router = normal((L, D_MODEL, N_EXPERTS), D_MODEL ** -0.5)
    b_router = normal((L, N_EXPERTS), 0.01)
    w_gu = normal((L, N_EXPERTS, D_MODEL, 2 * D_FF), D_MODEL ** -0.5)
    b_gu = normal((L, N_EXPERTS, 2 * D_FF), 0.01)
    w_down = normal((L, N_EXPERTS, D_FF, D_MODEL), D_FF ** -0.5 * DEEPNORM_BETA)
    b_down = normal((L, N_EXPERTS, D_MODEL), 0.01)
    w_ple_gate = normal((L, D_MODEL, D_MODEL), D_MODEL ** -0.5)
    w_ple_proj = normal((L, PLE_DIM, D_MODEL), PLE_DIM ** -0.5 * DEEPNORM_BETA)
    ln2_g = 1.0 + normal((L, D_MODEL), 0.02)
    ln2_b = normal((L, D_MODEL), 0.02)
    return {'x': x, 'p': p, 'ln0_g': ln0_g, 'ln0_b': ln0_b, 'w_in': w_in,
            'm_i_bias': m_i_bias, 'm_f_bias': m_f_bias, 'm_norm_w': m_norm_w,
            'sb_norm_w': sb_norm_w, 'g_conv_w': g_conv_w, 'g_A_log': g_A_log,
            'g_dt_bias': g_dt_bias, 'g_norm_w': g_norm_w, 'w_out': w_out,
            'ln1_g': ln1_g, 'ln1_b': ln1_b, 'w_router': w_router, 'b_router': b_router,
            'w_gu': w_gu, 'b_gu': b_gu, 'w_down': w_down, 'b_down': b_down,
            'w_ple_gate': w_ple_gate, 'w_ple_proj': w_ple_proj, 'ln2_g': ln2_g, 'ln2_b': ln2_b}


def reference(x, p, ln0_g, ln0_b, w_in, m_i_bias, m_f_bias, m_norm_w, sb_norm_w, g_conv_w,
              g_A_log, g_dt_bias, g_norm_w, w_out, ln1_g, ln1_b, w_router, b_router, w_gu,
              b_gu, w_down, b_down, w_ple_gate, w_ple_proj, ln2_g, ln2_b):
    h = layer_norm(x, ln0_g, ln0_b)
    for i in range(DEPTH):
        h = decoder_layer(h, p[i], w_in[i], m_i_bias[i], m_f_bias[i], m_norm_w[i], sb_norm_w[i],
                          g_conv_w[i], g_A_log[i], g_dt_bias[i], g_norm_w[i], w_out[i],
                          ln1_g[i], ln1_b[i], w_router[i], b_router[i], w_gu[i], b_gu[i],
                          w_down[i], b_down[i], w_ple_gate[i], w_ple_proj[i], ln2_g[i], ln2_b[i])
    return h
```

```python
import functools
import math

import jax
import jax.numpy as jnp
from jax import lax
from jax.experimental import pallas as pl
from jax.experimental.pallas import tpu as pltpu

F32 = jnp.float32
BF16 = jnp.bfloat16
I32 = jnp.int32

M_HEADS, M_HEAD_DIM = 4, 64
SB_HEADS, SB_HEAD_DIM = 4, 64
G_HEADS, G_HEAD_DIM = 4, 128
G_CONV = 4
M_WIDTH = M_HEADS * M_HEAD_DIM
SB_WIDTH = SB_HEADS * SB_HEAD_DIM
G_WIDTH = G_HEADS * G_HEAD_DIM
N_GATE_COLS = 16
TOP_K = 4
SWIGLU_LIMIT = 7.0
SWIGLU_ALPHA = 1.702
LN_EPS = 1e-5
NORM_EPS = 1e-6

ROW_BLOCK = 256
M_CHUNK = 256
SB_BLOCK = 256
G_CHUNK = 64
G_BLOCK = 256
G_SUB = 16
MOE_BLOCK = 256
VMEM_LIMIT = 56 * 1024 * 1024


def _cparams(sem):
    return pltpu.CompilerParams(dimension_semantics=sem, vmem_limit_bytes=VMEM_LIMIT)


def _split3(x):
    hi = x.astype(BF16)
    r1 = x - hi.astype(F32)
    mid = r1.astype(BF16)
    lo = (r1 - mid.astype(F32)).astype(BF16)
    return hi, mid, lo


def _split2(x):
    hi = x.astype(BF16)
    lo = (x - hi.astype(F32)).astype(BF16)
    return hi, lo


def _dot(a, b):
    return jnp.dot(a, b, preferred_element_type=F32)


def _dot_nt(a, b):
    return lax.dot_general(a, b, (((1,), (1,)), ((), ())), preferred_element_type=F32)


def _dot_tn(a, b):
    return lax.dot_general(a, b, (((0,), (0,)), ((), ())), preferred_element_type=F32)


def _dot3(a, b):
    ah, al = _split2(a)
    bh, bl = _split2(b)
    return _dot(ah, bh) + (_dot(ah, bl) + _dot(al, bh))


def _dot3_nt(a, b):
    ah, al = _split2(a)
    bh, bl = _split2(b)
    return _dot_nt(ah, bh) + (_dot_nt(ah, bl) + _dot_nt(al, bh))


def _exact_left(mask_bf16, x):
    hi, mid, lo = _split3(x)
    return _dot(mask_bf16, hi) + (_dot(mask_bf16, mid) + _dot(mask_bf16, lo))


def _exact_right(x, mask_bf16):
    hi, mid, lo = _split3(x)
    return _dot(hi, mask_bf16) + (_dot(mid, mask_bf16) + _dot(lo, mask_bf16))


def _softplus(x):
    return jnp.maximum(x, 0.0) + jnp.log1p(jnp.exp(-jnp.abs(x)))


def _log_sigmoid(x):
    return -_softplus(-x)


def _sigmoid(x):
    return 1.0 / (1.0 + jnp.exp(-x))


def _layer_norm(x, g, b):
    mu = jnp.mean(x, axis=-1, keepdims=True)
    xc = x - mu
    var = jnp.mean(xc * xc, axis=-1, keepdims=True)
    return xc * lax.rsqrt(var + LN_EPS) * g + b


def _iota2(shape, dim):
    return lax.broadcasted_iota(I32, shape, dim)


def _ln_kernel(x_ref, g_ref, b_ref, o_ref):
    o_ref[...] = _layer_norm(x_ref[...], g_ref[...], b_ref[...])


def _ln_call(x, g, b):
    T, D = x.shape
    tm = ROW_BLOCK
    return pl.pallas_call(
        _ln_kernel,
        grid=(T // tm,),
        in_specs=[pl.BlockSpec((tm, D), lambda i: (i, 0)),
                  pl.BlockSpec((1, D), lambda i: (0, 0)),
                  pl.BlockSpec((1, D), lambda i: (0, 0))],
        out_specs=pl.BlockSpec((tm, D), lambda i: (i, 0)),
        out_shape=jax.ShapeDtypeStruct((T, D), F32),
        compiler_params=_cparams(("parallel",)),
        name="embed_ln",
    )(x, g.reshape(1, D), b.reshape(1, D))


def _inproj_kernel(h_ref, w_ref, wgh_ref, wgl_ref, wgth_ref, wgtl_ref, u_ref, g_ref, gt_ref):
    h = h_ref[...]
    hh, hl = _split2(h)
    u_ref[...] = _dot(hh, w_ref[...])
    wgh, wgl = wgh_ref[...], wgl_ref[...]
    g_ref[...] = _dot(hh, wgh) + (_dot(hh, wgl) + _dot(hl, wgh))
    wgth, wgtl = wgth_ref[...], wgtl_ref[...]
    gt_ref[...] = _dot_nt(wgth, hh) + (_dot_nt(wgtl, hh) + _dot_nt(wgth, hl))


def _inproj_call(h, w_main, wg):
    T, D = h.shape
    N = w_main.shape[1]
    tm = ROW_BLOCK
    wgh, wgl = _split2(wg)
    wgt = wg.T
    wgth, wgtl = _split2(wgt)
    G = N_GATE_COLS
    const = lambda i: (0, 0)
    return pl.pallas_call(
        _inproj_kernel,
        grid=(T // tm,),
        in_specs=[pl.BlockSpec((tm, D), lambda i: (i, 0)),
                  pl.BlockSpec((D, N), const),
                  pl.BlockSpec((D, G), const), pl.BlockSpec((D, G), const),
                  pl.BlockSpec((G, D), const), pl.BlockSpec((G, D), const)],
        out_specs=[pl.BlockSpec((tm, N), lambda i: (i, 0)),
                   pl.BlockSpec((tm, G), lambda i: (i, 0)),
                   pl.BlockSpec((G, tm), lambda i: (0, i))],
        out_shape=[jax.ShapeDtypeStruct((T, N), F32),
                   jax.ShapeDtypeStruct((T, G), F32),
                   jax.ShapeDtypeStruct((G, T), F32)],
        compiler_params=_cparams(("parallel",)),
        name="in_proj",
    )(h, w_main, wgh, wgl, wgth, wgtl)


def _mlstm_kernel(q_ref, k_ref, v_ref, o_ref, g_ref, gt_ref, brow_ref, bcol_ref, nw_ref,
                  y_ref, c_sc, m_sc):
    L = q_ref.shape[0]
    d = M_HEAD_DIM

    @pl.when(pl.program_id(1) == 0)
    def _():
        c_sc[...] = jnp.zeros_like(c_sc)
        m_sc[...] = jnp.full_like(m_sc, -jnp.inf)

    row = _iota2((L, L), 0)
    col = _iota2((L, L), 1)
    causal = col <= row
    tril = causal.astype(BF16)
    triu = (row <= col).astype(BF16)

    gb = g_ref[...] + brow_ref[...]
    gtb = gt_ref[...] + bcol_ref[...]
    b_cols = _exact_left(tril, _log_sigmoid(gb))
    b_rows = _exact_right(_log_sigmoid(gtb), triu)

    ones_col = (_iota2((L, d), 1) == 0).astype(F32)
    outs = []
    for hd in range(M_HEADS):
        sl = slice(hd * d, (hd + 1) * d)
        q = q_ref[:, sl].astype(BF16)
        kf = k_ref[:, sl] * (d ** -0.5)
        k = kf.astype(BF16)
        v_aug = jnp.concatenate([v_ref[:, sl], ones_col], axis=1).astype(BF16)
        bc = b_cols[:, M_HEADS + hd:M_HEADS + hd + 1]
        lic = gb[:, hd:hd + 1]
        br = b_rows[M_HEADS + hd:M_HEADS + hd + 1, :]
        lir = gtb[hd:hd + 1, :]
        b_end = bc[L - 1:L, :]
        m_prev = m_sc[hd]
        c_aug = c_sc[hd]

        src = lir - br
        log_w = jnp.where(causal, bc + src, -jnp.inf)
        m_intra = jnp.max(log_w, axis=1, keepdims=True)
        carry_log = bc + m_prev
        m_t = jnp.maximum(carry_log, m_intra)
        inter = jnp.exp(carry_log - m_t)
        dm = jnp.exp(log_w - m_t)
        s = _dot_nt(q, k) * dm
        num_aug = inter * _dot(q, c_aug.astype(BF16)) + _dot(s.astype(BF16), v_aug)
        num = num_aug[:, :d]
        den = num_aug[:, d:d + 1]
        hh = num / jnp.maximum(jnp.abs(den), jnp.exp(-m_t))

        m_end = jnp.max(b_end + src, axis=1, keepdims=True)
        m_new = jnp.maximum(b_end + m_prev, m_end)
        carry_scale = jnp.exp(b_end + m_prev - m_new)
        wk = (jnp.exp(b_end - bc + lic - m_new) * kf).astype(BF16)
        c_sc[hd] = carry_scale * c_aug + _dot_tn(wk, v_aug)
        m_sc[hd] = m_new

        hn = hh * lax.rsqrt(jnp.mean(hh * hh, axis=1, keepdims=True) + NORM_EPS)
        outs.append(hn)
    hcat = jnp.concatenate(outs, axis=1) * nw_ref[...]
    y_ref[...] = _sigmoid(o_ref[...]) * hcat


def _mlstm_call(u, g, gt, i_bias, f_bias, norm_w, B, S):
    T = u.shape[0]
    L = min(M_CHUNK, S)
    nc = S // L
    W = M_WIDTH
    zeros = jnp.zeros((N_GATE_COLS - 2 * M_HEADS,), F32)
    bias = jnp.concatenate([i_bias, f_bias, zeros])
    G = N_GATE_COLS
    rowblk = lambda c: pl.BlockSpec((L, W), lambda b, j, c=c: (b * nc + j, c))
    const = lambda b, j: (0, 0)
    return pl.pallas_call(
        _mlstm_kernel,
        grid=(B, nc),
        in_specs=[rowblk(0), rowblk(1), rowblk(2), rowblk(3),
                  pl.BlockSpec((L, G), lambda b, j: (b * nc + j, 0)),
                  pl.BlockSpec((G, L), lambda b, j: (0, b * nc + j)),
                  pl.BlockSpec((1, G), const), pl.BlockSpec((G, 1), const),
                  pl.BlockSpec((1, W), const)],
        out_specs=pl.BlockSpec((L, W), lambda b, j: (b * nc + j, 0)),
        out_shape=jax.ShapeDtypeStruct((T, W), F32),
        scratch_shapes=[pltpu.VMEM((M_HEADS, M_HEAD_DIM, 2 * M_HEAD_DIM), F32),
                        pltpu.VMEM((M_HEADS, 1, 1), F32)],
        compiler_params=_cparams(("parallel", "arbitrary")),
        name="mlstm",
    )(u, u, u, u, g, gt, bias.reshape(1, G), bias.reshape(G, 1), norm_w.reshape(1, W))


def _sb_kernel(q_ref, k_ref, v_ref, nw_ref, y_ref):
    tq = q_ref.shape[0]
    tk = tq
    d = SB_HEAD_DIM
    qi = pl.program_id(1)
    row = _iota2((tq, tk), 0)
    col = _iota2((tq, tk), 1)
    upper = (col < row).astype(BF16)
    qs = [(q_ref[:, hd * d:(hd + 1) * d] * (d ** -0.5)).astype(BF16) for hd in range(SB_HEADS)]

    def body(it, carry):
        kb = qi - it
        off = pl.multiple_of(kb * tk, tk)
        valid = (col + kb * tk) < (row + qi * tq)
        new = []
        for hd in range(SB_HEADS):
            acc, r = carry[hd]
            kblk = k_ref[pl.ds(off, tk), hd * d:(hd + 1) * d].astype(BF16)
            vblk = v_ref[pl.ds(off, tk), hd * d:(hd + 1) * d].astype(BF16)
            z = _dot_nt(qs[hd], kblk)
            sp = _softplus(z)
            lk = jnp.where(valid, -sp, 0.0)
            lh, ll = _split2(lk)
            cs = _dot(lh, upper) + _dot(ll, upper)
            a = jnp.where(valid, jnp.exp(z - sp + cs + r), 0.0)
            acc = acc + _dot(a.astype(BF16), vblk)
            r = r + (cs[:, 0:1] + lk[:, 0:1])
            new.append((acc, r))
        return tuple(new)

    init = tuple((jnp.zeros((tq, d), F32), jnp.zeros((tq, 1), F32)) for _ in range(SB_HEADS))
    res = lax.fori_loop(0, qi + 1, body, init)
    outs = []
    for hd in range(SB_HEADS):
        o = res[hd][0]
        outs.append(o * lax.rsqrt(jnp.mean(o * o, axis=1, keepdims=True) + NORM_EPS))
    y_ref[...] = jnp.concatenate(outs, axis=1) * nw_ref[...]


def _sb_call(u, norm_w, B, S, col0):
    T = u.shape[0]
    tq = min(SB_BLOCK, S)
    nq = S // tq
    W = SB_WIDTH
    return pl.pallas_call(
        _sb_kernel,
        grid=(B, nq),
        in_specs=[pl.BlockSpec((tq, W), lambda b, i: (b * nq + i, col0)),
                  pl.BlockSpec((S, W), lambda b, i: (b, col0 + 1)),
                  pl.BlockSpec((S, W), lambda b, i: (b, col0 + 2)),
                  pl.BlockSpec((1, W), lambda b, i: (0, 0))],
        out_specs=pl.BlockSpec((tq, W), lambda b, i: (b * nq + i, 0)),
        out_shape=jax.ShapeDtypeStruct((T, W), F32),
        compiler_params=_cparams(("parallel", "arbitrary")),
        name="stickbreak",
    )(u, u, u, norm_w.reshape(1, W))


def _gdn_kernel(q_ref, k_ref, v_ref, z_ref, g_ref, gt_ref, cw_ref, arow_ref, acol_ref,
                drow_ref, dcol_ref, nw_ref, y_ref, xq_sc, xk_sc, xv_sc, s_sc):
    LB = q_ref.shape[0]
    C = G_CHUNK
    d = G_HEAD_DIM
    W = G_WIDTH
    HALO = 8

    @pl.when(pl.program_id(1) == 0)
    def _():
        s_sc[...] = jnp.zeros_like(s_sc)
        for sc in (xq_sc, xk_sc, xv_sc):
            sc[0:HALO, :] = jnp.zeros((HALO, W), F32)

    @pl.when(pl.program_id(1) != 0)
    def _():
        for sc in (xq_sc, xk_sc, xv_sc):
            sc[0:HALO, :] = sc[LB:LB + HALO, :]

    def conv_silu(x_ref, sc, j):
        sc[HALO:HALO + LB, :] = x_ref[...]
        acc = None
        for t in range(G_CONV):
            w = cw_ref[t:t + 1, j * W:(j + 1) * W]
            term = w * sc[HALO - (G_CONV - 1) + t:HALO - (G_CONV - 1) + t + LB, :]
            acc = term if acc is None else acc + term
        return acc * _sigmoid(acc)

    qc = conv_silu(q_ref, xq_sc, 0)
    kc = conv_silu(k_ref, xk_sc, 1)
    vc = conv_silu(v_ref, xv_sc, 2)

    gb = g_ref[...]
    gtb = gt_ref[...]
    gdec_cols = -jnp.exp(arow_ref[...]) * _softplus(gb + drow_ref[...])
    gdec_rows = -jnp.exp(acol_ref[...]) * _softplus(gtb + dcol_ref[...])
    beta_cols = _sigmoid(gb)
    row = _iota2((LB, LB), 0)
    col = _iota2((LB, LB), 1)
    shift = C.bit_length() - 1
    same = jnp.right_shift(row, shift) == jnp.right_shift(col, shift)
    tril = (same & (col <= row)).astype(BF16)
    triu = (same & (row <= col)).astype(BF16)
    gam_cols = _exact_left(tril, gdec_cols)
    gam_rows = _exact_right(gdec_rows, triu)

    r64 = _iota2((C, C), 0)
    c64 = _iota2((C, C), 1)
    incl = c64 <= r64
    strict = c64 < r64
    sshift = G_SUB.bit_length() - 1
    diag_blk = jnp.right_shift(r64, sshift) == jnp.right_shift(c64, sshift)
    eye = (r64 == c64).astype(F32)

    outs = [[None] * (LB // C) for _ in range(G_HEADS)]
    for hd in range(G_HEADS):
        ls = slice(hd * d, (hd + 1) * d)
        qh = qc[:, ls]
        kh = kc[:, ls]
        qh = qh * lax.rsqrt(jnp.sum(qh * qh, axis=1, keepdims=True) + NORM_EPS) * (d ** -0.5)
        kh = kh * lax.rsqrt(jnp.sum(kh * kh, axis=1, keepdims=True) + NORM_EPS)
        vh = vc[:, ls]
        ga = 2 * G_HEADS + hd
        gbi = 3 * G_HEADS + hd
        state = s_sc[hd]
        for c in range(LB // C):
            rs = slice(c * C, (c + 1) * C)
            q = qh[rs]
            k = kh[rs]
            v = vh[rs]
            gam_c = gam_cols[rs, ga:ga + 1]
            gam_r = gam_rows[ga:ga + 1, rs]
            beta = beta_cols[rs, gbi:gbi + 1]
            gam_end = gam_c[C - 1:C, :]
            decay = jnp.exp(jnp.where(incl, gam_c - gam_r, -jnp.inf))
            kk = _dot3_nt(k, k)
            m = jnp.where(strict, beta * kk * decay, 0.0)
            md = jnp.where(diag_blk, m, 0.0)
            moff = m - md
            md2 = _dot3(md, md)
            md4 = _dot3(md2, md2)
            md8 = _dot3(md4, md4)
            td = _dot3(_dot3(_dot3(eye - md, eye + md2), eye + md4), eye + md8)
            n1 = _dot3(td, moff)
            n2 = _dot3(n1, n1)
            n3 = _dot3(n1, n2)
            tinv = _dot3(eye - n1 + n2 - n3, td)
            eg = jnp.exp(gam_c)
            u = _dot3(tinv, v * beta)
            w = _dot3(tinv, k * (beta * eg))
            qk = _dot_nt(q.astype(BF16), k.astype(BF16)) * decay
            q_dec = (q * eg).astype(BF16)
            k_dec = (k * jnp.exp(gam_end - gam_c)).astype(BF16)
            sb = state.astype(BF16)
            v_new = u - _dot(w.astype(BF16), sb)
            o = _dot(q_dec, sb) + _dot(qk.astype(BF16), v_new.astype(BF16))
            state = jnp.exp(gam_end) * state + _dot_tn(k_dec, v_new.astype(BF16))
            outs[hd][c] = o
        s_sc[hd] = state

    nw = nw_ref[...]
    cols = []
    for hd in range(G_HEADS):
        o = jnp.concatenate(outs[hd], axis=0)
        o = o * lax.rsqrt(jnp.mean(o * o, axis=1, keepdims=True) + NORM_EPS) * nw
        cols.append(o)
    zz = z_ref[...]
    y_ref[...] = jnp.concatenate(cols, axis=1) * (zz * _sigmoid(zz))


def _gdn_call(u, g, gt, conv_w, a_log, dt_bias, norm_w, B, S, col0):
    T = u.shape[0]
    LB = min(G_BLOCK, S)
    nb = S // LB
    W = G_WIDTH
    G = N_GATE_COLS
    pad = jnp.zeros((2 * G_HEADS,), F32)
    arow = jnp.concatenate([pad, a_log, jnp.zeros((G_HEADS,), F32)])
    drow = jnp.concatenate([pad, dt_bias, jnp.zeros((G_HEADS,), F32)])
    rowblk = lambda c: pl.BlockSpec((LB, W), lambda b, j, c=c: (b * nb + j, col0 + c))
    const = lambda b, j: (0, 0)
    return pl.pallas_call(
        _gdn_kernel,
        grid=(B, nb),
        in_specs=[rowblk(0), rowblk(1), rowblk(2), rowblk(3),
                  pl.BlockSpec((LB, G), lambda b, j: (b * nb + j, 0)),
                  pl.BlockSpec((G, LB), lambda b, j: (0, b * nb + j)),
                  pl.BlockSpec((G_CONV, 3 * W), const),
                  pl.BlockSpec((1, G), const), pl.BlockSpec((G, 1), const),
                  pl.BlockSpec((1, G), const), pl.BlockSpec((G, 1), const),
                  pl.BlockSpec((1, G_HEAD_DIM), const)],
        out_specs=pl.BlockSpec((LB, W), lambda b, j: (b * nb + j, 0)),
        out_shape=jax.ShapeDtypeStruct((T, W), F32),
        scratch_shapes=[pltpu.VMEM((LB + 8, W), F32), pltpu.VMEM((LB + 8, W), F32),
                        pltpu.VMEM((LB + 8, W), F32),
                        pltpu.VMEM((G_HEADS, G_HEAD_DIM, G_HEAD_DIM), F32)],
        compiler_params=_cparams(("parallel", "arbitrary")),
        name="gdn",
    )(u, u, u, u, g, gt, conv_w, arow.reshape(1, G), arow.reshape(G, 1),
      drow.reshape(1, G), drow.reshape(G, 1), norm_w.reshape(1, G_HEAD_DIM))


def _post_kernel(alpha, h_ref, ym_ref, ys_ref, yg_ref, p_ref, wo_ref, wpg_ref, wpp_ref,
                 wrh_ref, wrl_ref, br_ref, g1_ref, b1_ref,
                 xb_ref, base_ref, idx_ref, gate_ref, rank_ref, cnt_ref, cnt_sc):
    tm = h_ref.shape[0]
    E = br_ref.shape[1]

    @pl.when(pl.program_id(0) == 0)
    def _():
        cnt_sc[...] = jnp.zeros_like(cnt_sc)

    y = jnp.concatenate([ym_ref[...], ys_ref[...], yg_ref[...]], axis=1).astype(BF16)
    a = alpha * h_ref[...] + _dot(y, wo_ref[...])
    h1 = _layer_norm(a, g1_ref[...], b1_ref[...])
    h1h, h1l = _split2(h1)
    xb_ref[...] = h1h
    ple = _sigmoid(_dot(h1h, wpg_ref[...])) * _dot(p_ref[...].astype(BF16), wpp_ref[...])
    base_ref[...] = alpha * h1 + ple

    wrh, wrl = wrh_ref[...], wrl_ref[...]
    logits = _dot(h1h, wrh) + (_dot(h1h, wrl) + _dot(h1l, wrh)) + br_ref[...]
    lane = _iota2((tm, E), 1).astype(F32)
    work = logits
    vals, idxs = [], []
    anyhot = jnp.zeros((tm, E), F32)
    for _ in range(TOP_K):
        mx = jnp.max(work, axis=1, keepdims=True)
        ix = jnp.min(jnp.where(work == mx, lane, float(E)), axis=1, keepdims=True)
        sel = lane == ix
        vals.append(mx)
        idxs.append(ix)
        anyhot = jnp.where(sel, 1.0, anyhot)
        work = jnp.where(sel, -jnp.inf, work)
    ex = [jnp.exp(v - vals[0]) for v in vals]
    tot = ex[0] + ex[1] + ex[2] + ex[3]
    gates = [e / tot for e in ex]

    r = _iota2((tm, tm), 0)
    c = _iota2((tm, tm), 1)
    before = (c < r).astype(BF16)
    pos = cnt_sc[...] + _dot(before, anyhot.astype(BF16))
    ranks = [jnp.sum(jnp.where(lane == ix, pos, 0.0), axis=1, keepdims=True) for ix in idxs]
    cnt_sc[...] = cnt_sc[...] + jnp.sum(anyhot, axis=0, keepdims=True)
    cnt_ref[...] = cnt_sc[...].astype(I32)

    kl = _iota2((tm, TOP_K), 1)

    def pack(cols):
        out = jnp.broadcast_to(cols[0], (tm, TOP_K))
        for k in range(1, TOP_K):
            out = jnp.where(kl == k, cols[k], out)
        return out

    idx_ref[...] = pack(idxs).astype(I32)
    gate_ref[...] = pack(gates)
    rank_ref[...] = pack(ranks).astype(I32)


def _post_call(alpha, h, ym, ys, yg, p, w_out, w_pg, w_pp, w_router, b_router, ln_g, ln_b):
    T, D = h.shape
    tm = ROW_BLOCK
    E = w_router.shape[1]
    P = p.shape[1]
    wrh, wrl = _split2(w_router)
    const = lambda i: (0, 0)
    rows = lambda w: pl.BlockSpec((tm, w), lambda i: (i, 0))
    return pl.pallas_call(
        functools.partial(_post_kernel, alpha),
        grid=(T // tm,),
        in_specs=[rows(D), rows(M_WIDTH), rows(SB_WIDTH), rows(G_WIDTH), rows(P),
                  pl.BlockSpec(w_out.shape, const), pl.BlockSpec((D, D), const),
                  pl.BlockSpec((P, D), const),
                  pl.BlockSpec((D, E), const), pl.BlockSpec((D, E), const),
                  pl.BlockSpec((1, E), const), pl.BlockSpec((1, D), const),
                  pl.BlockSpec((1, D), const)],
        out_specs=[rows(D), rows(D), rows(TOP_K), rows(TOP_K), rows(TOP_K),
                   pl.BlockSpec((1, E), const)],
        out_shape=[jax.ShapeDtypeStruct((T, D), BF16), jax.ShapeDtypeStruct((T, D), F32),
                   jax.ShapeDtypeStruct((T, TOP_K), I32), jax.ShapeDtypeStruct((T, TOP_K), F32),
                   jax.ShapeDtypeStruct((T, TOP_K), I32), jax.ShapeDtypeStruct((1, E), I32)],
        scratch_shapes=[pltpu.VMEM((1, E), F32)],
        compiler_params=_cparams(("arbitrary",)),
        name="post_mix",
    )(h, ym, ys, yg, p, w_out, w_pg, w_pp, wrh, wrl, b_router.reshape(1, E),
      ln_g.reshape(1, D), ln_b.reshape(1, D))


def _moe_kernel(be_ref, nb_ref, x_ref, wgu_ref, bgu_ref, wd_ref, bd_ref, y_ref):
    i = pl.program_id(0)
    F = wd_ref.shape[1]

    @pl.when(i < nb_ref[0])
    def _():
        h = _dot(x_ref[...], wgu_ref[0]) + bgu_ref[0]
        gate = jnp.minimum(h[:, :F], SWIGLU_LIMIT)
        up = jnp.clip(h[:, F:], -SWIGLU_LIMIT, SWIGLU_LIMIT)
        act = (up + 1.0) * gate * _sigmoid(SWIGLU_ALPHA * gate)
        y_ref[...] = _dot(act.astype(BF16), wd_ref[0]) + bd_ref[0]

    @pl.when(i >= nb_ref[0])
    def _():
        y_ref[...] = jnp.zeros_like(y_ref)


def _moe_call(block_e, n_used, xs, w_gu, b_gu, w_down, b_down):
    P, D = xs.shape
    E, _, F2 = w_gu.shape
    F = F2 // 2
    tm = MOE_BLOCK
    nblk = P // tm
    grid_spec = pltpu.PrefetchScalarGridSpec(
        num_scalar_prefetch=2,
        grid=(nblk,),
        in_specs=[pl.BlockSpec((tm, D), lambda i, be, nb: (i, 0)),
                  pl.BlockSpec((1, D, F2), lambda i, be, nb: (be[i], 0, 0)),
                  pl.BlockSpec((1, 1, F2), lambda i, be, nb: (be[i], 0, 0)),
                  pl.BlockSpec((1, F, D), lambda i, be, nb: (be[i], 0, 0)),
                  pl.BlockSpec((1, 1, D), lambda i, be, nb: (be[i], 0, 0))],
        out_specs=pl.BlockSpec((tm, D), lambda i, be, nb: (i, 0)),
    )
    return pl.pallas_call(
        _moe_kernel,
        grid_spec=grid_spec,
        out_shape=jax.ShapeDtypeStruct((P, D), F32),
        compiler_params=_cparams(("arbitrary",)),
        name="moe_experts",
    )(block_e, n_used, xs, w_gu, b_gu.reshape(E, 1, F2), w_down, b_down.reshape(E, 1, D))


def _final_kernel(base_ref, gate_ref, y0_ref, y1_ref, y2_ref, y3_ref, g_ref, b_ref, o_ref):
    gate = gate_ref[...]
    acc = base_ref[...]
    for k, yr in enumerate((y0_ref, y1_ref, y2_ref, y3_ref)):
        acc = acc + gate[:, k:k + 1] * yr[...]
    o_ref[...] = _layer_norm(acc, g_ref[...], b_ref[...])


def _final_call(base, gates, ys, ln_g, ln_b):
    T, D = base.shape
    tm = ROW_BLOCK
    rows = lambda w: pl.BlockSpec((tm, w), lambda i: (i, 0))
    const = lambda i: (0, 0)
    return pl.pallas_call(
        _final_kernel,
        grid=(T // tm,),
        in_specs=[rows(D), rows(TOP_K), rows(D), rows(D), rows(D), rows(D),
                  pl.BlockSpec((1, D), const), pl.BlockSpec((1, D), const)],
        out_specs=rows(D),
        out_shape=jax.ShapeDtypeStruct((T, D), F32),
        compiler_params=_cparams(("parallel",)),
        name="combine_ln",
    )(base, gates, *ys, ln_g.reshape(1, D), ln_b.reshape(1, D))


def _moe_ffn(xb, base, idx, gates, rank, counts, w_gu, b_gu, w_down, b_down, ln_g, ln_b):
    T, D = xb.shape
    E = w_gu.shape[0]
    A = T * TOP_K
    counts = counts.reshape(E)
    padded = (counts + MOE_BLOCK - 1) // MOE_BLOCK * MOE_BLOCK
    pad_end = jnp.cumsum(padded)
    pad_start = pad_end - padded
    n_blocks = -(-A // MOE_BLOCK) + E
    P = n_blocks * MOE_BLOCK
    dest = pad_start[idx] + rank
    tok = jnp.broadcast_to(jnp.arange(T, dtype=I32)[:, None], (T, TOP_K))
    slot_tok = jnp.zeros((P,), I32).at[dest.reshape(A)].set(tok.reshape(A))
    block_e = jnp.minimum(
        jnp.searchsorted(pad_end, jnp.arange(n_blocks, dtype=I32) * MOE_BLOCK, side='right'),
        E - 1).astype(I32)
    n_used = (pad_end[-1] // MOE_BLOCK).astype(I32).reshape(1)
    xs = jnp.take(xb, slot_tok, axis=0)
    y = _moe_call(block_e, n_used, xs, w_gu, b_gu, w_down, b_down)
    ys = [jnp.take(y, dest[:, k], axis=0) for k in range(TOP_K)]
    return _final_call(base, gates, ys, ln_g, ln_b)


def _layer(h, p_i, B, S, alpha, w_main, wg, m_i_bias, m_f_bias, m_norm_w, sb_norm_w, g_conv_w,
           g_A_log, g_dt_bias, g_norm_w, w_out, ln1_g, ln1_b, w_router, b_router, w_gu, b_gu,
           w_down, b_down, w_pg, w_pp, ln2_g, ln2_b):
    u, g, gt = _inproj_call(h, w_main, wg)
    ym = _mlstm_call(u, g, gt, m_i_bias, m_f_bias, m_norm_w, B, S)
    ys = _sb_call(u, sb_norm_w, B, S, (4 * M_WIDTH + 4 * G_WIDTH) // SB_WIDTH)
    yg = _gdn_call(u, g, gt, g_conv_w, g_A_log, g_dt_bias, g_norm_w, B, S, 4 * M_WIDTH // G_WIDTH)
    xb, base, idx, gates, rank, counts = _post_call(
        alpha, h, ym, ys, yg, p_i, w_out, w_pg, w_pp, w_router, b_router, ln1_g, ln1_b)
    return _moe_ffn(xb, base, idx, gates, rank, counts, w_gu, b_gu, w_down, b_down, ln2_g, ln2_b)


def kernel(x, p, ln0_g, ln0_b, w_in, m_i_bias, m_f_bias, m_norm_w, sb_norm_w, g_conv_w, g_A_log,
           g_dt_bias, g_norm_w, w_out, ln1_g, ln1_b, w_router, b_router, w_gu, b_gu, w_down,
           b_down, w_ple_gate, w_ple_proj, ln2_g, ln2_b):
    B, S, D = x.shape
    depth = w_in.shape[0]
    T = B * S
    alpha = (2 * depth) ** 0.25
    g0 = 4 * M_WIDTH
    g1 = g0 + 2 * M_HEADS
    s1 = g1 + 3 * SB_WIDTH
    g2 = s1 + 4 * G_WIDTH
    h = _ln_call(x.reshape(T, D), ln0_g, ln0_b)
    for i in range(depth):
        w = w_in[i]
        w_main = jnp.concatenate([w[:, :g0], w[:, s1:g2], w[:, g1:s1]], axis=1).astype(BF16)
        wg = jnp.concatenate([w[:, g0:g1], w[:, g2:]], axis=1)
        h = _layer(h, p[i].reshape(T, -1), B, S, alpha, w_main, wg, m_i_bias[i], m_f_bias[i],
                   m_norm_w[i], sb_norm_w[i], g_conv_w[i], g_A_log[i], g_dt_bias[i], g_norm_w[i],
                   w_out[i].astype(BF16), ln1_g[i], ln1_b[i], w_router[i], b_router[i],
                   w_gu[i].astype(BF16), b_gu[i], w_down[i].astype(BF16), b_down[i],
                   w_ple_gate[i].astype(BF16), w_ple_proj[i].astype(BF16), ln2_g[i], ln2_b[i])
    return h.reshape(B, S, D)
```

```python
import functools
import math

import jax
import jax.numpy as jnp
from jax import lax
from jax.experimental import pallas as pl
from jax.experimental.pallas import tpu as pltpu

F32 = jnp.float32
BF16 = jnp.bfloat16
I32 = jnp.int32

M_HEADS, M_HEAD_DIM = 4, 64
SB_HEADS, SB_HEAD_DIM = 4, 64
G_HEADS, G_HEAD_DIM = 4, 128
G_CONV = 4
M_WIDTH = M_HEADS * M_HEAD_DIM
SB_WIDTH = SB_HEADS * SB_HEAD_DIM
G_WIDTH = G_HEADS * G_HEAD_DIM
N_GATE_COLS = 16
TOP_K = 4
SWIGLU_LIMIT = 7.0
SWIGLU_ALPHA = 1.702
LN_EPS = 1e-5
NORM_EPS = 1e-6

ROW_BLOCK = 256
M_CHUNK = 256
SB_BLOCK = 256
G_CHUNK = 64
G_BLOCK = 256
G_SUB = 16
MOE_BLOCK = 256
VMEM_LIMIT = 56 * 1024 * 1024


def _cparams(sem):
    return pltpu.CompilerParams(dimension_semantics=sem, vmem_limit_bytes=VMEM_LIMIT)


def _split3(x):
    hi = x.astype(BF16)
    r1 = x - hi.astype(F32)
    mid = r1.astype(BF16)
    lo = (r1 - mid.astype(F32)).astype(BF16)
    return hi, mid, lo


def _split2(x):
    hi = x.astype(BF16)
    lo = (x - hi.astype(F32)).astype(BF16)
    return hi, lo


def _dot(a, b):
    return jnp.dot(a, b, preferred_element_type=F32)


def _dot_nt(a, b):
    return lax.dot_general(a, b, (((1,), (1,)), ((), ())), preferred_element_type=F32)


def _dot_tn(a, b):
    return lax.dot_general(a, b, (((0,), (0,)), ((), ())), preferred_element_type=F32)


def _dot3(a, b):
    ah, al = _split2(a)
    bh, bl = _split2(b)
    return _dot(ah, bh) + (_dot(ah, bl) + _dot(al, bh))


def _dot3_nt(a, b):
    ah, al = _split2(a)
    bh, bl = _split2(b)
    return _dot_nt(ah, bh) + (_dot_nt(ah, bl) + _dot_nt(al, bh))


def _bdot(a, b):
    return lax.dot_general(a, b, (((2,), (1,)), ((0,), (0,))), preferred_element_type=F32)


def _bdot_nt(a, b):
    return lax.dot_general(a, b, (((2,), (2,)), ((0,), (0,))), preferred_element_type=F32)


def _bdot_tn(a, b):
    return lax.dot_general(a, b, (((1,), (1,)), ((0,), (0,))), preferred_element_type=F32)


def _bdot3(a, b):
    ah, al = _split2(a)
    bh, bl = _split2(b)
    return _bdot(ah, bh) + (_bdot(ah, bl) + _bdot(al, bh))


def _bdot3_nt(a, b):
    ah, al = _split2(a)
    bh, bl = _split2(b)
    return _bdot_nt(ah, bh) + (_bdot_nt(ah, bl) + _bdot_nt(al, bh))


def _exact_left(mask_bf16, x):
    hi, mid, lo = _split3(x)
    return _dot(mask_bf16, hi) + (_dot(mask_bf16, mid) + _dot(mask_bf16, lo))


def _exact_right(x, mask_bf16):
    hi, mid, lo = _split3(x)
    return _dot(hi, mask_bf16) + (_dot(mid, mask_bf16) + _dot(lo, mask_bf16))


def _softplus(x):
    return jnp.maximum(x, 0.0) + jnp.log1p(jnp.exp(-jnp.abs(x)))


def _log_sigmoid(x):
    return -_softplus(-x)


def _sigmoid(x):
    return 1.0 / (1.0 + jnp.exp(-x))


def _layer_norm(x, g, b):
    mu = jnp.mean(x, axis=-1, keepdims=True)
    xc = x - mu
    var = jnp.mean(xc * xc, axis=-1, keepdims=True)
    return xc * lax.rsqrt(var + LN_EPS) * g + b


def _iota2(shape, dim):
    return lax.broadcasted_iota(I32, shape, dim)


def _ln_kernel(x_ref, g_ref, b_ref, o_ref):
    o_ref[...] = _layer_norm(x_ref[...], g_ref[...], b_ref[...])


def _ln_call(x, g, b):
    T, D = x.shape
    tm = ROW_BLOCK
    return pl.pallas_call(
        _ln_kernel,
        grid=(T // tm,),
        in_specs=[pl.BlockSpec((tm, D), lambda i: (i, 0)),
                  pl.BlockSpec((1, D), lambda i: (0, 0)),
                  pl.BlockSpec((1, D), lambda i: (0, 0))],
        out_specs=pl.BlockSpec((tm, D), lambda i: (i, 0)),
        out_shape=jax.ShapeDtypeStruct((T, D), F32),
        compiler_params=_cparams(("parallel",)),
        name="embed_ln",
    )(x, g.reshape(1, D), b.reshape(1, D))


def _inproj_kernel(h_ref, w_ref, wgh_ref, wgl_ref, wgth_ref, wgtl_ref, u_ref, g_ref, gt_ref):
    h = h_ref[...]
    hh, hl = _split2(h)
    u_ref[...] = _dot(hh, w_ref[...])
    wgh, wgl = wgh_ref[...], wgl_ref[...]
    g_ref[...] = _dot(hh, wgh) + (_dot(hh, wgl) + _dot(hl, wgh))
    wgth, wgtl = wgth_ref[...], wgtl_ref[...]
    gt_ref[...] = _dot_nt(wgth, hh) + (_dot_nt(wgtl, hh) + _dot_nt(wgth, hl))


def _inproj_call(h, w_main, wg):
    T, D = h.shape
    N = w_main.shape[1]
    tm = ROW_BLOCK
    wgh, wgl = _split2(wg)
    wgt = wg.T
    wgth, wgtl = _split2(wgt)
    G = N_GATE_COLS
    const = lambda i: (0, 0)
    return pl.pallas_call(
        _inproj_kernel,
        grid=(T // tm,),
        in_specs=[pl.BlockSpec((tm, D), lambda i: (i, 0)),
                  pl.BlockSpec((D, N), const),
                  pl.BlockSpec((D, G), const), pl.BlockSpec((D, G), const),
                  pl.BlockSpec((G, D), const), pl.BlockSpec((G, D), const)],
        out_specs=[pl.BlockSpec((tm, N), lambda i: (i, 0)),
                   pl.BlockSpec((tm, G), lambda i: (i, 0)),
                   pl.BlockSpec((G, tm), lambda i: (0, i))],
        out_shape=[jax.ShapeDtypeStruct((T, N), F32),
                   jax.ShapeDtypeStruct((T, G), F32),
                   jax.ShapeDtypeStruct((G, T), F32)],
        compiler_params=_cparams(("parallel",)),
        name="in_proj",
    )(h, w_main, wgh, wgl, wgth, wgtl)


def _mlstm_kernel(q_ref, k_ref, v_ref, o_ref, g_ref, gt_ref, brow_ref, bcol_ref, nw_ref,
                  y_ref, c_sc, m_sc):
    L = q_ref.shape[0]
    d = M_HEAD_DIM

    @pl.when(pl.program_id(1) == 0)
    def _():
        c_sc[...] = jnp.zeros_like(c_sc)
        m_sc[...] = jnp.full_like(m_sc, -jnp.inf)

    row = _iota2((L, L), 0)
    col = _iota2((L, L), 1)
    causal = col <= row
    tril = causal.astype(BF16)
    triu = (row <= col).astype(BF16)

    gb = g_ref[...] + brow_ref[...]
    gtb = gt_ref[...] + bcol_ref[...]
    b_cols = _exact_left(tril, _log_sigmoid(gb))
    b_rows = _exact_right(_log_sigmoid(gtb), triu)

    ones_col = (_iota2((L, d), 1) == 0).astype(F32)
    outs = []
    for hd in range(M_HEADS):
        sl = slice(hd * d, (hd + 1) * d)
        q = q_ref[:, sl].astype(BF16)
        kf = k_ref[:, sl] * (d ** -0.5)
        k = kf.astype(BF16)
        v_aug = jnp.concatenate([v_ref[:, sl], ones_col], axis=1).astype(BF16)
        bc = b_cols[:, M_HEADS + hd:M_HEADS + hd + 1]
        lic = gb[:, hd:hd + 1]
        br = b_rows[M_HEADS + hd:M_HEADS + hd + 1, :]
        lir = gtb[hd:hd + 1, :]
        b_end = bc[L - 1:L, :]
        m_prev = m_sc[hd]
        c_aug = c_sc[hd]

        src = lir - br
        log_w = jnp.where(causal, bc + src, -jnp.inf)
        m_intra = jnp.max(log_w, axis=1, keepdims=True)
        carry_log = bc + m_prev
        m_t = jnp.maximum(carry_log, m_intra)
        inter = jnp.exp(carry_log - m_t)
        dm = jnp.exp(log_w - m_t)
        s = _dot_nt(q, k) * dm
        num_aug = inter * _dot(q, c_aug.astype(BF16)) + _dot(s.astype(BF16), v_aug)
        num = num_aug[:, :d]
        den = num_aug[:, d:d + 1]
        hh = num / jnp.maximum(jnp.abs(den), jnp.exp(-m_t))

        m_end = jnp.max(b_end + src, axis=1, keepdims=True)
        m_new = jnp.maximum(b_end + m_prev, m_end)
        carry_scale = jnp.exp(b_end + m_prev - m_new)
        wk = (jnp.exp(b_end - bc + lic - m_new) * kf).astype(BF16)
        c_sc[hd] = carry_scale * c_aug + _dot_tn(wk, v_aug)
        m_sc[hd] = m_new

        hn = hh * lax.rsqrt(jnp.mean(hh * hh, axis=1, keepdims=True) + NORM_EPS)
        outs.append(hn)
    hcat = jnp.concatenate(outs, axis=1) * nw_ref[...]
    y_ref[...] = _sigmoid(o_ref[...]) * hcat


def _mlstm_call(u, g, gt, i_bias, f_bias, norm_w, B, S):
    T = u.shape[0]
    L = min(M_CHUNK, S)
    nc = S // L
    W = M_WIDTH
    zeros = jnp.zeros((N_GATE_COLS - 2 * M_HEADS,), F32)
    bias = jnp.concatenate([i_bias, f_bias, zeros])
    G = N_GATE_COLS
    rowblk = lambda c: pl.BlockSpec((L, W), lambda b, j, c=c: (b * nc + j, c))
    const = lambda b, j: (0, 0)
    return pl.pallas_call(
        _mlstm_kernel,
        grid=(B, nc),
        in_specs=[rowblk(0), rowblk(1), rowblk(2), rowblk(3),
                  pl.BlockSpec((L, G), lambda b, j: (b * nc + j, 0)),
                  pl.BlockSpec((G, L), lambda b, j: (0, b * nc + j)),
                  pl.BlockSpec((1, G), const), pl.BlockSpec((G, 1), const),
                  pl.BlockSpec((1, W), const)],
        out_specs=pl.BlockSpec((L, W), lambda b, j: (b * nc + j, 0)),
        out_shape=jax.ShapeDtypeStruct((T, W), F32),
        scratch_shapes=[pltpu.VMEM((M_HEADS, M_HEAD_DIM, 2 * M_HEAD_DIM), F32),
                        pltpu.VMEM((M_HEADS, 1, 1), F32)],
        compiler_params=_cparams(("parallel", "arbitrary")),
        name="mlstm",
    )(u, u, u, u, g, gt, bias.reshape(1, G), bias.reshape(G, 1), norm_w.reshape(1, W))


def _sb_kernel(q_ref, k_ref, v_ref, nw_ref, y_ref):
    tq = q_ref.shape[0]
    tk = tq
    d = SB_HEAD_DIM
    qi = pl.program_id(1)
    row = _iota2((tq, tk), 0)
    col = _iota2((tq, tk), 1)
    upper = (col < row).astype(BF16)
    qs = [(q_ref[:, hd * d:(hd + 1) * d] * (d ** -0.5)).astype(BF16) for hd in range(SB_HEADS)]

    def body(it, carry):
        kb = qi - it
        off = pl.multiple_of(kb * tk, tk)
        valid = (col + kb * tk) < (row + qi * tq)
        new = []
        for hd in range(SB_HEADS):
            acc, r = carry[hd]
            kblk = k_ref[pl.ds(off, tk), hd * d:(hd + 1) * d].astype(BF16)
            vblk = v_ref[pl.ds(off, tk), hd * d:(hd + 1) * d].astype(BF16)
            z = _dot_nt(qs[hd], kblk)
            sp = _softplus(z)
            lk = jnp.where(valid, -sp, 0.0)
            lh, ll = _split2(lk)
            cs = _dot(lh, upper) + _dot(ll, upper)
            a = jnp.where(valid, jnp.exp(z - sp + cs + r), 0.0)
            acc = acc + _dot(a.astype(BF16), vblk)
            r = r + (cs[:, 0:1] + lk[:, 0:1])
            new.append((acc, r))
        return tuple(new)

    init = tuple((jnp.zeros((tq, d), F32), jnp.zeros((tq, 1), F32)) for _ in range(SB_HEADS))
    res = lax.fori_loop(0, qi + 1, body, init)
    outs = []
    for hd in range(SB_HEADS):
        o = res[hd][0]
        outs.append(o * lax.rsqrt(jnp.mean(o * o, axis=1, keepdims=True) + NORM_EPS))
    y_ref[...] = jnp.concatenate(outs, axis=1) * nw_ref[...]


def _sb_call(u, norm_w, B, S, col0):
    T = u.shape[0]
    tq = min(SB_BLOCK, S)
    nq = S // tq
    W = SB_WIDTH
    return pl.pallas_call(
        _sb_kernel,
        grid=(B, nq),
        in_specs=[pl.BlockSpec((tq, W), lambda b, i: (b * nq + i, col0)),
                  pl.BlockSpec((S, W), lambda b, i: (b, col0 + 1)),
                  pl.BlockSpec((S, W), lambda b, i: (b, col0 + 2)),
                  pl.BlockSpec((1, W), lambda b, i: (0, 0))],
        out_specs=pl.BlockSpec((tq, W), lambda b, i: (b * nq + i, 0)),
        out_shape=jax.ShapeDtypeStruct((T, W), F32),
        compiler_params=_cparams(("parallel", "arbitrary")),
        name="stickbreak",
    )(u, u, u, norm_w.reshape(1, W))


def _gdn_kernel(q_ref, k_ref, v_ref, z_ref, g_ref, gt_ref, cw_ref, arow_ref, acol_ref,
                drow_ref, dcol_ref, nw_ref, y_ref, xq_sc, xk_sc, xv_sc, s_sc):
    LB = q_ref.shape[0]
    C = G_CHUNK
    d = G_HEAD_DIM
    W = G_WIDTH
    HALO = 8

    @pl.when(pl.program_id(1) == 0)
    def _():
        s_sc[...] = jnp.zeros_like(s_sc)
        for sc in (xq_sc, xk_sc, xv_sc):
            sc[0:HALO, :] = jnp.zeros((HALO, W), F32)

    @pl.when(pl.program_id(1) != 0)
    def _():
        for sc in (xq_sc, xk_sc, xv_sc):
            sc[0:HALO, :] = sc[LB:LB + HALO, :]

    def conv_silu(x_ref, sc, j):
        sc[HALO:HALO + LB, :] = x_ref[...]
        acc = None
        for t in range(G_CONV):
            w = cw_ref[t:t + 1, j * W:(j + 1) * W]
            term = w * sc[HALO - (G_CONV - 1) + t:HALO - (G_CONV - 1) + t + LB, :]
            acc = term if acc is None else acc + term
        return acc * _sigmoid(acc)

    qc = conv_silu(q_ref, xq_sc, 0)
    kc = conv_silu(k_ref, xk_sc, 1)
    vc = conv_silu(v_ref, xv_sc, 2)

    gb = g_ref[...]
    gtb = gt_ref[...]
    gdec_cols = -jnp.exp(arow_ref[...]) * _softplus(gb + drow_ref[...])
    gdec_rows = -jnp.exp(acol_ref[...]) * _softplus(gtb + dcol_ref[...])
    beta_cols = _sigmoid(gb)
    row = _iota2((LB, LB), 0)
    col = _iota2((LB, LB), 1)
    shift = C.bit_length() - 1
    same = jnp.right_shift(row, shift) == jnp.right_shift(col, shift)
    tril = (same & (col <= row)).astype(BF16)
    triu = (same & (row <= col)).astype(BF16)
    gam_cols = _exact_left(tril, gdec_cols)
    gam_rows = _exact_right(gdec_rows, triu)

    nc = LB // C
    NB = G_HEADS * nc

    def per_head(x):
        return jnp.stack([x[:, h * d:(h + 1) * d] for h in range(G_HEADS)], axis=0).reshape(NB, C, d)

    q3 = per_head(qc)
    k3 = per_head(kc)
    v3 = per_head(vc)
    q3 = q3 * lax.rsqrt(jnp.sum(q3 * q3, axis=2, keepdims=True) + NORM_EPS) * (d ** -0.5)
    k3 = k3 * lax.rsqrt(jnp.sum(k3 * k3, axis=2, keepdims=True) + NORM_EPS)
    ga0 = 2 * G_HEADS
    gb0 = 3 * G_HEADS
    gam_c = jnp.stack([gam_cols[:, ga0 + h:ga0 + h + 1] for h in range(G_HEADS)], 0).reshape(NB, C, 1)
    beta = jnp.stack([beta_cols[:, gb0 + h:gb0 + h + 1] for h in range(G_HEADS)], 0).reshape(NB, C, 1)
    gam_r = jnp.stack([gam_rows[ga0 + h:ga0 + h + 1, c * C:(c + 1) * C]
                       for h in range(G_HEADS) for c in range(nc)], 0)
    gam_end = gam_c[:, C - 1:C, :]

    r64 = _iota2((NB, C, C), 1)
    c64 = _iota2((NB, C, C), 2)
    incl = c64 <= r64
    strict = c64 < r64
    sshift = G_SUB.bit_length() - 1
    diag_blk = jnp.right_shift(r64, sshift) == jnp.right_shift(c64, sshift)
    eye = (r64 == c64).astype(F32)

    decay = jnp.exp(jnp.where(incl, gam_c - gam_r, -jnp.inf))
    kk = _bdot3_nt(k3, k3)
    m = jnp.where(strict, beta * kk * decay, 0.0)
    md = jnp.where(diag_blk, m, 0.0)
    moff = m - md
    md2 = _bdot3(md, md)
    md4 = _bdot3(md2, md2)
    md8 = _bdot3(md4, md4)
    td = _bdot3(_bdot3(_bdot3(eye - md, eye + md2), eye + md4), eye + md8)
    n1 = _bdot3(td, moff)
    n2 = _bdot3(n1, n1)
    n3 = _bdot3(n1, n2)
    tinv = _bdot3(eye - n1 + n2 - n3, td)
    eg = jnp.exp(gam_c)
    u = _bdot3(tinv, v3 * beta)
    w = _bdot3(tinv, k3 * (beta * eg)).astype(BF16)
    qk = (_bdot_nt(q3.astype(BF16), k3.astype(BF16)) * decay).astype(BF16)
    q_dec = (q3 * eg).astype(BF16)
    k_dec = (k3 * jnp.exp(gam_end - gam_c)).astype(BF16)
    cdec = jnp.exp(gam_end)

    def chunk(x, c):
        return x.reshape((G_HEADS, nc) + x.shape[1:])[:, c]

    state = s_sc[...]
    o_chunks = []
    for c in range(nc):
        sb = state.astype(BF16)
        v_new = chunk(u, c) - _bdot(chunk(w, c), sb)
        vb = v_new.astype(BF16)
        o_chunks.append(_bdot(chunk(q_dec, c), sb) + _bdot(chunk(qk, c), vb))
        state = chunk(cdec, c) * state + _bdot_tn(chunk(k_dec, c), vb)
    s_sc[...] = state

    nw = nw_ref[...]
    o = jnp.concatenate(o_chunks, axis=1)
    o = o * lax.rsqrt(jnp.mean(o * o, axis=2, keepdims=True) + NORM_EPS) * nw
    zz = z_ref[...]
    y_ref[...] = jnp.concatenate([o[h] for h in range(G_HEADS)], axis=1) * (zz * _sigmoid(zz))


def _gdn_call(u, g, gt, conv_w, a_log, dt_bias, norm_w, B, S, col0):
    T = u.shape[0]
    LB = min(G_BLOCK, S)
    nb = S // LB
    W = G_WIDTH
    G = N_GATE_COLS
    pad = jnp.zeros((2 * G_HEADS,), F32)
    arow = jnp.concatenate([pad, a_log, jnp.zeros((G_HEADS,), F32)])
    drow = jnp.concatenate([pad, dt_bias, jnp.zeros((G_HEADS,), F32)])
    rowblk = lambda c: pl.BlockSpec((LB, W), lambda b, j, c=c: (b * nb + j, col0 + c))
    const = lambda b, j: (0, 0)
    return pl.pallas_call(
        _gdn_kernel,
        grid=(B, nb),
        in_specs=[rowblk(0), rowblk(1), rowblk(2), rowblk(3),
                  pl.BlockSpec((LB, G), lambda b, j: (b * nb + j, 0)),
                  pl.BlockSpec((G, LB), lambda b, j: (0, b * nb + j)),
                  pl.BlockSpec((G_CONV, 3 * W), const),
                  pl.BlockSpec((1, G), const), pl.BlockSpec((G, 1), const),
                  pl.BlockSpec((1, G), const), pl.BlockSpec((G, 1), const),
                  pl.BlockSpec((1, G_HEAD_DIM), const)],
        out_specs=pl.BlockSpec((LB, W), lambda b, j: (b * nb + j, 0)),
        out_shape=jax.ShapeDtypeStruct((T, W), F32),
        scratch_shapes=[pltpu.VMEM((LB + 8, W), F32), pltpu.VMEM((LB + 8, W), F32),
                        pltpu.VMEM((LB + 8, W), F32),
                        pltpu.VMEM((G_HEADS, G_HEAD_DIM, G_HEAD_DIM), F32)],
        compiler_params=_cparams(("parallel", "arbitrary")),
        name="gdn",
    )(u, u, u, u, g, gt, conv_w, arow.reshape(1, G), arow.reshape(G, 1),
      drow.reshape(1, G), drow.reshape(G, 1), norm_w.reshape(1, G_HEAD_DIM))


def _post_kernel(alpha, h_ref, ym_ref, ys_ref, yg_ref, wo_ref,
                 wrh_ref, wrl_ref, br_ref, g1_ref, b1_ref,
                 h1_ref, idx_ref, gate_ref, rank_ref, cnt_ref, cnt_sc):
    tm = h_ref.shape[0]
    E = br_ref.shape[1]

    @pl.when(pl.program_id(0) == 0)
    def _():
        cnt_sc[...] = jnp.zeros_like(cnt_sc)

    y = jnp.concatenate([ym_ref[...], ys_ref[...], yg_ref[...]], axis=1).astype(BF16)
    a = alpha * h_ref[...] + _dot(y, wo_ref[...])
    h1 = _layer_norm(a, g1_ref[...], b1_ref[...])
    h1h, h1l = _split2(h1)
    h1_ref[...] = h1

    wrh, wrl = wrh_ref[...], wrl_ref[...]
    logits = _dot(h1h, wrh) + (_dot(h1h, wrl) + _dot(h1l, wrh)) + br_ref[...]
    lane = _iota2((tm, E), 1).astype(F32)
    work = logits
    vals, idxs = [], []
    anyhot = jnp.zeros((tm, E), F32)
    for _ in range(TOP_K):
        mx = jnp.max(work, axis=1, keepdims=True)
        ix = jnp.min(jnp.where(work == mx, lane, float(E)), axis=1, keepdims=True)
        sel = lane == ix
        vals.append(mx)
        idxs.append(ix)
        anyhot = jnp.where(sel, 1.0, anyhot)
        work = jnp.where(sel, -jnp.inf, work)
    ex = [jnp.exp(v - vals[0]) for v in vals]
    tot = ex[0] + ex[1] + ex[2] + ex[3]
    gates = [e / tot for e in ex]

    r = _iota2((tm, tm), 0)
    c = _iota2((tm, tm), 1)
    before = (c < r).astype(BF16)
    pos = cnt_sc[...] + _dot(before, anyhot.astype(BF16))
    ranks = [jnp.sum(jnp.where(lane == ix, pos, 0.0), axis=1, keepdims=True) for ix in idxs]
    cnt_sc[...] = cnt_sc[...] + jnp.sum(anyhot, axis=0, keepdims=True)
    cnt_ref[...] = cnt_sc[...].astype(I32)

    kl = _iota2((tm, TOP_K), 1)

    def pack(cols):
        out = jnp.broadcast_to(cols[0], (tm, TOP_K))
        for k in range(1, TOP_K):
            out = jnp.where(kl == k, cols[k], out)
        return out

    idx_ref[...] = pack(idxs).astype(I32)
    gate_ref[...] = pack(gates)
    rank_ref[...] = pack(ranks).astype(I32)


def _post_call(alpha, h, ym, ys, yg, w_out, w_router, b_router, ln_g, ln_b):
    T, D = h.shape
    tm = ROW_BLOCK
    E = w_router.shape[1]
    wrh, wrl = _split2(w_router)
    const = lambda i: (0, 0)
    rows = lambda w: pl.BlockSpec((tm, w), lambda i: (i, 0))
    return pl.pallas_call(
        functools.partial(_post_kernel, alpha),
        grid=(T // tm,),
        in_specs=[rows(D), rows(M_WIDTH), rows(SB_WIDTH), rows(G_WIDTH),
                  pl.BlockSpec(w_out.shape, const),
                  pl.BlockSpec((D, E), const), pl.BlockSpec((D, E), const),
                  pl.BlockSpec((1, E), const), pl.BlockSpec((1, D), const),
                  pl.BlockSpec((1, D), const)],
        out_specs=[rows(D), rows(TOP_K), rows(TOP_K), rows(TOP_K),
                   pl.BlockSpec((1, E), const)],
        out_shape=[jax.ShapeDtypeStruct((T, D), F32),
                   jax.ShapeDtypeStruct((T, TOP_K), I32), jax.ShapeDtypeStruct((T, TOP_K), F32),
                   jax.ShapeDtypeStruct((T, TOP_K), I32), jax.ShapeDtypeStruct((1, E), I32)],
        scratch_shapes=[pltpu.VMEM((1, E), F32)],
        compiler_params=_cparams(("arbitrary",)),
        name="post_mix",
    )(h, ym, ys, yg, w_out, wrh, wrl, b_router.reshape(1, E),
      ln_g.reshape(1, D), ln_b.reshape(1, D))


def _row_copy(src, s, dst, d, sem):
    return pltpu.make_async_copy(src.at[pl.ds(s, 1), :], dst.at[pl.ds(d, 1), :], sem)


def _dispatch_kernel(alpha, dest_ref, h1_ref, p_ref, wpg_ref, wpp_ref, xs_in_ref,
                     base_ref, xs_ref, xpk_sc, sem):
    del xs_in_ref
    tm, D = h1_ref.shape
    half = D // 2
    h1 = h1_ref[...]
    hb = h1.astype(BF16)
    bits = lax.bitcast_convert_type(hb.astype(F32), jnp.uint32)
    xpk_sc[...] = (bits[:, :half] >> 16) | (bits[:, half:] & jnp.uint32(0xFFFF0000))

    def issue(r, carry):
        for k in range(TOP_K):
            _row_copy(xpk_sc, r, xs_ref, dest_ref[r * TOP_K + k], sem).start()
        return carry

    lax.fori_loop(0, tm, issue, 0)
    ple = _sigmoid(_dot(hb, wpg_ref[...])) * _dot(p_ref[...].astype(BF16), wpp_ref[...])
    base_ref[...] = alpha * h1 + ple

    def drain(r, carry):
        for k in range(TOP_K):
            _row_copy(xpk_sc, r, xs_ref, dest_ref[r * TOP_K + k], sem).wait()
        return carry

    lax.fori_loop(0, tm, drain, 0)


def _dispatch_call(alpha, dest_flat, h1, p, w_pg, w_pp, n_slots):
    T, D = h1.shape
    tm = ROW_BLOCK
    P = p.shape[1]
    const = lambda i: (0, 0)
    rows = lambda w: pl.BlockSpec((tm, w), lambda i: (i, 0))
    xs0 = jnp.zeros((n_slots, D // 2), jnp.uint32)
    base, xs = pl.pallas_call(
        functools.partial(_dispatch_kernel, alpha),
        grid=(T // tm,),
        in_specs=[pl.BlockSpec((tm * TOP_K,), lambda i: (i,), memory_space=pltpu.SMEM),
                  rows(D), rows(P), pl.BlockSpec((D, D), const), pl.BlockSpec((P, D), const),
                  pl.BlockSpec(memory_space=pl.ANY)],
        out_specs=[rows(D), pl.BlockSpec(memory_space=pl.ANY)],
        out_shape=[jax.ShapeDtypeStruct((T, D), F32),
                   jax.ShapeDtypeStruct((n_slots, D // 2), jnp.uint32)],
        scratch_shapes=[pltpu.VMEM((tm, D // 2), jnp.uint32), pltpu.SemaphoreType.DMA],
        input_output_aliases={5: 1},
        compiler_params=_cparams(("arbitrary",)),
        name="ple_dispatch",
    )(dest_flat, h1, p, w_pg, w_pp, xs0)
    return base, xs


def _moe_kernel(be_ref, nb_ref, x_ref, wgu_ref, bgu_ref, wd_ref, bd_ref, y_ref, wgu_sc, wd_sc):
    i = pl.program_id(0)
    F = wd_ref.shape[2]
    CH = 128

    new_expert = jnp.logical_or(i == 0, be_ref[i] != be_ref[jnp.maximum(i - 1, 0)])

    @pl.when(jnp.logical_and(new_expert, i < nb_ref[0]))
    def _():
        def cast_gu(r, c):
            rows = pl.ds(pl.multiple_of(r * CH, CH), CH)
            wgu_sc[rows, :] = wgu_ref[0, 0, rows, :].astype(BF16)
            return c

        def cast_d(r, c):
            rows = pl.ds(pl.multiple_of(r * CH, CH), CH)
            wd_sc[rows, :] = wd_ref[0, 0, rows, :].astype(BF16)
            return c

        lax.fori_loop(0, wgu_ref.shape[2] // CH, cast_gu, 0)
        lax.fori_loop(0, F // CH, cast_d, 0)

    @pl.when(i < nb_ref[0])
    def _():
        xw = x_ref[...]
        lo = lax.bitcast_convert_type(xw << 16, F32)
        hi = lax.bitcast_convert_type(xw & jnp.uint32(0xFFFF0000), F32)
        xb = jnp.concatenate([lo, hi], axis=1).astype(BF16)
        h = _dot(xb, wgu_sc[...]) + bgu_ref[0, 0]
        gate = jnp.minimum(h[:, :F], SWIGLU_LIMIT)
        up = jnp.clip(h[:, F:], -SWIGLU_LIMIT, SWIGLU_LIMIT)
        act = (up + 1.0) * gate * _sigmoid(SWIGLU_ALPHA * gate)
        y_ref[...] = _dot(act.astype(BF16), wd_sc[...]) + bd_ref[0, 0]

    @pl.when(i >= nb_ref[0])
    def _():
        y_ref[...] = jnp.zeros_like(y_ref)


def _moe_call(layer, block_e, n_used, xs, w_gu, b_gu, w_down, b_down):
    P = xs.shape[0]
    L, E, D, F2 = w_gu.shape
    F = F2 // 2
    tm = MOE_BLOCK
    nblk = P // tm
    grid_spec = pltpu.PrefetchScalarGridSpec(
        num_scalar_prefetch=2,
        grid=(nblk,),
        in_specs=[pl.BlockSpec((tm, D // 2), lambda i, be, nb: (i, 0)),
                  pl.BlockSpec((1, 1, D, F2), lambda i, be, nb: (layer, be[i], 0, 0)),
                  pl.BlockSpec((1, 1, 1, F2), lambda i, be, nb: (layer, be[i], 0, 0)),
                  pl.BlockSpec((1, 1, F, D), lambda i, be, nb: (layer, be[i], 0, 0)),
                  pl.BlockSpec((1, 1, 1, D), lambda i, be, nb: (layer, be[i], 0, 0))],
        out_specs=pl.BlockSpec((tm, D), lambda i, be, nb: (i, 0)),
        scratch_shapes=[pltpu.VMEM((D, F2), BF16), pltpu.VMEM((F, D), BF16)],
    )
    return pl.pallas_call(
        _moe_kernel,
        grid_spec=grid_spec,
        out_shape=jax.ShapeDtypeStruct((P, D), F32),
        compiler_params=_cparams(("arbitrary",)),
        name="moe_experts",
    )(block_e, n_used, xs, w_gu, b_gu.reshape(L, E, 1, F2), w_down, b_down.reshape(L, E, 1, D))


def _combine_kernel(dest_ref, base_ref, gate_ref, y_ref, g_ref, b_ref, o_ref, ybuf, sem):
    tm = base_ref.shape[0]

    def copy(r, k):
        return pltpu.make_async_copy(y_ref.at[pl.ds(dest_ref[r * TOP_K + k], 1), :],
                                     ybuf.at[k, pl.ds(r, 1), :], sem)

    def issue(r, carry):
        for k in range(TOP_K):
            copy(r, k).start()
        return carry

    def drain(r, carry):
        for k in range(TOP_K):
            copy(r, k).wait()
        return carry

    lax.fori_loop(0, tm, issue, 0)
    lax.fori_loop(0, tm, drain, 0)
    gate = gate_ref[...]
    acc = base_ref[...]
    for k in range(TOP_K):
        acc = acc + gate[:, k:k + 1] * ybuf[k]
    o_ref[...] = _layer_norm(acc, g_ref[...], b_ref[...])


def _combine_call(dest_flat, base, gates, y, ln_g, ln_b):
    T, D = base.shape
    tm = ROW_BLOCK
    rows = lambda w: pl.BlockSpec((tm, w), lambda i: (i, 0))
    const = lambda i: (0, 0)
    return pl.pallas_call(
        _combine_kernel,
        grid=(T // tm,),
        in_specs=[pl.BlockSpec((tm * TOP_K,), lambda i: (i,), memory_space=pltpu.SMEM),
                  rows(D), rows(TOP_K), pl.BlockSpec(memory_space=pl.ANY),
                  pl.BlockSpec((1, D), const), pl.BlockSpec((1, D), const)],
        out_specs=rows(D),
        out_shape=jax.ShapeDtypeStruct((T, D), F32),
        scratch_shapes=[pltpu.VMEM((TOP_K, tm, D), F32), pltpu.SemaphoreType.DMA],
        compiler_params=_cparams(("arbitrary",)),
        name="combine_ln",
    )(dest_flat, base, gates, y, ln_g.reshape(1, D), ln_b.reshape(1, D))


def _moe_ffn(alpha, layer, h1, p_i, idx, gates, rank, counts, w_pg, w_pp, w_gu, b_gu, w_down,
             b_down, ln_g, ln_b):
    T, D = h1.shape
    E = w_gu.shape[1]
    A = T * TOP_K
    counts = counts.reshape(E)
    padded = (counts + MOE_BLOCK - 1) // MOE_BLOCK * MOE_BLOCK
    pad_end = jnp.cumsum(padded)
    pad_start = pad_end - padded
    n_blocks = -(-A // MOE_BLOCK) + E
    P = n_blocks * MOE_BLOCK
    experts = jnp.arange(E, dtype=I32)
    dest = jnp.sum(jnp.where(idx[:, :, None] == experts, pad_start, 0), axis=-1) + rank
    dest_flat = dest.reshape(A).astype(I32)
    blk_start = jnp.arange(n_blocks, dtype=I32) * MOE_BLOCK
    block_e = jnp.minimum(jnp.sum((pad_end[None, :] <= blk_start[:, None]).astype(I32), axis=1),
                          E - 1).astype(I32)
    n_used = (pad_end[-1] // MOE_BLOCK).astype(I32).reshape(1)
    base, xs = _dispatch_call(alpha, dest_flat, h1, p_i, w_pg, w_pp, P)
    y = _moe_call(layer, block_e, n_used, xs, w_gu, b_gu, w_down, b_down)
    return _combine_call(dest_flat, base, gates, y, ln_g, ln_b)


def _layer(layer, h, p_i, B, S, alpha, w_main, wg, m_i_bias, m_f_bias, m_norm_w, sb_norm_w, g_conv_w,
           g_A_log, g_dt_bias, g_norm_w, w_out, ln1_g, ln1_b, w_router, b_router, w_gu, b_gu,
           w_down, b_down, w_pg, w_pp, ln2_g, ln2_b):
    u, g, gt = _inproj_call(h, w_main, wg)
    ym = _mlstm_call(u, g, gt, m_i_bias, m_f_bias, m_norm_w, B, S)
    ys = _sb_call(u, sb_norm_w, B, S, (4 * M_WIDTH + 4 * G_WIDTH) // SB_WIDTH)
    yg = _gdn_call(u, g, gt, g_conv_w, g_A_log, g_dt_bias, g_norm_w, B, S, 4 * M_WIDTH // G_WIDTH)
    h1, idx, gates, rank, counts = _post_call(
        alpha, h, ym, ys, yg, w_out, w_router, b_router, ln1_g, ln1_b)
    return _moe_ffn(alpha, layer, h1, p_i, idx, gates, rank, counts, w_pg, w_pp, w_gu, b_gu,
                    w_down, b_down, ln2_g, ln2_b)


def kernel(x, p, ln0_g, ln0_b, w_in, m_i_bias, m_f_bias, m_norm_w, sb_norm_w, g_conv_w, g_A_log,
           g_dt_bias, g_norm_w, w_out, ln1_g, ln1_b, w_router, b_router, w_gu, b_gu, w_down,
           b_down, w_ple_gate, w_ple_proj, ln2_g, ln2_b):
    B, S, D = x.shape
    depth = w_in.shape[0]
    T = B * S
    alpha = (2 * depth) ** 0.25
    g0 = 4 * M_WIDTH
    g1 = g0 + 2 * M_HEADS
    s1 = g1 + 3 * SB_WIDTH
    g2 = s1 + 4 * G_WIDTH
    h = _ln_call(x.reshape(T, D), ln0_g, ln0_b)
    for i in range(depth):
        w = w_in[i]
        w_main = jnp.concatenate([w[:, :g0], w[:, s1:g2], w[:, g1:s1]], axis=1).astype(BF16)
        wg = jnp.concatenate([w[:, g0:g1], w[:, g2:]], axis=1)
        h = _layer(i, h, p[i].reshape(T, -1), B, S, alpha, w_main, wg, m_i_bias[i], m_f_bias[i],
                   m_norm_w[i], sb_norm_w[i], g_conv_w[i], g_A_log[i], g_dt_bias[i], g_norm_w[i],
                   w_out[i].astype(BF16), ln1_g[i], ln1_b[i], w_router[i], b_router[i],
                   w_gu, b_gu, w_down, b_down,
                   w_ple_gate[i].astype(BF16), w_ple_proj[i].astype(BF16), ln2_g[i], ln2_b[i])
    return h.reshape(B, S, D)
```

```python
import functools
import math

import jax
import jax.numpy as jnp
from jax import lax
from jax.experimental import pallas as pl
from jax.experimental.pallas import tpu as pltpu

F32 = jnp.float32
BF16 = jnp.bfloat16
I32 = jnp.int32

M_HEADS, M_HEAD_DIM = 4, 64
SB_HEADS, SB_HEAD_DIM = 4, 64
G_HEADS, G_HEAD_DIM = 4, 128
G_CONV = 4
M_WIDTH = M_HEADS * M_HEAD_DIM
SB_WIDTH = SB_HEADS * SB_HEAD_DIM
G_WIDTH = G_HEADS * G_HEAD_DIM
N_GATE_COLS = 16
TOP_K = 4
SWIGLU_LIMIT = 7.0
SWIGLU_ALPHA = 1.702
LN_EPS = 1e-5
NORM_EPS = 1e-6

ROW_BLOCK = 256
POST_BLOCK = 512
M_CHUNK = 256
SB_BLOCK = 256
SB_STRIP = 256
G_CHUNK = 64
G_BLOCK = 256
G_SUB = 16
MOE_BLOCK = 512
DMA_UNROLL = 8
VMEM_LIMIT = 56 * 1024 * 1024


def _cparams(sem):
    return pltpu.CompilerParams(dimension_semantics=sem, vmem_limit_bytes=VMEM_LIMIT)


def _split3(x):
    hi = x.astype(BF16)
    r1 = x - hi.astype(F32)
    mid = r1.astype(BF16)
    lo = (r1 - mid.astype(F32)).astype(BF16)
    return hi, mid, lo


def _split2(x):
    hi = x.astype(BF16)
    lo = (x - hi.astype(F32)).astype(BF16)
    return hi, lo


def _dot(a, b):
    return jnp.dot(a, b, preferred_element_type=F32)


def _dot_nt(a, b):
    return lax.dot_general(a, b, (((1,), (1,)), ((), ())), preferred_element_type=F32)


def _dot_tn(a, b):
    return lax.dot_general(a, b, (((0,), (0,)), ((), ())), preferred_element_type=F32)


def _dot3(a, b):
    ah, al = _split2(a)
    bh, bl = _split2(b)
    return _dot(ah, bh) + (_dot(ah, bl) + _dot(al, bh))


def _dot3_nt(a, b):
    ah, al = _split2(a)
    bh, bl = _split2(b)
    return _dot_nt(ah, bh) + (_dot_nt(ah, bl) + _dot_nt(al, bh))


def _bdot(a, b):
    return lax.dot_general(a, b, (((2,), (1,)), ((0,), (0,))), preferred_element_type=F32)


def _bdot_nt(a, b):
    return lax.dot_general(a, b, (((2,), (2,)), ((0,), (0,))), preferred_element_type=F32)


def _bdot_tn(a, b):
    return lax.dot_general(a, b, (((1,), (1,)), ((0,), (0,))), preferred_element_type=F32)


def _bdot3(a, b):
    ah, al = _split2(a)
    bh, bl = _split2(b)
    return _bdot(ah, bh) + (_bdot(ah, bl) + _bdot(al, bh))


def _bdot3_nt(a, b):
    ah, al = _split2(a)
    bh, bl = _split2(b)
    return _bdot_nt(ah, bh) + (_bdot_nt(ah, bl) + _bdot_nt(al, bh))


def _exact_left(mask_bf16, x):
    hi, mid, lo = _split3(x)
    return _dot(mask_bf16, hi) + (_dot(mask_bf16, mid) + _dot(mask_bf16, lo))


def _exact_right(x, mask_bf16):
    hi, mid, lo = _split3(x)
    return _dot(hi, mask_bf16) + (_dot(mid, mask_bf16) + _dot(lo, mask_bf16))


def _softplus(x):
    return jnp.maximum(x, 0.0) + jnp.log1p(jnp.exp(-jnp.abs(x)))


def _log_sigmoid(x):
    return -_softplus(-x)


def _sigmoid(x):
    return 1.0 / (1.0 + jnp.exp(-x))


def _layer_norm(x, g, b):
    mu = jnp.mean(x, axis=-1, keepdims=True)
    xc = x - mu
    var = jnp.mean(xc * xc, axis=-1, keepdims=True)
    return xc * lax.rsqrt(var + LN_EPS) * g + b


def _iota2(shape, dim):
    return lax.broadcasted_iota(I32, shape, dim)


def _ln_kernel(x_ref, g_ref, b_ref, o_ref):
    o_ref[...] = _layer_norm(x_ref[...], g_ref[...], b_ref[...])


def _ln_call(x, g, b):
    T, D = x.shape
    tm = ROW_BLOCK
    return pl.pallas_call(
        _ln_kernel,
        grid=(T // tm,),
        in_specs=[pl.BlockSpec((tm, D), lambda i: (i, 0)),
                  pl.BlockSpec((1, D), lambda i: (0, 0)),
                  pl.BlockSpec((1, D), lambda i: (0, 0))],
        out_specs=pl.BlockSpec((tm, D), lambda i: (i, 0)),
        out_shape=jax.ShapeDtypeStruct((T, D), F32),
        compiler_params=_cparams(("parallel",)),
        name="embed_ln",
    )(x, g.reshape(1, D), b.reshape(1, D))


def _inproj_kernel(h_ref, w_ref, wgh_ref, wgl_ref, wgth_ref, wgtl_ref, u_ref, g_ref, gt_ref):
    h = h_ref[...]
    hh, hl = _split2(h)
    u_ref[...] = _dot(hh, w_ref[...])
    wgh, wgl = wgh_ref[...], wgl_ref[...]
    g_ref[...] = _dot(hh, wgh) + (_dot(hh, wgl) + _dot(hl, wgh))
    wgth, wgtl = wgth_ref[...], wgtl_ref[...]
    gt_ref[...] = _dot_nt(wgth, hh) + (_dot_nt(wgtl, hh) + _dot_nt(wgth, hl))


def _inproj_call(h, w_main, wg):
    T, D = h.shape
    N = w_main.shape[1]
    tm = ROW_BLOCK
    wgh, wgl = _split2(wg)
    wgt = wg.T
    wgth, wgtl = _split2(wgt)
    G = N_GATE_COLS
    const = lambda i: (0, 0)
    return pl.pallas_call(
        _inproj_kernel,
        grid=(T // tm,),
        in_specs=[pl.BlockSpec((tm, D), lambda i: (i, 0)),
                  pl.BlockSpec((D, N), const),
                  pl.BlockSpec((D, G), const), pl.BlockSpec((D, G), const),
                  pl.BlockSpec((G, D), const), pl.BlockSpec((G, D), const)],
        out_specs=[pl.BlockSpec((tm, N), lambda i: (i, 0)),
                   pl.BlockSpec((tm, G), lambda i: (i, 0)),
                   pl.BlockSpec((G, tm), lambda i: (0, i))],
        out_shape=[jax.ShapeDtypeStruct((T, N), F32),
                   jax.ShapeDtypeStruct((T, G), F32),
                   jax.ShapeDtypeStruct((G, T), F32)],
        compiler_params=_cparams(("parallel",)),
        name="in_proj",
    )(h, w_main, wgh, wgl, wgth, wgtl)


def _mlstm_kernel(q_ref, k_ref, v_ref, o_ref, g_ref, gt_ref, brow_ref, bcol_ref, nw_ref,
                  y_ref, c_sc, m_sc):
    L = q_ref.shape[0]
    d = M_HEAD_DIM

    @pl.when(pl.program_id(1) == 0)
    def _():
        c_sc[...] = jnp.zeros_like(c_sc)
        m_sc[...] = jnp.full_like(m_sc, -jnp.inf)

    row = _iota2((L, L), 0)
    col = _iota2((L, L), 1)
    causal = col <= row
    tril = causal.astype(BF16)
    triu = (row <= col).astype(BF16)

    gb = g_ref[...] + brow_ref[...]
    gtb = gt_ref[...] + bcol_ref[...]
    b_cols = _exact_left(tril, _log_sigmoid(gb))
    b_rows = _exact_right(_log_sigmoid(gtb), triu)

    ones_col = (_iota2((L, d), 1) == 0).astype(F32)
    outs = []
    for hd in range(M_HEADS):
        sl = slice(hd * d, (hd + 1) * d)
        q = q_ref[:, sl].astype(BF16)
        kf = k_ref[:, sl] * (d ** -0.5)
        k = kf.astype(BF16)
        v_aug = jnp.concatenate([v_ref[:, sl], ones_col], axis=1).astype(BF16)
        bc = b_cols[:, M_HEADS + hd:M_HEADS + hd + 1]
        lic = gb[:, hd:hd + 1]
        br = b_rows[M_HEADS + hd:M_HEADS + hd + 1, :]
        lir = gtb[hd:hd + 1, :]
        b_end = bc[L - 1:L, :]
        m_prev = m_sc[hd]
        c_aug = c_sc[hd]

        src = lir - br
        log_w = jnp.where(causal, bc + src, -jnp.inf)
        m_intra = jnp.max(log_w, axis=1, keepdims=True)
        carry_log = bc + m_prev
        m_t = jnp.maximum(carry_log, m_intra)
        inter = jnp.exp(carry_log - m_t)
        dm = jnp.exp(log_w - m_t)
        s = _dot_nt(q, k) * dm
        num_aug = inter * _dot(q, c_aug.astype(BF16)) + _dot(s.astype(BF16), v_aug)
        num = num_aug[:, :d]
        den = num_aug[:, d:d + 1]
        hh = num / jnp.maximum(jnp.abs(den), jnp.exp(-m_t))

        m_end = jnp.max(b_end + src, axis=1, keepdims=True)
        m_new = jnp.maximum(b_end + m_prev, m_end)
        carry_scale = jnp.exp(b_end + m_prev - m_new)
        wk = (jnp.exp(b_end - bc + lic - m_new) * kf).astype(BF16)
        c_sc[hd] = carry_scale * c_aug + _dot_tn(wk, v_aug)
        m_sc[hd] = m_new

        hn = hh * lax.rsqrt(jnp.mean(hh * hh, axis=1, keepdims=True) + NORM_EPS)
        outs.append(hn)
    hcat = jnp.concatenate(outs, axis=1) * nw_ref[...]
    y_ref[...] = _sigmoid(o_ref[...]) * hcat


def _mlstm_call(u, g, gt, i_bias, f_bias, norm_w, B, S):
    T = u.shape[0]
    L = min(M_CHUNK, S)
    nc = S // L
    W = M_WIDTH
    zeros = jnp.zeros((N_GATE_COLS - 2 * M_HEADS,), F32)
    bias = jnp.concatenate([i_bias, f_bias, zeros])
    G = N_GATE_COLS
    rowblk = lambda c: pl.BlockSpec((L, W), lambda b, j, c=c: (b * nc + j, c))
    const = lambda b, j: (0, 0)
    return pl.pallas_call(
        _mlstm_kernel,
        grid=(B, nc),
        in_specs=[rowblk(0), rowblk(1), rowblk(2), rowblk(3),
                  pl.BlockSpec((L, G), lambda b, j: (b * nc + j, 0)),
                  pl.BlockSpec((G, L), lambda b, j: (0, b * nc + j)),
                  pl.BlockSpec((1, G), const), pl.BlockSpec((G, 1), const),
                  pl.BlockSpec((1, W), const)],
        out_specs=pl.BlockSpec((L, W), lambda b, j: (b * nc + j, 0)),
        out_shape=jax.ShapeDtypeStruct((T, W), F32),
        scratch_shapes=[pltpu.VMEM((M_HEADS, M_HEAD_DIM, 2 * M_HEAD_DIM), F32),
                        pltpu.VMEM((M_HEADS, 1, 1), F32)],
        compiler_params=_cparams(("parallel", "arbitrary")),
        name="mlstm",
    )(u, u, u, u, g, gt, bias.reshape(1, G), bias.reshape(G, 1), norm_w.reshape(1, W))


def _sb_kernel(q_ref, k_ref, v_ref, nw_ref, y_ref, qb_sc, acc_sc, r_sc):
    tq = q_ref.shape[0]
    tk = tq
    d = SB_HEAD_DIM
    RS = min(SB_STRIP, tq)
    ns = tq // RS
    qi = pl.program_id(1)
    jj = _iota2((tk, tk), 0)
    ss = _iota2((tk, tk), 1)
    later = (ss < jj).astype(BF16)
    srow = _iota2((RS, tk), 0)
    scol = _iota2((RS, tk), 1)

    qb_sc[...] = (q_ref[...] * (d ** -0.5)).astype(BF16)
    acc_sc[...] = jnp.zeros_like(acc_sc)
    r_sc[...] = jnp.zeros_like(r_sc)

    def block(kb, diagonal):
        off = pl.multiple_of(kb * tk, tk)
        for hd in range(SB_HEADS):
            cols = slice(hd * d, (hd + 1) * d)
            kblk = k_ref[pl.ds(off, tk), cols].astype(BF16)
            vblk = v_ref[pl.ds(off, tk), cols].astype(BF16)
            for s in range(ns):
                rows = slice(s * RS, (s + 1) * RS)
                z = _dot_nt(qb_sc[rows, cols], kblk)
                sp = jnp.maximum(z, 0.0) + jnp.log(1.0 + jnp.exp(-jnp.abs(z)))
                if diagonal:
                    valid = scol < (srow + s * RS)
                    sp = jnp.where(valid, sp, 0.0)
                sh, sl = _split2(sp)
                c2 = _dot(jnp.concatenate([sh, sl], axis=0), later)
                cs = c2[:RS] + c2[RS:]
                a = jnp.exp(z - sp - cs)
                if diagonal:
                    a = jnp.where(valid, a, 0.0)
                pv = _dot(a.astype(BF16), vblk)
                rest = r_sc[hd, rows, :]
                acc_sc[hd, rows, :] = acc_sc[hd, rows, :] + jnp.exp(-rest) * pv
                r_sc[hd, rows, :] = rest + (cs[:, 0:1] + sp[:, 0:1])

    block(qi, True)

    def body(it, carry):
        block(qi - it, False)
        return carry

    lax.fori_loop(1, qi + 1, body, 0)
    outs = []
    for hd in range(SB_HEADS):
        o = acc_sc[hd]
        outs.append(o * lax.rsqrt(jnp.mean(o * o, axis=1, keepdims=True) + NORM_EPS))
    y_ref[...] = jnp.concatenate(outs, axis=1) * nw_ref[...]


def _sb_call(u, norm_w, B, S, col0):
    T = u.shape[0]
    tq = min(SB_BLOCK, S)
    nq = S // tq
    W = SB_WIDTH
    return pl.pallas_call(
        _sb_kernel,
        grid=(B, nq),
        in_specs=[pl.BlockSpec((tq, W), lambda b, i: (b * nq + i, col0)),
                  pl.BlockSpec((S, W), lambda b, i: (b, col0 + 1)),
                  pl.BlockSpec((S, W), lambda b, i: (b, col0 + 2)),
                  pl.BlockSpec((1, W), lambda b, i: (0, 0))],
        out_specs=pl.BlockSpec((tq, W), lambda b, i: (b * nq + i, 0)),
        out_shape=jax.ShapeDtypeStruct((T, W), F32),
        scratch_shapes=[pltpu.VMEM((tq, W), BF16),
                        pltpu.VMEM((SB_HEADS, tq, SB_HEAD_DIM), F32),
                        pltpu.VMEM((SB_HEADS, tq, 1), F32)],
        compiler_params=_cparams(("parallel", "arbitrary")),
        name="stickbreak",
    )(u, u, u, norm_w.reshape(1, W))


def _gdn_kernel(q_ref, k_ref, v_ref, z_ref, g_ref, gt_ref, cw_ref, arow_ref, acol_ref,
                drow_ref, dcol_ref, nw_ref, y_ref, xq_sc, xk_sc, xv_sc, s_sc):
    LB = q_ref.shape[0]
    C = G_CHUNK
    d = G_HEAD_DIM
    W = G_WIDTH
    HALO = 8

    @pl.when(pl.program_id(1) == 0)
    def _():
        s_sc[...] = jnp.zeros_like(s_sc)
        for sc in (xq_sc, xk_sc, xv_sc):
            sc[0:HALO, :] = jnp.zeros((HALO, W), F32)

    @pl.when(pl.program_id(1) != 0)
    def _():
        for sc in (xq_sc, xk_sc, xv_sc):
            sc[0:HALO, :] = sc[LB:LB + HALO, :]

    def conv_silu(x_ref, sc, j):
        sc[HALO:HALO + LB, :] = x_ref[...]
        acc = None
        for t in range(G_CONV):
            w = cw_ref[t:t + 1, j * W:(j + 1) * W]
            term = w * sc[HALO - (G_CONV - 1) + t:HALO - (G_CONV - 1) + t + LB, :]
            acc = term if acc is None else acc + term
        return acc * _sigmoid(acc)

    qc = conv_silu(q_ref, xq_sc, 0)
    kc = conv_silu(k_ref, xk_sc, 1)
    vc = conv_silu(v_ref, xv_sc, 2)

    gb = g_ref[...]
    gtb = gt_ref[...]
    gdec_cols = -jnp.exp(arow_ref[...]) * _softplus(gb + drow_ref[...])
    gdec_rows = -jnp.exp(acol_ref[...]) * _softplus(gtb + dcol_ref[...])
    beta_cols = _sigmoid(gb)
    row = _iota2((LB, LB), 0)
    col = _iota2((LB, LB), 1)
    shift = C.bit_length() - 1
    same = jnp.right_shift(row, shift) == jnp.right_shift(col, shift)
    tril = (same & (col <= row)).astype(BF16)
    triu = (same & (row <= col)).astype(BF16)
    gam_cols = _exact_left(tril, gdec_cols)
    gam_rows = _exact_right(gdec_rows, triu)

    nc = LB // C
    NB = G_HEADS * nc

    def per_head(x):
        return jnp.stack([x[:, h * d:(h + 1) * d] for h in range(G_HEADS)], axis=0).reshape(NB, C, d)

    q3 = per_head(qc)
    k3 = per_head(kc)
    v3 = per_head(vc)
    q3 = q3 * lax.rsqrt(jnp.sum(q3 * q3, axis=2, keepdims=True) + NORM_EPS) * (d ** -0.5)
    k3 = k3 * lax.rsqrt(jnp.sum(k3 * k3, axis=2, keepdims=True) + NORM_EPS)
    ga0 = 2 * G_HEADS
    gb0 = 3 * G_HEADS
    gam_c = jnp.stack([gam_cols[:, ga0 + h:ga0 + h + 1] for h in range(G_HEADS)], 0).reshape(NB, C, 1)
    beta = jnp.stack([beta_cols[:, gb0 + h:gb0 + h + 1] for h in range(G_HEADS)], 0).reshape(NB, C, 1)
    gam_r = jnp.stack([gam_rows[ga0 + h:ga0 + h + 1, c * C:(c + 1) * C]
                       for h in range(G_HEADS) for c in range(nc)], 0)
    gam_end = gam_c[:, C - 1:C, :]

    r64 = _iota2((NB, C, C), 1)
    c64 = _iota2((NB, C, C), 2)
    incl = c64 <= r64
    strict = c64 < r64
    sshift = G_SUB.bit_length() - 1
    diag_blk = jnp.right_shift(r64, sshift) == jnp.right_shift(c64, sshift)
    eye = (r64 == c64).astype(F32)

    decay = jnp.exp(jnp.where(incl, gam_c - gam_r, -jnp.inf))
    kk = _bdot3_nt(k3, k3)
    m = jnp.where(strict, beta * kk * decay, 0.0)
    md = jnp.where(diag_blk, m, 0.0)
    moff = m - md
    md2 = _bdot3(md, md)
    md4 = _bdot3(md2, md2)
    md8 = _bdot3(md4, md4)
    td = _bdot3(_bdot3(_bdot3(eye - md, eye + md2), eye + md4), eye + md8)
    n1 = _bdot3(td, moff)
    n2 = _bdot3(n1, n1)
    n3 = _bdot3(n1, n2)
    tinv = _bdot3(eye - n1 + n2 - n3, td)
    eg = jnp.exp(gam_c)
    u = _bdot3(tinv, v3 * beta)
    w = _bdot3(tinv, k3 * (beta * eg)).astype(BF16)
    qk = (_bdot_nt(q3.astype(BF16), k3.astype(BF16)) * decay).astype(BF16)
    q_dec = (q3 * eg).astype(BF16)
    k_dec = (k3 * jnp.exp(gam_end - gam_c)).astype(BF16)
    cdec = jnp.exp(gam_end)

    def chunk(x, c):
        return x.reshape((G_HEADS, nc) + x.shape[1:])[:, c]

    state = s_sc[...]
    o_chunks = []
    for c in range(nc):
        sb = state.astype(BF16)
        v_new = chunk(u, c) - _bdot(chunk(w, c), sb)
        vb = v_new.astype(BF16)
        o_chunks.append(_bdot(chunk(q_dec, c), sb) + _bdot(chunk(qk, c), vb))
        state = chunk(cdec, c) * state + _bdot_tn(chunk(k_dec, c), vb)
    s_sc[...] = state

    nw = nw_ref[...]
    o = jnp.concatenate(o_chunks, axis=1)
    o = o * lax.rsqrt(jnp.mean(o * o, axis=2, keepdims=True) + NORM_EPS) * nw
    zz = z_ref[...]
    y_ref[...] = jnp.concatenate([o[h] for h in range(G_HEADS)], axis=1) * (zz * _sigmoid(zz))


def _gdn_call(u, g, gt, conv_w, a_log, dt_bias, norm_w, B, S, col0):
    T = u.shape[0]
    LB = min(G_BLOCK, S)
    nb = S // LB
    W = G_WIDTH
    G = N_GATE_COLS
    pad = jnp.zeros((2 * G_HEADS,), F32)
    arow = jnp.concatenate([pad, a_log, jnp.zeros((G_HEADS,), F32)])
    drow = jnp.concatenate([pad, dt_bias, jnp.zeros((G_HEADS,), F32)])
    rowblk = lambda c: pl.BlockSpec((LB, W), lambda b, j, c=c: (b * nb + j, col0 + c))
    const = lambda b, j: (0, 0)
    return pl.pallas_call(
        _gdn_kernel,
        grid=(B, nb),
        in_specs=[rowblk(0), rowblk(1), rowblk(2), rowblk(3),
                  pl.BlockSpec((LB, G), lambda b, j: (b * nb + j, 0)),
                  pl.BlockSpec((G, LB), lambda b, j: (0, b * nb + j)),
                  pl.BlockSpec((G_CONV, 3 * W), const),
                  pl.BlockSpec((1, G), const), pl.BlockSpec((G, 1), const),
                  pl.BlockSpec((1, G), const), pl.BlockSpec((G, 1), const),
                  pl.BlockSpec((1, G_HEAD_DIM), const)],
        out_specs=pl.BlockSpec((LB, W), lambda b, j: (b * nb + j, 0)),
        out_shape=jax.ShapeDtypeStruct((T, W), F32),
        scratch_shapes=[pltpu.VMEM((LB + 8, W), F32), pltpu.VMEM((LB + 8, W), F32),
                        pltpu.VMEM((LB + 8, W), F32),
                        pltpu.VMEM((G_HEADS, G_HEAD_DIM, G_HEAD_DIM), F32)],
        compiler_params=_cparams(("parallel", "arbitrary")),
        name="gdn",
    )(u, u, u, u, g, gt, conv_w, arow.reshape(1, G), arow.reshape(G, 1),
      drow.reshape(1, G), drow.reshape(G, 1), norm_w.reshape(1, G_HEAD_DIM))


def _post_kernel(alpha, h_ref, ym_ref, ys_ref, yg_ref, wo_ref,
                 wrh_ref, wrl_ref, br_ref, g1_ref, b1_ref,
                 h1_ref, idx_ref, gate_ref, rank_ref, cnt_ref, cnt_sc):
    tm = h_ref.shape[0]
    E = br_ref.shape[1]

    @pl.when(pl.program_id(0) == 0)
    def _():
        cnt_sc[...] = jnp.zeros_like(cnt_sc)

    y = jnp.concatenate([ym_ref[...], ys_ref[...], yg_ref[...]], axis=1).astype(BF16)
    a = alpha * h_ref[...] + _dot(y, wo_ref[...])
    h1 = _layer_norm(a, g1_ref[...], b1_ref[...])
    h1h, h1l = _split2(h1)
    h1_ref[...] = h1

    wrh, wrl = wrh_ref[...], wrl_ref[...]
    logits = _dot(h1h, wrh) + (_dot(h1h, wrl) + _dot(h1l, wrh)) + br_ref[...]
    lane = _iota2((tm, E), 1).astype(F32)
    work = logits
    vals, idxs = [], []
    anyhot = jnp.zeros((tm, E), F32)
    for _ in range(TOP_K):
        mx = jnp.max(work, axis=1, keepdims=True)
        ix = jnp.min(jnp.where(work == mx, lane, float(E)), axis=1, keepdims=True)
        sel = lane == ix
        vals.append(mx)
        idxs.append(ix)
        anyhot = jnp.where(sel, 1.0, anyhot)
        work = jnp.where(sel, -jnp.inf, work)
    ex = [jnp.exp(v - vals[0]) for v in vals]
    tot = ex[0] + ex[1] + ex[2] + ex[3]
    gates = [e / tot for e in ex]

    r = _iota2((tm, tm), 0)
    c = _iota2((tm, tm), 1)
    before = (c < r).astype(BF16)
    pos = cnt_sc[...] + _dot(before, anyhot.astype(BF16))
    ranks = [jnp.sum(jnp.where(lane == ix, pos, 0.0), axis=1, keepdims=True) for ix in idxs]
    cnt_sc[...] = cnt_sc[...] + jnp.sum(anyhot, axis=0, keepdims=True)
    cnt_ref[...] = cnt_sc[...].astype(I32)

    kl = _iota2((tm, TOP_K), 1)

    def pack(cols):
        out = jnp.broadcast_to(cols[0], (tm, TOP_K))
        for k in range(1, TOP_K):
            out = jnp.where(kl == k, cols[k], out)
        return out

    idx_ref[...] = pack(idxs).astype(I32)
    gate_ref[...] = pack(gates)
    rank_ref[...] = pack(ranks).astype(I32)


def _post_call(alpha, h, ym, ys, yg, w_out, w_router, b_router, ln_g, ln_b):
    T, D = h.shape
    tm = POST_BLOCK
    E = w_router.shape[1]
    wrh, wrl = _split2(w_router)
    const = lambda i: (0, 0)
    rows = lambda w: pl.BlockSpec((tm, w), lambda i: (i, 0))
    return pl.pallas_call(
        functools.partial(_post_kernel, alpha),
        grid=(T // tm,),
        in_specs=[rows(D), rows(M_WIDTH), rows(SB_WIDTH), rows(G_WIDTH),
                  pl.BlockSpec(w_out.shape, const),
                  pl.BlockSpec((D, E), const), pl.BlockSpec((D, E), const),
                  pl.BlockSpec((1, E), const), pl.BlockSpec((1, D), const),
                  pl.BlockSpec((1, D), const)],
        out_specs=[rows(D), rows(TOP_K), rows(TOP_K), rows(TOP_K),
                   pl.BlockSpec((1, E), const)],
        out_shape=[jax.ShapeDtypeStruct((T, D), F32),
                   jax.ShapeDtypeStruct((T, TOP_K), I32), jax.ShapeDtypeStruct((T, TOP_K), F32),
                   jax.ShapeDtypeStruct((T, TOP_K), I32), jax.ShapeDtypeStruct((1, E), I32)],
        scratch_shapes=[pltpu.VMEM((1, E), F32)],
        compiler_params=_cparams(("arbitrary",)),
        name="post_mix",
    )(h, ym, ys, yg, w_out, wrh, wrl, b_router.reshape(1, E),
      ln_g.reshape(1, D), ln_b.reshape(1, D))


def _row_copy(src, s, dst, d, sem):
    return pltpu.make_async_copy(src.at[pl.ds(s, 1), :], dst.at[pl.ds(d, 1), :], sem)


def _dispatch_kernel(alpha, dest_ref, h1_ref, p_ref, wpg_ref, wpp_ref, xs_in_ref,
                     base_ref, xs_ref, xpk_sc, sem):
    del xs_in_ref
    tm, D = h1_ref.shape
    half = D // 2
    h1 = h1_ref[...]
    hb = h1.astype(BF16)
    bits = lax.bitcast_convert_type(hb.astype(F32), jnp.uint32)
    xpk_sc[...] = (bits[:, :half] >> 16) | (bits[:, half:] & jnp.uint32(0xFFFF0000))

    def issue(r, carry):
        for k in range(TOP_K):
            _row_copy(xpk_sc, r, xs_ref, dest_ref[r * TOP_K + k], sem).start()
        return carry

    lax.fori_loop(0, tm, issue, 0, unroll=DMA_UNROLL)
    ple = _sigmoid(_dot(hb, wpg_ref[...])) * _dot(p_ref[...].astype(BF16), wpp_ref[...])
    base_ref[...] = alpha * h1 + ple
    for k in range(TOP_K):
        pltpu.make_async_copy(xpk_sc, xs_ref.at[pl.ds(0, tm), :], sem).wait()


def _dispatch_call(alpha, dest_flat, h1, p, w_pg, w_pp, n_slots):
    T, D = h1.shape
    tm = ROW_BLOCK
    P = p.shape[1]
    const = lambda i: (0, 0)
    rows = lambda w: pl.BlockSpec((tm, w), lambda i: (i, 0))
    xs0 = jnp.zeros((n_slots, D // 2), jnp.uint32)
    base, xs = pl.pallas_call(
        functools.partial(_dispatch_kernel, alpha),
        grid=(T // tm,),
        in_specs=[pl.BlockSpec((tm * TOP_K,), lambda i: (i,), memory_space=pltpu.SMEM),
                  rows(D), rows(P), pl.BlockSpec((D, D), const), pl.BlockSpec((P, D), const),
                  pl.BlockSpec(memory_space=pl.ANY)],
        out_specs=[rows(D), pl.BlockSpec(memory_space=pl.ANY)],
        out_shape=[jax.ShapeDtypeStruct((T, D), F32),
                   jax.ShapeDtypeStruct((n_slots, D // 2), jnp.uint32)],
        scratch_shapes=[pltpu.VMEM((tm, D // 2), jnp.uint32), pltpu.SemaphoreType.DMA],
        input_output_aliases={5: 1},
        compiler_params=_cparams(("arbitrary",)),
        name="ple_dispatch",
    )(dest_flat, h1, p, w_pg, w_pp, xs0)
    return base, xs


def _moe_kernel(be_ref, nb_ref, x_ref, wgu_ref, bgu_ref, wd_ref, bd_ref, y_ref, wgu_sc, wd_sc):
    i = pl.program_id(0)
    F = wd_ref.shape[2]
    CH = 128

    new_expert = jnp.logical_or(i == 0, be_ref[i] != be_ref[jnp.maximum(i - 1, 0)])

    @pl.when(jnp.logical_and(new_expert, i < nb_ref[0]))
    def _():
        def cast_gu(r, c):
            rows = pl.ds(pl.multiple_of(r * CH, CH), CH)
            wgu_sc[rows, :] = wgu_ref[0, 0, rows, :].astype(BF16)
            return c

        def cast_d(r, c):
            rows = pl.ds(pl.multiple_of(r * CH, CH), CH)
            wd_sc[rows, :] = wd_ref[0, 0, rows, :].astype(BF16)
            return c

        lax.fori_loop(0, wgu_ref.shape[2] // CH, cast_gu, 0)
        lax.fori_loop(0, F // CH, cast_d, 0)

    @pl.when(i < nb_ref[0])
    def _():
        xw = x_ref[...]
        lo = lax.bitcast_convert_type(xw << 16, F32)
        hi = lax.bitcast_convert_type(xw & jnp.uint32(0xFFFF0000), F32)
        xb = jnp.concatenate([lo, hi], axis=1).astype(BF16)
        h = _dot(xb, wgu_sc[...]) + bgu_ref[0, 0]
        gate = jnp.minimum(h[:, :F], SWIGLU_LIMIT)
        up = jnp.clip(h[:, F:], -SWIGLU_LIMIT, SWIGLU_LIMIT)
        act = (up + 1.0) * gate * _sigmoid(SWIGLU_ALPHA * gate)
        y_ref[...] = _dot(act.astype(BF16), wd_sc[...]) + bd_ref[0, 0]

    @pl.when(i >= nb_ref[0])
    def _():
        y_ref[...] = jnp.zeros_like(y_ref)


def _moe_call(layer, block_e, n_used, xs, w_gu, b_gu, w_down, b_down):
    P = xs.shape[0]
    L, E, D, F2 = w_gu.shape
    F = F2 // 2
    tm = MOE_BLOCK
    nblk = P // tm
    grid_spec = pltpu.PrefetchScalarGridSpec(
        num_scalar_prefetch=2,
        grid=(nblk,),
        in_specs=[pl.BlockSpec((tm, D // 2), lambda i, be, nb: (i, 0)),
                  pl.BlockSpec((1, 1, D, F2), lambda i, be, nb: (layer, be[i], 0, 0)),
                  pl.BlockSpec((1, 1, 1, F2), lambda i, be, nb: (layer, be[i], 0, 0)),
                  pl.BlockSpec((1, 1, F, D), lambda i, be, nb: (layer, be[i], 0, 0)),
                  pl.BlockSpec((1, 1, 1, D), lambda i, be, nb: (layer, be[i], 0, 0))],
        out_specs=pl.BlockSpec((tm, D), lambda i, be, nb: (i, 0)),
        scratch_shapes=[pltpu.VMEM((D, F2), BF16), pltpu.VMEM((F, D), BF16)],
    )
    return pl.pallas_call(
        _moe_kernel,
        grid_spec=grid_spec,
        out_shape=jax.ShapeDtypeStruct((P, D), F32),
        compiler_params=_cparams(("arbitrary",)),
        name="moe_experts",
    )(block_e, n_used, xs, w_gu, b_gu.reshape(L, E, 1, F2), w_down, b_down.reshape(L, E, 1, D))


def _combine_kernel(dest_ref, dnext_ref, base_ref, gate_ref, y_ref, g_ref, b_ref, o_ref, ybuf, sems):
    tm = base_ref.shape[0]
    i = pl.program_id(0)
    n = pl.num_programs(0)
    slot = i % 2

    def gather(idx_ref, sl):
        def issue(r, carry):
            for k in range(TOP_K):
                pltpu.make_async_copy(y_ref.at[pl.ds(idx_ref[r * TOP_K + k], 1), :],
                                      ybuf.at[sl, k, pl.ds(r, 1), :], sems.at[sl]).start()
            return carry
        lax.fori_loop(0, tm, issue, 0, unroll=DMA_UNROLL)

    @pl.when(i == 0)
    def _():
        gather(dest_ref, 0)

    @pl.when(i + 1 < n)
    def _():
        gather(dnext_ref, 1 - slot)

    for k in range(TOP_K):
        pltpu.make_async_copy(y_ref.at[pl.ds(0, tm), :], ybuf.at[slot, k], sems.at[slot]).wait()
    gate = gate_ref[...]
    acc = base_ref[...]
    for k in range(TOP_K):
        acc = acc + gate[:, k:k + 1] * ybuf[slot, k]
    o_ref[...] = _layer_norm(acc, g_ref[...], b_ref[...])


def _combine_call(dest_flat, base, gates, y, ln_g, ln_b):
    T, D = base.shape
    tm = ROW_BLOCK
    rows = lambda w: pl.BlockSpec((tm, w), lambda i: (i, 0))
    const = lambda i: (0, 0)
    nblk = T // tm
    return pl.pallas_call(
        _combine_kernel,
        grid=(nblk,),
        in_specs=[pl.BlockSpec((tm * TOP_K,), lambda i: (i,), memory_space=pltpu.SMEM),
                  pl.BlockSpec((tm * TOP_K,), lambda i: (jnp.minimum(i + 1, nblk - 1),),
                               memory_space=pltpu.SMEM),
                  rows(D), rows(TOP_K), pl.BlockSpec(memory_space=pl.ANY),
                  pl.BlockSpec((1, D), const), pl.BlockSpec((1, D), const)],
        out_specs=rows(D),
        out_shape=jax.ShapeDtypeStruct((T, D), F32),
        scratch_shapes=[pltpu.VMEM((2, TOP_K, tm, D), F32), pltpu.SemaphoreType.DMA((2,))],
        compiler_params=_cparams(("arbitrary",)),
        name="combine_ln",
    )(dest_flat, dest_flat, base, gates, y, ln_g.reshape(1, D), ln_b.reshape(1, D))


def _moe_ffn(alpha, layer, h1, p_i, idx, gates, rank, counts, w_pg, w_pp, w_gu, b_gu, w_down,
             b_down, ln_g, ln_b):
    T, D = h1.shape
    E = w_gu.shape[1]
    A = T * TOP_K
    counts = counts.reshape(E)
    padded = (counts + MOE_BLOCK - 1) // MOE_BLOCK * MOE_BLOCK
    pad_end = jnp.cumsum(padded)
    pad_start = pad_end - padded
    n_blocks = -(-A // MOE_BLOCK) + E
    P = n_blocks * MOE_BLOCK
    experts = jnp.arange(E, dtype=I32)
    dest = jnp.sum(jnp.where(idx[:, :, None] == experts, pad_start, 0), axis=-1) + rank
    dest_flat = dest.reshape(A).astype(I32)
    blk_start = jnp.arange(n_blocks, dtype=I32) * MOE_BLOCK
    block_e = jnp.minimum(jnp.sum((pad_end[None, :] <= blk_start[:, None]).astype(I32), axis=1),
                          E - 1).astype(I32)
    n_used = (pad_end[-1] // MOE_BLOCK).astype(I32).reshape(1)
    base, xs = _dispatch_call(alpha, dest_flat, h1, p_i, w_pg, w_pp, P)
    y = _moe_call(layer, block_e, n_used, xs, w_gu, b_gu, w_down, b_down)
    return _combine_call(dest_flat, base, gates, y, ln_g, ln_b)


def _layer(layer, h, p_i, B, S, alpha, w_main, wg, m_i_bias, m_f_bias, m_norm_w, sb_norm_w, g_conv_w,
           g_A_log, g_dt_bias, g_norm_w, w_out, ln1_g, ln1_b, w_router, b_router, w_gu, b_gu,
           w_down, b_down, w_pg, w_pp, ln2_g, ln2_b):
    u, g, gt = _inproj_call(h, w_main, wg)
    ym = _mlstm_call(u, g, gt, m_i_bias, m_f_bias, m_norm_w, B, S)
    ys = _sb_call(u, sb_norm_w, B, S, (4 * M_WIDTH + 4 * G_WIDTH) // SB_WIDTH)
    yg = _gdn_call(u, g, gt, g_conv_w, g_A_log, g_dt_bias, g_norm_w, B, S, 4 * M_WIDTH // G_WIDTH)
    h1, idx, gates, rank, counts = _post_call(
        alpha, h, ym, ys, yg, w_out, w_router, b_router, ln1_g, ln1_b)
    return _moe_ffn(alpha, layer, h1, p_i, idx, gates, rank, counts, w_pg, w_pp, w_gu, b_gu,
                    w_down, b_down, ln2_g, ln2_b)


def kernel(x, p, ln0_g, ln0_b, w_in, m_i_bias, m_f_bias, m_norm_w, sb_norm_w, g_conv_w, g_A_log,
           g_dt_bias, g_norm_w, w_out, ln1_g, ln1_b, w_router, b_router, w_gu, b_gu, w_down,
           b_down, w_ple_gate, w_ple_proj, ln2_g, ln2_b):
    B, S, D = x.shape
    depth = w_in.shape[0]
    T = B * S
    alpha = (2 * depth) ** 0.25
    g0 = 4 * M_WIDTH
    g1 = g0 + 2 * M_HEADS
    s1 = g1 + 3 * SB_WIDTH
    g2 = s1 + 4 * G_WIDTH
    h = _ln_call(x.reshape(T, D), ln0_g, ln0_b)
    for i in range(depth):
        w = w_in[i]
        w_main = jnp.concatenate([w[:, :g0], w[:, s1:g2], w[:, g1:s1]], axis=1).astype(BF16)
        wg = jnp.concatenate([w[:, g0:g1], w[:, g2:]], axis=1)
        h = _layer(i, h, p[i].reshape(T, -1), B, S, alpha, w_main, wg, m_i_bias[i], m_f_bias[i],
                   m_norm_w[i], sb_norm_w[i], g_conv_w[i], g_A_log[i], g_dt_bias[i], g_norm_w[i],
                   w_out[i].astype(BF16), ln1_g[i], ln1_b[i], w_router[i], b_router[i],
                   w_gu, b_gu, w_down, b_down,
                   w_ple_gate[i].astype(BF16), w_ple_proj[i].astype(BF16), ln2_g[i], ln2_b[i])
    return h.reshape(B, S, D)
```

```python
import functools
import math

import jax
import jax.numpy as jnp
from jax import lax
from jax.experimental import pallas as pl
from jax.experimental.pallas import tpu as pltpu

F32 = jnp.float32
BF16 = jnp.bfloat16
I32 = jnp.int32

M_HEADS, M_HEAD_DIM = 4, 64
SB_HEADS, SB_HEAD_DIM = 4, 64
G_HEADS, G_HEAD_DIM = 4, 128
G_CONV = 4
M_WIDTH = M_HEADS * M_HEAD_DIM
SB_WIDTH = SB_HEADS * SB_HEAD_DIM
G_WIDTH = G_HEADS * G_HEAD_DIM
N_GATE_COLS = 16
TOP_K = 4
SWIGLU_LIMIT = 7.0
SWIGLU_ALPHA = 1.702
LN_EPS = 1e-5
NORM_EPS = 1e-6

ROW_BLOCK = 256
POST_BLOCK = 512
M_CHUNK = 256
SB_BLOCK = 256
G_CHUNK = 64
G_BLOCK = 256
MOE_BLOCK = 512
DMA_UNROLL = 8
VMEM_LIMIT = 56 * 1024 * 1024


def _cparams(sem):
    return pltpu.CompilerParams(dimension_semantics=sem, vmem_limit_bytes=VMEM_LIMIT)


def _split3(x):
    hi = x.astype(BF16)
    r1 = x - hi.astype(F32)
    mid = r1.astype(BF16)
    lo = (r1 - mid.astype(F32)).astype(BF16)
    return hi, mid, lo


def _split2(x):
    hi = x.astype(BF16)
    lo = (x - hi.astype(F32)).astype(BF16)
    return hi, lo


def _dot(a, b):
    return jnp.dot(a, b, preferred_element_type=F32)


def _dot_nt(a, b):
    return lax.dot_general(a, b, (((1,), (1,)), ((), ())), preferred_element_type=F32)


def _dot_tn(a, b):
    return lax.dot_general(a, b, (((0,), (0,)), ((), ())), preferred_element_type=F32)


def _dot3(a, b):
    ah, al = _split2(a)
    bh, bl = _split2(b)
    return _dot(ah, bh) + (_dot(ah, bl) + _dot(al, bh))


def _dot3_nt(a, b):
    ah, al = _split2(a)
    bh, bl = _split2(b)
    return _dot_nt(ah, bh) + (_dot_nt(ah, bl) + _dot_nt(al, bh))


def _bdot(a, b):
    return lax.dot_general(a, b, (((2,), (1,)), ((0,), (0,))), preferred_element_type=F32)


def _bdot_nt(a, b):
    return lax.dot_general(a, b, (((2,), (2,)), ((0,), (0,))), preferred_element_type=F32)


def _bdot_tn(a, b):
    return lax.dot_general(a, b, (((1,), (1,)), ((0,), (0,))), preferred_element_type=F32)


def _bdot3(a, b):
    ah, al = _split2(a)
    bh, bl = _split2(b)
    return _bdot(ah, bh) + (_bdot(ah, bl) + _bdot(al, bh))


def _bdot3_nt(a, b):
    ah, al = _split2(a)
    bh, bl = _split2(b)
    return _bdot_nt(ah, bh) + (_bdot_nt(ah, bl) + _bdot_nt(al, bh))


def _exact_left(mask_bf16, x):
    hi, mid, lo = _split3(x)
    return _dot(mask_bf16, hi) + (_dot(mask_bf16, mid) + _dot(mask_bf16, lo))


def _exact_right(x, mask_bf16):
    hi, mid, lo = _split3(x)
    return _dot(hi, mask_bf16) + (_dot(mid, mask_bf16) + _dot(lo, mask_bf16))


def _softplus(x):
    return jnp.maximum(x, 0.0) + jnp.log1p(jnp.exp(-jnp.abs(x)))


def _log_sigmoid(x):
    return -_softplus(-x)


def _sigmoid(x):
    return 1.0 / (1.0 + jnp.exp(-x))


def _layer_norm(x, g, b):
    mu = jnp.mean(x, axis=-1, keepdims=True)
    xc = x - mu
    var = jnp.mean(xc * xc, axis=-1, keepdims=True)
    return xc * lax.rsqrt(var + LN_EPS) * g + b


def _iota2(shape, dim):
    return lax.broadcasted_iota(I32, shape, dim)


def _ln_kernel(x_ref, g_ref, b_ref, o_ref):
    o_ref[...] = _layer_norm(x_ref[...], g_ref[...], b_ref[...])


def _ln_call(x, g, b):
    T, D = x.shape
    tm = ROW_BLOCK
    return pl.pallas_call(
        _ln_kernel,
        grid=(T // tm,),
        in_specs=[pl.BlockSpec((tm, D), lambda i: (i, 0)),
                  pl.BlockSpec((1, D), lambda i: (0, 0)),
                  pl.BlockSpec((1, D), lambda i: (0, 0))],
        out_specs=pl.BlockSpec((tm, D), lambda i: (i, 0)),
        out_shape=jax.ShapeDtypeStruct((T, D), F32),
        compiler_params=_cparams(("parallel",)),
        name="embed_ln",
    )(x, g.reshape(1, D), b.reshape(1, D))


def _inproj_kernel(h_ref, w_ref, wgh_ref, wgl_ref, wgth_ref, wgtl_ref, u_ref, g_ref, gt_ref):
    h = h_ref[...]
    hh, hl = _split2(h)
    u_ref[...] = _dot(hh, w_ref[...])
    wgh, wgl = wgh_ref[...], wgl_ref[...]
    g_ref[...] = _dot(hh, wgh) + (_dot(hh, wgl) + _dot(hl, wgh))
    wgth, wgtl = wgth_ref[...], wgtl_ref[...]
    gt_ref[...] = _dot_nt(wgth, hh) + (_dot_nt(wgtl, hh) + _dot_nt(wgth, hl))


def _inproj_call(h, w_main, wg):
    T, D = h.shape
    N = w_main.shape[1]
    tm = ROW_BLOCK
    wgh, wgl = _split2(wg)
    wgt = wg.T
    wgth, wgtl = _split2(wgt)
    G = N_GATE_COLS
    const = lambda i: (0, 0)
    return pl.pallas_call(
        _inproj_kernel,
        grid=(T // tm,),
        in_specs=[pl.BlockSpec((tm, D), lambda i: (i, 0)),
                  pl.BlockSpec((D, N), const),
                  pl.BlockSpec((D, G), const), pl.BlockSpec((D, G), const),
                  pl.BlockSpec((G, D), const), pl.BlockSpec((G, D), const)],
        out_specs=[pl.BlockSpec((tm, N), lambda i: (i, 0)),
                   pl.BlockSpec((tm, G), lambda i: (i, 0)),
                   pl.BlockSpec((G, tm), lambda i: (0, i))],
        out_shape=[jax.ShapeDtypeStruct((T, N), F32),
                   jax.ShapeDtypeStruct((T, G), F32),
                   jax.ShapeDtypeStruct((G, T), F32)],
        compiler_params=_cparams(("parallel",)),
        name="in_proj",
    )(h, w_main, wgh, wgl, wgth, wgtl)


def _mlstm_kernel(q_ref, k_ref, v_ref, o_ref, g_ref, gt_ref, brow_ref, bcol_ref, nw_ref,
                  y_ref, c_sc, m_sc):
    L = q_ref.shape[0]
    d = M_HEAD_DIM

    @pl.when(pl.program_id(1) == 0)
    def _():
        c_sc[...] = jnp.zeros_like(c_sc)
        m_sc[...] = jnp.full_like(m_sc, -jnp.inf)

    row = _iota2((L, L), 0)
    col = _iota2((L, L), 1)
    causal = col <= row
    tril = causal.astype(BF16)
    triu = (row <= col).astype(BF16)

    gb = g_ref[...] + brow_ref[...]
    gtb = gt_ref[...] + bcol_ref[...]
    b_cols = _exact_left(tril, _log_sigmoid(gb))
    b_rows = _exact_right(_log_sigmoid(gtb), triu)

    ones_col = (_iota2((L, d), 1) == 0).astype(F32)
    H = range(M_HEADS)
    sl = [slice(hd * d, (hd + 1) * d) for hd in H]
    q = [q_ref[:, c].astype(BF16) for c in sl]
    kf = [k_ref[:, c] * (d ** -0.5) for c in sl]
    k = [x.astype(BF16) for x in kf]
    v_aug = [jnp.concatenate([v_ref[:, c], ones_col], axis=1).astype(BF16) for c in sl]
    bc = [b_cols[:, M_HEADS + hd:M_HEADS + hd + 1] for hd in H]
    lic = [gb[:, hd:hd + 1] for hd in H]
    src = [gtb[hd:hd + 1, :] - b_rows[M_HEADS + hd:M_HEADS + hd + 1, :] for hd in H]
    b_end = [x[L - 1:L, :] for x in bc]
    m_prev = [m_sc[hd] for hd in H]
    c_aug = [c_sc[hd] for hd in H]

    qk = [_dot_nt(q[h], k[h]) for h in H]
    qc = [_dot(q[h], c_aug[h].astype(BF16)) for h in H]
    log_w = [jnp.where(causal, bc[h] + src[h], -jnp.inf) for h in H]
    m_intra = [jnp.max(x, axis=1, keepdims=True) for x in log_w]
    carry_log = [bc[h] + m_prev[h] for h in H]
    m_t = [jnp.maximum(carry_log[h], m_intra[h]) for h in H]
    inter = [jnp.exp(carry_log[h] - m_t[h]) for h in H]
    s = [(qk[h] * jnp.exp(log_w[h] - m_t[h])).astype(BF16) for h in H]
    num_aug = [inter[h] * qc[h] + _dot(s[h], v_aug[h]) for h in H]
    hh = [num_aug[h][:, :d] / jnp.maximum(jnp.abs(num_aug[h][:, d:d + 1]), jnp.exp(-m_t[h]))
          for h in H]

    m_end = [jnp.max(b_end[h] + src[h], axis=1, keepdims=True) for h in H]
    m_new = [jnp.maximum(b_end[h] + m_prev[h], m_end[h]) for h in H]
    wk = [(jnp.exp(b_end[h] - bc[h] + lic[h] - m_new[h]) * kf[h]).astype(BF16) for h in H]
    kv = [_dot_tn(wk[h], v_aug[h]) for h in H]
    for h in H:
        c_sc[h] = jnp.exp(b_end[h] + m_prev[h] - m_new[h]) * c_aug[h] + kv[h]
        m_sc[h] = m_new[h]

    outs = [x * lax.rsqrt(jnp.mean(x * x, axis=1, keepdims=True) + NORM_EPS) for x in hh]
    hcat = jnp.concatenate(outs, axis=1) * nw_ref[...]
    y_ref[...] = _sigmoid(o_ref[...]) * hcat


def _mlstm_call(u, g, gt, i_bias, f_bias, norm_w, B, S):
    T = u.shape[0]
    L = min(M_CHUNK, S)
    nc = S // L
    W = M_WIDTH
    zeros = jnp.zeros((N_GATE_COLS - 2 * M_HEADS,), F32)
    bias = jnp.concatenate([i_bias, f_bias, zeros])
    G = N_GATE_COLS
    rowblk = lambda c: pl.BlockSpec((L, W), lambda b, j, c=c: (b * nc + j, c))
    const = lambda b, j: (0, 0)
    return pl.pallas_call(
        _mlstm_kernel,
        grid=(B, nc),
        in_specs=[rowblk(0), rowblk(1), rowblk(2), rowblk(3),
                  pl.BlockSpec((L, G), lambda b, j: (b * nc + j, 0)),
                  pl.BlockSpec((G, L), lambda b, j: (0, b * nc + j)),
                  pl.BlockSpec((1, G), const), pl.BlockSpec((G, 1), const),
                  pl.BlockSpec((1, W), const)],
        out_specs=pl.BlockSpec((L, W), lambda b, j: (b * nc + j, 0)),
        out_shape=jax.ShapeDtypeStruct((T, W), F32),
        scratch_shapes=[pltpu.VMEM((M_HEADS, M_HEAD_DIM, 2 * M_HEAD_DIM), F32),
                        pltpu.VMEM((M_HEADS, 1, 1), F32)],
        compiler_params=_cparams(("parallel", "arbitrary")),
        name="mlstm",
    )(u, u, u, u, g, gt, bias.reshape(1, G), bias.reshape(G, 1), norm_w.reshape(1, W))


def _sb_kernel(q_ref, k_ref, v_ref, nw_ref, y_ref, qb_sc, acc_sc, r_sc):
    tq = q_ref.shape[0]
    tk = tq
    d = SB_HEAD_DIM
    qi = pl.program_id(1)
    srow = _iota2((tq, tk), 0)
    scol = _iota2((tq, tk), 1)
    later = (scol < srow).astype(BF16)

    qb_sc[...] = (q_ref[...] * (d ** -0.5)).astype(BF16)
    acc_sc[...] = jnp.zeros_like(acc_sc)
    r_sc[...] = jnp.zeros_like(r_sc)

    def block(kb, diagonal):
        off = pl.multiple_of(kb * tk, tk)
        heads = range(SB_HEADS)
        cols = [slice(hd * d, (hd + 1) * d) for hd in heads]
        valid = scol < srow
        z = [_dot_nt(qb_sc[:, c], k_ref[pl.ds(off, tk), c].astype(BF16)) for c in cols]
        sp = [jnp.maximum(x, 0.0) + jnp.log(1.0 + jnp.exp(-jnp.abs(x))) for x in z]
        if diagonal:
            sp = [jnp.where(valid, x, 0.0) for x in sp]
        c2 = [_dot(jnp.concatenate(_split2(x), axis=0), later) for x in sp]
        cs = [x[:tq] + x[tq:] for x in c2]
        a = [jnp.exp(z[h] - sp[h] - cs[h]) for h in heads]
        if diagonal:
            a = [jnp.where(valid, x, 0.0) for x in a]
        pv = [_dot(a[h].astype(BF16), v_ref[pl.ds(off, tk), cols[h]].astype(BF16)) for h in heads]
        for h in heads:
            rest = r_sc[h]
            acc_sc[h] = acc_sc[h] + jnp.exp(-rest) * pv[h]
            r_sc[h] = rest + (cs[h][:, 0:1] + sp[h][:, 0:1])

    block(qi, True)

    def body(it, carry):
        block(qi - it, False)
        return carry

    lax.fori_loop(1, qi + 1, body, 0)
    outs = []
    for hd in range(SB_HEADS):
        o = acc_sc[hd]
        outs.append(o * lax.rsqrt(jnp.mean(o * o, axis=1, keepdims=True) + NORM_EPS))
    y_ref[...] = jnp.concatenate(outs, axis=1) * nw_ref[...]


def _sb_call(u, norm_w, B, S, col0):
    T = u.shape[0]
    tq = min(SB_BLOCK, S)
    nq = S // tq
    W = SB_WIDTH
    return pl.pallas_call(
        _sb_kernel,
        grid=(B, nq),
        in_specs=[pl.BlockSpec((tq, W), lambda b, i: (b * nq + i, col0)),
                  pl.BlockSpec((S, W), lambda b, i: (b, col0 + 1)),
                  pl.BlockSpec((S, W), lambda b, i: (b, col0 + 2)),
                  pl.BlockSpec((1, W), lambda b, i: (0, 0))],
        out_specs=pl.BlockSpec((tq, W), lambda b, i: (b * nq + i, 0)),
        out_shape=jax.ShapeDtypeStruct((T, W), F32),
        scratch_shapes=[pltpu.VMEM((tq, W), BF16),
                        pltpu.VMEM((SB_HEADS, tq, SB_HEAD_DIM), F32),
                        pltpu.VMEM((SB_HEADS, tq, 1), F32)],
        compiler_params=_cparams(("parallel", "arbitrary")),
        name="stickbreak",
    )(u, u, u, norm_w.reshape(1, W))


def _gdn_kernel(q_ref, k_ref, v_ref, z_ref, g_ref, gt_ref, cw_ref, arow_ref, acol_ref,
                drow_ref, dcol_ref, nw_ref, y_ref, xq_sc, xk_sc, xv_sc, s_sc):
    LB = q_ref.shape[0]
    C = G_CHUNK
    d = G_HEAD_DIM
    W = G_WIDTH
    HALO = 8

    @pl.when(pl.program_id(1) == 0)
    def _():
        s_sc[...] = jnp.zeros_like(s_sc)
        for sc in (xq_sc, xk_sc, xv_sc):
            sc[0:HALO, :] = jnp.zeros((HALO, W), F32)

    @pl.when(pl.program_id(1) != 0)
    def _():
        for sc in (xq_sc, xk_sc, xv_sc):
            sc[0:HALO, :] = sc[LB:LB + HALO, :]

    def conv_silu(x_ref, sc, j):
        sc[HALO:HALO + LB, :] = x_ref[...]
        acc = None
        for t in range(G_CONV):
            w = cw_ref[t:t + 1, j * W:(j + 1) * W]
            term = w * sc[HALO - (G_CONV - 1) + t:HALO - (G_CONV - 1) + t + LB, :]
            acc = term if acc is None else acc + term
        return acc * _sigmoid(acc)

    qc = conv_silu(q_ref, xq_sc, 0)
    kc = conv_silu(k_ref, xk_sc, 1)
    vc = conv_silu(v_ref, xv_sc, 2)

    gb = g_ref[...]
    gtb = gt_ref[...]
    gdec_cols = -jnp.exp(arow_ref[...]) * _softplus(gb + drow_ref[...])
    gdec_rows = -jnp.exp(acol_ref[...]) * _softplus(gtb + dcol_ref[...])
    beta_cols = _sigmoid(gb)
    row = _iota2((LB, LB), 0)
    col = _iota2((LB, LB), 1)
    shift = C.bit_length() - 1
    same = jnp.right_shift(row, shift) == jnp.right_shift(col, shift)
    tril = (same & (col <= row)).astype(BF16)
    triu = (same & (row <= col)).astype(BF16)
    gam_cols = _exact_left(tril, gdec_cols)
    gam_rows = _exact_right(gdec_rows, triu)

    nc = LB // C
    NB = G_HEADS * nc

    def per_head(x):
        return jnp.stack([x[:, h * d:(h + 1) * d] for h in range(G_HEADS)], axis=0).reshape(NB, C, d)

    q3 = per_head(qc)
    k3 = per_head(kc)
    v3 = per_head(vc)
    q3 = q3 * lax.rsqrt(jnp.sum(q3 * q3, axis=2, keepdims=True) + NORM_EPS) * (d ** -0.5)
    k3 = k3 * lax.rsqrt(jnp.sum(k3 * k3, axis=2, keepdims=True) + NORM_EPS)
    ga0 = 2 * G_HEADS
    gb0 = 3 * G_HEADS
    gam_c = jnp.stack([gam_cols[:, ga0 + h:ga0 + h + 1] for h in range(G_HEADS)], 0).reshape(NB, C, 1)
    beta = jnp.stack([beta_cols[:, gb0 + h:gb0 + h + 1] for h in range(G_HEADS)], 0).reshape(NB, C, 1)
    gam_r = jnp.stack([gam_rows[ga0 + h:ga0 + h + 1, c * C:(c + 1) * C]
                       for h in range(G_HEADS) for c in range(nc)], 0)
    gam_end = gam_c[:, C - 1:C, :]

    r64 = _iota2((1, C, C), 1)
    c64 = _iota2((1, C, C), 2)
    incl = c64 <= r64
    strict = c64 < r64

    decay = jnp.exp(jnp.where(incl, gam_c - gam_r, -jnp.inf))
    kk = _bdot3_nt(k3, k3)
    m = jnp.where(strict, beta * kk * decay, 0.0)
    tinv = jnp.broadcast_to((r64 == c64).astype(F32), (NB, C, C))
    s = 1
    while s < C:
        sh = s.bit_length() - 1
        same2s = jnp.right_shift(r64, sh + 1) == jnp.right_shift(c64, sh + 1)
        low_mask = same2s & ((jnp.right_shift(r64, sh) & 1) == 1) & ((jnp.right_shift(c64, sh) & 1) == 0)
        low = jnp.where(low_mask, m, 0.0).astype(BF16)
        tb = tinv.astype(BF16)
        tinv = tinv - _bdot(_bdot(tb, low).astype(BF16), tb)
        s *= 2
    eg = jnp.exp(gam_c)
    tb = tinv.astype(BF16)
    u = _bdot(tb, (v3 * beta).astype(BF16))
    w = _bdot(tb, (k3 * (beta * eg)).astype(BF16)).astype(BF16)
    qk = (_bdot_nt(q3.astype(BF16), k3.astype(BF16)) * decay).astype(BF16)
    q_dec = (q3 * eg).astype(BF16)
    k_dec = (k3 * jnp.exp(gam_end - gam_c)).astype(BF16)
    cdec = jnp.exp(gam_end)

    def chunk(x, c):
        return x.reshape((G_HEADS, nc) + x.shape[1:])[:, c]

    state = s_sc[...]
    o_chunks = []
    for c in range(nc):
        sb = state.astype(BF16)
        v_new = chunk(u, c) - _bdot(chunk(w, c), sb)
        vb = v_new.astype(BF16)
        o_chunks.append(_bdot(chunk(q_dec, c), sb) + _bdot(chunk(qk, c), vb))
        state = chunk(cdec, c) * state + _bdot_tn(chunk(k_dec, c), vb)
    s_sc[...] = state

    nw = nw_ref[...]
    o = jnp.concatenate(o_chunks, axis=1)
    o = o * lax.rsqrt(jnp.mean(o * o, axis=2, keepdims=True) + NORM_EPS) * nw
    zz = z_ref[...]
    y_ref[...] = jnp.concatenate([o[h] for h in range(G_HEADS)], axis=1) * (zz * _sigmoid(zz))


def _gdn_call(u, g, gt, conv_w, a_log, dt_bias, norm_w, B, S, col0):
    T = u.shape[0]
    LB = min(G_BLOCK, S)
    nb = S // LB
    W = G_WIDTH
    G = N_GATE_COLS
    pad = jnp.zeros((2 * G_HEADS,), F32)
    arow = jnp.concatenate([pad, a_log, jnp.zeros((G_HEADS,), F32)])
    drow = jnp.concatenate([pad, dt_bias, jnp.zeros((G_HEADS,), F32)])
    rowblk = lambda c: pl.BlockSpec((LB, W), lambda b, j, c=c: (b * nb + j, col0 + c))
    const = lambda b, j: (0, 0)
    return pl.pallas_call(
        _gdn_kernel,
        grid=(B, nb),
        in_specs=[rowblk(0), rowblk(1), rowblk(2), rowblk(3),
                  pl.BlockSpec((LB, G), lambda b, j: (b * nb + j, 0)),
                  pl.BlockSpec((G, LB), lambda b, j: (0, b * nb + j)),
                  pl.BlockSpec((G_CONV, 3 * W), const),
                  pl.BlockSpec((1, G), const), pl.BlockSpec((G, 1), const),
                  pl.BlockSpec((1, G), const), pl.BlockSpec((G, 1), const),
                  pl.BlockSpec((1, G_HEAD_DIM), const)],
        out_specs=pl.BlockSpec((LB, W), lambda b, j: (b * nb + j, 0)),
        out_shape=jax.ShapeDtypeStruct((T, W), F32),
        scratch_shapes=[pltpu.VMEM((LB + 8, W), F32), pltpu.VMEM((LB + 8, W), F32),
                        pltpu.VMEM((LB + 8, W), F32),
                        pltpu.VMEM((G_HEADS, G_HEAD_DIM, G_HEAD_DIM), F32)],
        compiler_params=_cparams(("parallel", "arbitrary")),
        name="gdn",
    )(u, u, u, u, g, gt, conv_w, arow.reshape(1, G), arow.reshape(G, 1),
      drow.reshape(1, G), drow.reshape(G, 1), norm_w.reshape(1, G_HEAD_DIM))


def _post_kernel(alpha, h_ref, ym_ref, ys_ref, yg_ref, wo_ref,
                 wrh_ref, wrl_ref, br_ref, g1_ref, b1_ref,
                 h1_ref, idx_ref, gate_ref, rank_ref, cnt_ref, cnt_sc):
    tm = h_ref.shape[0]
    E = br_ref.shape[1]

    @pl.when(pl.program_id(0) == 0)
    def _():
        cnt_sc[...] = jnp.zeros_like(cnt_sc)

    y = jnp.concatenate([ym_ref[...], ys_ref[...], yg_ref[...]], axis=1).astype(BF16)
    a = alpha * h_ref[...] + _dot(y, wo_ref[...])
    h1 = _layer_norm(a, g1_ref[...], b1_ref[...])
    h1h, h1l = _split2(h1)
    h1_ref[...] = h1

    wrh, wrl = wrh_ref[...], wrl_ref[...]
    logits = _dot(h1h, wrh) + (_dot(h1h, wrl) + _dot(h1l, wrh)) + br_ref[...]
    lane = _iota2((tm, E), 1).astype(F32)
    work = logits
    vals, idxs = [], []
    anyhot = jnp.zeros((tm, E), F32)
    for _ in range(TOP_K):
        mx = jnp.max(work, axis=1, keepdims=True)
        ix = jnp.min(jnp.where(work == mx, lane, float(E)), axis=1, keepdims=True)
        sel = lane == ix
        vals.append(mx)
        idxs.append(ix)
        anyhot = jnp.where(sel, 1.0, anyhot)
        work = jnp.where(sel, -jnp.inf, work)
    ex = [jnp.exp(v - vals[0]) for v in vals]
    tot = ex[0] + ex[1] + ex[2] + ex[3]
    gates = [e / tot for e in ex]

    r = _iota2((tm, tm), 0)
    c = _iota2((tm, tm), 1)
    before = (c < r).astype(BF16)
    pos = cnt_sc[...] + _dot(before, anyhot.astype(BF16))
    ranks = [jnp.sum(jnp.where(lane == ix, pos, 0.0), axis=1, keepdims=True) for ix in idxs]
    cnt_sc[...] = cnt_sc[...] + jnp.sum(anyhot, axis=0, keepdims=True)
    cnt_ref[...] = cnt_sc[...].astype(I32)

    kl = _iota2((tm, TOP_K), 1)

    def pack(cols):
        out = jnp.broadcast_to(cols[0], (tm, TOP_K))
        for k in range(1, TOP_K):
            out = jnp.where(kl == k, cols[k], out)
        return out

    idx_ref[...] = pack(idxs).astype(I32)
    gate_ref[...] = pack(gates)
    rank_ref[...] = pack(ranks).astype(I32)


def _post_call(alpha, h, ym, ys, yg, w_out, w_router, b_router, ln_g, ln_b):
    T, D = h.shape
    tm = POST_BLOCK
    E = w_router.shape[1]
    wrh, wrl = _split2(w_router)
    const = lambda i: (0, 0)
    rows = lambda w: pl.BlockSpec((tm, w), lambda i: (i, 0))
    return pl.pallas_call(
        functools.partial(_post_kernel, alpha),
        grid=(T // tm,),
        in_specs=[rows(D), rows(M_WIDTH), rows(SB_WIDTH), rows(G_WIDTH),
                  pl.BlockSpec(w_out.shape, const),
                  pl.BlockSpec((D, E), const), pl.BlockSpec((D, E), const),
                  pl.BlockSpec((1, E), const), pl.BlockSpec((1, D), const),
                  pl.BlockSpec((1, D), const)],
        out_specs=[rows(D), rows(TOP_K), rows(TOP_K), rows(TOP_K),
                   pl.BlockSpec((1, E), const)],
        out_shape=[jax.ShapeDtypeStruct((T, D), F32),
                   jax.ShapeDtypeStruct((T, TOP_K), I32), jax.ShapeDtypeStruct((T, TOP_K), F32),
                   jax.ShapeDtypeStruct((T, TOP_K), I32), jax.ShapeDtypeStruct((1, E), I32)],
        scratch_shapes=[pltpu.VMEM((1, E), F32)],
        compiler_params=_cparams(("arbitrary",)),
        name="post_mix",
    )(h, ym, ys, yg, w_out, wrh, wrl, b_router.reshape(1, E),
      ln_g.reshape(1, D), ln_b.reshape(1, D))


def _row_copy(src, s, dst, d, sem):
    return pltpu.make_async_copy(src.at[pl.ds(s, 1), :], dst.at[pl.ds(d, 1), :], sem)


def _dispatch_kernel(alpha, dest_ref, h1_ref, p_ref, wpg_ref, wpp_ref, xs_in_ref,
                     base_ref, xs_ref, xpk_sc, sem):
    del xs_in_ref
    tm, D = h1_ref.shape
    half = D // 2
    h1 = h1_ref[...]
    hb = h1.astype(BF16)
    bits = lax.bitcast_convert_type(hb.astype(F32), jnp.uint32)
    xpk_sc[...] = (bits[:, :half] >> 16) | (bits[:, half:] & jnp.uint32(0xFFFF0000))

    def issue(r, carry):
        for k in range(TOP_K):
            _row_copy(xpk_sc, r, xs_ref, dest_ref[r * TOP_K + k], sem).start()
        return carry

    lax.fori_loop(0, tm, issue, 0, unroll=DMA_UNROLL)
    ple = _sigmoid(_dot(hb, wpg_ref[...])) * _dot(p_ref[...].astype(BF16), wpp_ref[...])
    base_ref[...] = alpha * h1 + ple
    for k in range(TOP_K):
        pltpu.make_async_copy(xpk_sc, xs_ref.at[pl.ds(0, tm), :], sem).wait()


def _dispatch_call(alpha, dest_flat, h1, p, w_pg, w_pp, n_slots):
    T, D = h1.shape
    tm = ROW_BLOCK
    P = p.shape[1]
    const = lambda i: (0, 0)
    rows = lambda w: pl.BlockSpec((tm, w), lambda i: (i, 0))
    xs0 = jnp.zeros((n_slots, D // 2), jnp.uint32)
    base, xs = pl.pallas_call(
        functools.partial(_dispatch_kernel, alpha),
        grid=(T // tm,),
        in_specs=[pl.BlockSpec((tm * TOP_K,), lambda i: (i,), memory_space=pltpu.SMEM),
                  rows(D), rows(P), pl.BlockSpec((D, D), const), pl.BlockSpec((P, D), const),
                  pl.BlockSpec(memory_space=pl.ANY)],
        out_specs=[rows(D), pl.BlockSpec(memory_space=pl.ANY)],
        out_shape=[jax.ShapeDtypeStruct((T, D), F32),
                   jax.ShapeDtypeStruct((n_slots, D // 2), jnp.uint32)],
        scratch_shapes=[pltpu.VMEM((tm, D // 2), jnp.uint32), pltpu.SemaphoreType.DMA],
        input_output_aliases={5: 1},
        compiler_params=_cparams(("arbitrary",)),
        name="ple_dispatch",
    )(dest_flat, h1, p, w_pg, w_pp, xs0)
    return base, xs


def _moe_kernel(be_ref, nb_ref, x_ref, wgu_ref, bgu_ref, wd_ref, bd_ref, y_ref, wgu_sc, wd_sc):
    i = pl.program_id(0)
    F = wd_ref.shape[2]
    CH = 128

    new_expert = jnp.logical_or(i == 0, be_ref[i] != be_ref[jnp.maximum(i - 1, 0)])

    @pl.when(jnp.logical_and(new_expert, i < nb_ref[0]))
    def _():
        def cast_gu(r, c):
            rows = pl.ds(pl.multiple_of(r * CH, CH), CH)
            wgu_sc[rows, :] = wgu_ref[0, 0, rows, :].astype(BF16)
            return c

        def cast_d(r, c):
            rows = pl.ds(pl.multiple_of(r * CH, CH), CH)
            wd_sc[rows, :] = wd_ref[0, 0, rows, :].astype(BF16)
            return c

        lax.fori_loop(0, wgu_ref.shape[2] // CH, cast_gu, 0)
        lax.fori_loop(0, F // CH, cast_d, 0)

    @pl.when(i < nb_ref[0])
    def _():
        xw = x_ref[...]
        lo = lax.bitcast_convert_type(xw << 16, F32)
        hi = lax.bitcast_convert_type(xw & jnp.uint32(0xFFFF0000), F32)
        xb = jnp.concatenate([lo, hi], axis=1).astype(BF16)
        h = _dot(xb, wgu_sc[...]) + bgu_ref[0, 0]
        gate = jnp.minimum(h[:, :F], SWIGLU_LIMIT)
        up = jnp.clip(h[:, F:], -SWIGLU_LIMIT, SWIGLU_LIMIT)
        act = (up + 1.0) * gate * _sigmoid(SWIGLU_ALPHA * gate)
        y_ref[...] = _dot(act.astype(BF16), wd_sc[...]) + bd_ref[0, 0]

    @pl.when(i >= nb_ref[0])
    def _():
        y_ref[...] = jnp.zeros_like(y_ref)


def _moe_call(layer, block_e, n_used, xs, w_gu, b_gu, w_down, b_down):
    P = xs.shape[0]
    L, E, D, F2 = w_gu.shape
    F = F2 // 2
    tm = MOE_BLOCK
    nblk = P // tm
    grid_spec = pltpu.PrefetchScalarGridSpec(
        num_scalar_prefetch=2,
        grid=(nblk,),
        in_specs=[pl.BlockSpec((tm, D // 2), lambda i, be, nb: (i, 0)),
                  pl.BlockSpec((1, 1, D, F2), lambda i, be, nb: (layer, be[i], 0, 0)),
                  pl.BlockSpec((1, 1, 1, F2), lambda i, be, nb: (layer, be[i], 0, 0)),
                  pl.BlockSpec((1, 1, F, D), lambda i, be, nb: (layer, be[i], 0, 0)),
                  pl.BlockSpec((1, 1, 1, D), lambda i, be, nb: (layer, be[i], 0, 0))],
        out_specs=pl.BlockSpec((tm, D), lambda i, be, nb: (i, 0)),
        scratch_shapes=[pltpu.VMEM((D, F2), BF16), pltpu.VMEM((F, D), BF16)],
    )
    return pl.pallas_call(
        _moe_kernel,
        grid_spec=grid_spec,
        out_shape=jax.ShapeDtypeStruct((P, D), F32),
        compiler_params=_cparams(("arbitrary",)),
        name="moe_experts",
    )(block_e, n_used, xs, w_gu, b_gu.reshape(L, E, 1, F2), w_down, b_down.reshape(L, E, 1, D))


def _combine_kernel(dest_ref, dnext_ref, base_ref, gate_ref, y_ref, g_ref, b_ref, o_ref, ybuf, sems):
    tm = base_ref.shape[0]
    i = pl.program_id(0)
    n = pl.num_programs(0)
    slot = i % 2

    def gather(idx_ref, sl):
        def issue(r, carry):
            for k in range(TOP_K):
                pltpu.make_async_copy(y_ref.at[pl.ds(idx_ref[r * TOP_K + k], 1), :],
                                      ybuf.at[sl, k, pl.ds(r, 1), :], sems.at[sl]).start()
            return carry
        lax.fori_loop(0, tm, issue, 0, unroll=DMA_UNROLL)

    @pl.when(i == 0)
    def _():
        gather(dest_ref, 0)

    @pl.when(i + 1 < n)
    def _():
        gather(dnext_ref, 1 - slot)

    for k in range(TOP_K):
        pltpu.make_async_copy(y_ref.at[pl.ds(0, tm), :], ybuf.at[slot, k], sems.at[slot]).wait()
    gate = gate_ref[...]
    acc = base_ref[...]
    for k in range(TOP_K):
        acc = acc + gate[:, k:k + 1] * ybuf[slot, k]
    o_ref[...] = _layer_norm(acc, g_ref[...], b_ref[...])


def _combine_call(dest_flat, base, gates, y, ln_g, ln_b):
    T, D = base.shape
    tm = ROW_BLOCK
    rows = lambda w: pl.BlockSpec((tm, w), lambda i: (i, 0))
    const = lambda i: (0, 0)
    nblk = T // tm
    return pl.pallas_call(
        _combine_kernel,
        grid=(nblk,),
        in_specs=[pl.BlockSpec((tm * TOP_K,), lambda i: (i,), memory_space=pltpu.SMEM),
                  pl.BlockSpec((tm * TOP_K,), lambda i: (jnp.minimum(i + 1, nblk - 1),),
                               memory_space=pltpu.SMEM),
                  rows(D), rows(TOP_K), pl.BlockSpec(memory_space=pl.ANY),
                  pl.BlockSpec((1, D), const), pl.BlockSpec((1, D), const)],
        out_specs=rows(D),
        out_shape=jax.ShapeDtypeStruct((T, D), F32),
        scratch_shapes=[pltpu.VMEM((2, TOP_K, tm, D), F32), pltpu.SemaphoreType.DMA((2,))],
        compiler_params=_cparams(("arbitrary",)),
        name="combine_ln",
    )(dest_flat, dest_flat, base, gates, y, ln_g.reshape(1, D), ln_b.reshape(1, D))


def _moe_ffn(alpha, layer, h1, p_i, idx, gates, rank, counts, w_pg, w_pp, w_gu, b_gu, w_down,
             b_down, ln_g, ln_b):
    T, D = h1.shape
    E = w_gu.shape[1]
    A = T * TOP_K
    counts = counts.reshape(E)
    padded = (counts + MOE_BLOCK - 1) // MOE_BLOCK * MOE_BLOCK
    pad_end = jnp.cumsum(padded)
    pad_start = pad_end - padded
    n_blocks = -(-A // MOE_BLOCK) + E
    P = n_blocks * MOE_BLOCK
    experts = jnp.arange(E, dtype=I32)
    dest = jnp.sum(jnp.where(idx[:, :, None] == experts, pad_start, 0), axis=-1) + rank
    dest_flat = dest.reshape(A).astype(I32)
    blk_start = jnp.arange(n_blocks, dtype=I32) * MOE_BLOCK
    block_e = jnp.minimum(jnp.sum((pad_end[None, :] <= blk_start[:, None]).astype(I32), axis=1),
                          E - 1).astype(I32)
    n_used = (pad_end[-1] // MOE_BLOCK).astype(I32).reshape(1)
    base, xs = _dispatch_call(alpha, dest_flat, h1, p_i, w_pg, w_pp, P)
    y = _moe_call(layer, block_e, n_used, xs, w_gu, b_gu, w_down, b_down)
    return _combine_call(dest_flat, base, gates, y, ln_g, ln_b)


def _layer(layer, h, p_i, B, S, alpha, w_main, wg, m_i_bias, m_f_bias, m_norm_w, sb_norm_w, g_conv_w,
           g_A_log, g_dt_bias, g_norm_w, w_out, ln1_g, ln1_b, w_router, b_router, w_gu, b_gu,
           w_down, b_down, w_pg, w_pp, ln2_g, ln2_b):
    u, g, gt = _inproj_call(h, w_main, wg)
    ym = _mlstm_call(u, g, gt, m_i_bias, m_f_bias, m_norm_w, B, S)
    ys = _sb_call(u, sb_norm_w, B, S, (4 * M_WIDTH + 4 * G_WIDTH) // SB_WIDTH)
    yg = _gdn_call(u, g, gt, g_conv_w, g_A_log, g_dt_bias, g_norm_w, B, S, 4 * M_WIDTH // G_WIDTH)
    h1, idx, gates, rank, counts = _post_call(
        alpha, h, ym, ys, yg, w_out, w_router, b_router, ln1_g, ln1_b)
    return _moe_ffn(alpha, layer, h1, p_i, idx, gates, rank, counts, w_pg, w_pp, w_gu, b_gu,
                    w_down, b_down, ln2_g, ln2_b)


def kernel(x, p, ln0_g, ln0_b, w_in, m_i_bias, m_f_bias, m_norm_w, sb_norm_w, g_conv_w, g_A_log,
           g_dt_bias, g_norm_w, w_out, ln1_g, ln1_b, w_router, b_router, w_gu, b_gu, w_down,
           b_down, w_ple_gate, w_ple_proj, ln2_g, ln2_b):
    B, S, D = x.shape
    depth = w_in.shape[0]
    T = B * S
    alpha = (2 * depth) ** 0.25
    g0 = 4 * M_WIDTH
    g1 = g0 + 2 * M_HEADS
    s1 = g1 + 3 * SB_WIDTH
    g2 = s1 + 4 * G_WIDTH
    h = _ln_call(x.reshape(T, D), ln0_g, ln0_b)
    for i in range(depth):
        w = w_in[i]
        w_main = jnp.concatenate([w[:, :g0], w[:, s1:g2], w[:, g1:s1]], axis=1).astype(BF16)
        wg = jnp.concatenate([w[:, g0:g1], w[:, g2:]], axis=1)
        h = _layer(i, h, p[i].reshape(T, -1), B, S, alpha, w_main, wg, m_i_bias[i], m_f_bias[i],
                   m_norm_w[i], sb_norm_w[i], g_conv_w[i], g_A_log[i], g_dt_bias[i], g_norm_w[i],
                   w_out[i].astype(BF16), ln1_g[i], ln1_b[i], w_router[i], b_router[i],
                   w_gu, b_gu, w_down, b_down,
                   w_ple_gate[i].astype(BF16), w_ple_proj[i].astype(BF16), ln2_g[i], ln2_b[i])
    return h.reshape(B, S, D)
```

```python
import functools
import math

import jax
import jax.numpy as jnp
from jax import lax
from jax.experimental import pallas as pl
from jax.experimental.pallas import tpu as pltpu

F32 = jnp.float32
BF16 = jnp.bfloat16
I32 = jnp.int32

M_HEADS, M_HEAD_DIM = 4, 64
SB_HEADS, SB_HEAD_DIM = 4, 64
G_HEADS, G_HEAD_DIM = 4, 128
G_CONV = 4
M_WIDTH = M_HEADS * M_HEAD_DIM
SB_WIDTH = SB_HEADS * SB_HEAD_DIM
G_WIDTH = G_HEADS * G_HEAD_DIM
N_GATE_COLS = 16
TOP_K = 4
SWIGLU_LIMIT = 7.0
SWIGLU_ALPHA = 1.702
LN_EPS = 1e-5
NORM_EPS = 1e-6

ROW_BLOCK = 256
POST_BLOCK = 512
IN_CHUNK = 256
M_CHUNK = 256
SB_BLOCK = 256
G_CHUNK = 64
G_BLOCK = 256
MOE_BLOCK = 512
DMA_UNROLL = 8
VMEM_LIMIT = 56 * 1024 * 1024


def _cparams(sem):
    return pltpu.CompilerParams(dimension_semantics=sem, vmem_limit_bytes=VMEM_LIMIT)


def _split3(x):
    hi = x.astype(BF16)
    r1 = x - hi.astype(F32)
    mid = r1.astype(BF16)
    lo = (r1 - mid.astype(F32)).astype(BF16)
    return hi, mid, lo


def _split2(x):
    hi = x.astype(BF16)
    lo = (x - hi.astype(F32)).astype(BF16)
    return hi, lo


def _dot(a, b):
    return jnp.dot(a, b, preferred_element_type=F32)


def _dot_nt(a, b):
    return lax.dot_general(a, b, (((1,), (1,)), ((), ())), preferred_element_type=F32)


def _dot_tn(a, b):
    return lax.dot_general(a, b, (((0,), (0,)), ((), ())), preferred_element_type=F32)


def _dot3(a, b):
    ah, al = _split2(a)
    bh, bl = _split2(b)
    return _dot(ah, bh) + (_dot(ah, bl) + _dot(al, bh))


def _dot3_nt(a, b):
    ah, al = _split2(a)
    bh, bl = _split2(b)
    return _dot_nt(ah, bh) + (_dot_nt(ah, bl) + _dot_nt(al, bh))


def _bdot(a, b):
    return lax.dot_general(a, b, (((2,), (1,)), ((0,), (0,))), preferred_element_type=F32)


def _bdot_nt(a, b):
    return lax.dot_general(a, b, (((2,), (2,)), ((0,), (0,))), preferred_element_type=F32)


def _bdot_tn(a, b):
    return lax.dot_general(a, b, (((1,), (1,)), ((0,), (0,))), preferred_element_type=F32)


def _bdot3(a, b):
    ah, al = _split2(a)
    bh, bl = _split2(b)
    return _bdot(ah, bh) + (_bdot(ah, bl) + _bdot(al, bh))


def _bdot3_nt(a, b):
    ah, al = _split2(a)
    bh, bl = _split2(b)
    return _bdot_nt(ah, bh) + (_bdot_nt(ah, bl) + _bdot_nt(al, bh))


def _exact_left(mask_bf16, x):
    hi, mid, lo = _split3(x)
    return _dot(mask_bf16, hi) + (_dot(mask_bf16, mid) + _dot(mask_bf16, lo))


def _exact_right(x, mask_bf16):
    hi, mid, lo = _split3(x)
    return _dot(hi, mask_bf16) + (_dot(mid, mask_bf16) + _dot(lo, mask_bf16))


def _softplus(x):
    return jnp.maximum(x, 0.0) + jnp.log1p(jnp.exp(-jnp.abs(x)))


def _log_sigmoid(x):
    return -_softplus(-x)


def _sigmoid(x):
    return 1.0 / (1.0 + jnp.exp(-x))


def _layer_norm(x, g, b):
    mu = jnp.mean(x, axis=-1, keepdims=True)
    xc = x - mu
    var = jnp.mean(xc * xc, axis=-1, keepdims=True)
    return xc * lax.rsqrt(var + LN_EPS) * g + b


def _iota2(shape, dim):
    return lax.broadcasted_iota(I32, shape, dim)


def _ln_kernel(x_ref, g_ref, b_ref, o_ref):
    o_ref[...] = _layer_norm(x_ref[...], g_ref[...], b_ref[...])


def _ln_call(x, g, b):
    T, D = x.shape
    tm = ROW_BLOCK
    return pl.pallas_call(
        _ln_kernel,
        grid=(T // tm,),
        in_specs=[pl.BlockSpec((tm, D), lambda i: (i, 0)),
                  pl.BlockSpec((1, D), lambda i: (0, 0)),
                  pl.BlockSpec((1, D), lambda i: (0, 0))],
        out_specs=pl.BlockSpec((tm, D), lambda i: (i, 0)),
        out_shape=jax.ShapeDtypeStruct((T, D), F32),
        compiler_params=_cparams(("parallel",)),
        name="embed_ln",
    )(x, g.reshape(1, D), b.reshape(1, D))


def _inproj_kernel(h_ref, w_ref, wgh_ref, wgl_ref, wgth_ref, wgtl_ref, u_ref, g_ref, gt_ref):
    h = h_ref[...]
    hh, hl = _split2(h)
    u_ref[...] = _dot(hh, w_ref[...])
    wgh, wgl = wgh_ref[...], wgl_ref[...]
    g_ref[...] = _dot(hh, wgh) + (_dot(hh, wgl) + _dot(hl, wgh))
    wgth, wgtl = wgth_ref[...], wgtl_ref[...]
    gt_ref[...] = _dot_nt(wgth, hh) + (_dot_nt(wgtl, hh) + _dot_nt(wgth, hl))


def _inproj_call(h, w_main, wg):
    T, D = h.shape
    N = w_main.shape[1]
    tm = ROW_BLOCK
    wgh, wgl = _split2(wg)
    wgt = wg.T
    wgth, wgtl = _split2(wgt)
    G = N_GATE_COLS
    const = lambda i: (0, 0)
    return pl.pallas_call(
        _inproj_kernel,
        grid=(T // tm,),
        in_specs=[pl.BlockSpec((tm, D), lambda i: (i, 0)),
                  pl.BlockSpec((D, N), const),
                  pl.BlockSpec((D, G), const), pl.BlockSpec((D, G), const),
                  pl.BlockSpec((G, D), const), pl.BlockSpec((G, D), const)],
        out_specs=[pl.BlockSpec((tm, N), lambda i: (i, 0)),
                   pl.BlockSpec((tm, G), lambda i: (i, 0)),
                   pl.BlockSpec((G, tm), lambda i: (0, i))],
        out_shape=[jax.ShapeDtypeStruct((T, N), F32),
                   jax.ShapeDtypeStruct((T, G), F32),
                   jax.ShapeDtypeStruct((G, T), F32)],
        compiler_params=_cparams(("parallel",)),
        name="in_proj",
    )(h, w_main, wgh, wgl, wgth, wgtl)


def _mlstm_kernel(q_ref, k_ref, v_ref, o_ref, g_ref, gt_ref, brow_ref, bcol_ref, nw_ref,
                  y_ref, c_sc, m_sc):
    L = q_ref.shape[0]
    d = M_HEAD_DIM

    @pl.when(pl.program_id(1) == 0)
    def _():
        c_sc[...] = jnp.zeros_like(c_sc)
        m_sc[...] = jnp.full_like(m_sc, -jnp.inf)

    row = _iota2((L, L), 0)
    col = _iota2((L, L), 1)
    causal = col <= row
    tril = causal.astype(BF16)
    triu = (row <= col).astype(BF16)

    gb = g_ref[...] + brow_ref[...]
    gtb = gt_ref[...] + bcol_ref[...]
    b_cols = _exact_left(tril, _log_sigmoid(gb))
    b_rows = _exact_right(_log_sigmoid(gtb), triu)

    ones_col = (_iota2((L, d), 1) == 0).astype(F32)
    H = range(M_HEADS)
    sl = [slice(hd * d, (hd + 1) * d) for hd in H]
    q = [q_ref[:, c].astype(BF16) for c in sl]
    kf = [k_ref[:, c] * (d ** -0.5) for c in sl]
    k = [x.astype(BF16) for x in kf]
    v_aug = [jnp.concatenate([v_ref[:, c], ones_col], axis=1).astype(BF16) for c in sl]
    bc = [b_cols[:, M_HEADS + hd:M_HEADS + hd + 1] for hd in H]
    lic = [gb[:, hd:hd + 1] for hd in H]
    src = [gtb[hd:hd + 1, :] - b_rows[M_HEADS + hd:M_HEADS + hd + 1, :] for hd in H]
    b_end = [x[L - 1:L, :] for x in bc]
    m_prev = [m_sc[hd] for hd in H]
    c_aug = [c_sc[hd] for hd in H]

    qk = [_dot_nt(q[h], k[h]) for h in H]
    qc = [_dot(q[h], c_aug[h].astype(BF16)) for h in H]
    log_w = [jnp.where(causal, bc[h] + src[h], -jnp.inf) for h in H]
    m_intra = [jnp.max(x, axis=1, keepdims=True) for x in log_w]
    carry_log = [bc[h] + m_prev[h] for h in H]
    m_t = [jnp.maximum(carry_log[h], m_intra[h]) for h in H]
    inter = [jnp.exp(carry_log[h] - m_t[h]) for h in H]
    s = [(qk[h] * jnp.exp(log_w[h] - m_t[h])).astype(BF16) for h in H]
    num_aug = [inter[h] * qc[h] + _dot(s[h], v_aug[h]) for h in H]
    hh = [num_aug[h][:, :d] / jnp.maximum(jnp.abs(num_aug[h][:, d:d + 1]), jnp.exp(-m_t[h]))
          for h in H]

    m_end = [jnp.max(b_end[h] + src[h], axis=1, keepdims=True) for h in H]
    m_new = [jnp.maximum(b_end[h] + m_prev[h], m_end[h]) for h in H]
    wk = [(jnp.exp(b_end[h] - bc[h] + lic[h] - m_new[h]) * kf[h]).astype(BF16) for h in H]
    kv = [_dot_tn(wk[h], v_aug[h]) for h in H]
    for h in H:
        c_sc[h] = jnp.exp(b_end[h] + m_prev[h] - m_new[h]) * c_aug[h] + kv[h]
        m_sc[h] = m_new[h]

    outs = [x * lax.rsqrt(jnp.mean(x * x, axis=1, keepdims=True) + NORM_EPS) for x in hh]
    hcat = jnp.concatenate(outs, axis=1) * nw_ref[...]
    y_ref[...] = _sigmoid(o_ref[...]) * hcat


def _mlstm_call(u, g, gt, i_bias, f_bias, norm_w, B, S):
    T = u.shape[0]
    L = min(M_CHUNK, S)
    nc = S // L
    W = M_WIDTH
    zeros = jnp.zeros((N_GATE_COLS - 2 * M_HEADS,), F32)
    bias = jnp.concatenate([i_bias, f_bias, zeros])
    G = N_GATE_COLS
    rowblk = lambda c: pl.BlockSpec((L, W), lambda b, j, c=c: (b * nc + j, c))
    const = lambda b, j: (0, 0)
    return pl.pallas_call(
        _mlstm_kernel,
        grid=(B, nc),
        in_specs=[rowblk(0), rowblk(1), rowblk(2), rowblk(3),
                  pl.BlockSpec((L, G), lambda b, j: (b * nc + j, 0)),
                  pl.BlockSpec((G, L), lambda b, j: (0, b * nc + j)),
                  pl.BlockSpec((1, G), const), pl.BlockSpec((G, 1), const),
                  pl.BlockSpec((1, W), const)],
        out_specs=pl.BlockSpec((L, W), lambda b, j: (b * nc + j, 0)),
        out_shape=jax.ShapeDtypeStruct((T, W), F32),
        scratch_shapes=[pltpu.VMEM((M_HEADS, M_HEAD_DIM, 2 * M_HEAD_DIM), F32),
                        pltpu.VMEM((M_HEADS, 1, 1), F32)],
        compiler_params=_cparams(("parallel", "arbitrary")),
        name="mlstm",
    )(u, u, u, u, g, gt, bias.reshape(1, G), bias.reshape(G, 1), norm_w.reshape(1, W))


def _sb_kernel(q_ref, k_ref, v_ref, nw_ref, y_ref, qb_sc, acc_sc, r_sc):
    tq = q_ref.shape[0]
    tk = tq
    d = SB_HEAD_DIM
    qi = pl.program_id(1)
    srow = _iota2((tq, tk), 0)
    scol = _iota2((tq, tk), 1)
    later = (scol < srow).astype(BF16)

    qb_sc[...] = (q_ref[...] * (d ** -0.5)).astype(BF16)
    acc_sc[...] = jnp.zeros_like(acc_sc)
    r_sc[...] = jnp.zeros_like(r_sc)

    def block(kb, diagonal):
        off = pl.multiple_of(kb * tk, tk)
        heads = range(SB_HEADS)
        cols = [slice(hd * d, (hd + 1) * d) for hd in heads]
        valid = scol < srow
        z = [_dot_nt(qb_sc[:, c], k_ref[pl.ds(off, tk), c].astype(BF16)) for c in cols]
        sp = [jnp.maximum(x, 0.0) + jnp.log(1.0 + jnp.exp(-jnp.abs(x))) for x in z]
        if diagonal:
            sp = [jnp.where(valid, x, 0.0) for x in sp]
        c2 = [_dot(jnp.concatenate(_split2(x), axis=0), later) for x in sp]
        cs = [x[:tq] + x[tq:] for x in c2]
        a = [jnp.exp(z[h] - sp[h] - cs[h]) for h in heads]
        if diagonal:
            a = [jnp.where(valid, x, 0.0) for x in a]
        pv = [_dot(a[h].astype(BF16), v_ref[pl.ds(off, tk), cols[h]].astype(BF16)) for h in heads]
        for h in heads:
            rest = r_sc[h]
            acc_sc[h] = acc_sc[h] + jnp.exp(-rest) * pv[h]
            r_sc[h] = rest + (cs[h][:, 0:1] + sp[h][:, 0:1])

    block(qi, True)

    def body(it, carry):
        block(qi - it, False)
        return carry

    lax.fori_loop(1, qi + 1, body, 0)
    outs = []
    for hd in range(SB_HEADS):
        o = acc_sc[hd]
        outs.append(o * lax.rsqrt(jnp.mean(o * o, axis=1, keepdims=True) + NORM_EPS))
    y_ref[...] = jnp.concatenate(outs, axis=1) * nw_ref[...]


def _sb_call(u, norm_w, B, S, col0):
    T = u.shape[0]
    tq = min(SB_BLOCK, S)
    nq = S // tq
    W = SB_WIDTH
    return pl.pallas_call(
        _sb_kernel,
        grid=(B, nq),
        in_specs=[pl.BlockSpec((tq, W), lambda b, i: (b * nq + i, col0)),
                  pl.BlockSpec((S, W), lambda b, i: (b, col0 + 1)),
                  pl.BlockSpec((S, W), lambda b, i: (b, col0 + 2)),
                  pl.BlockSpec((1, W), lambda b, i: (0, 0))],
        out_specs=pl.BlockSpec((tq, W), lambda b, i: (b * nq + i, 0)),
        out_shape=jax.ShapeDtypeStruct((T, W), F32),
        scratch_shapes=[pltpu.VMEM((tq, W), BF16),
                        pltpu.VMEM((SB_HEADS, tq, SB_HEAD_DIM), F32),
                        pltpu.VMEM((SB_HEADS, tq, 1), F32)],
        compiler_params=_cparams(("parallel", "arbitrary")),
        name="stickbreak",
    )(u, u, u, norm_w.reshape(1, W))


def _gdn_kernel(q_ref, k_ref, v_ref, z_ref, g_ref, gt_ref, cw_ref, arow_ref, acol_ref,
                drow_ref, dcol_ref, nw_ref, y_ref, xq_sc, xk_sc, xv_sc, s_sc):
    LB = q_ref.shape[0]
    C = G_CHUNK
    d = G_HEAD_DIM
    W = G_WIDTH
    HALO = 8

    @pl.when(pl.program_id(1) == 0)
    def _():
        s_sc[...] = jnp.zeros_like(s_sc)
        for sc in (xq_sc, xk_sc, xv_sc):
            sc[0:HALO, :] = jnp.zeros((HALO, W), F32)

    @pl.when(pl.program_id(1) != 0)
    def _():
        for sc in (xq_sc, xk_sc, xv_sc):
            sc[0:HALO, :] = sc[LB:LB + HALO, :]

    def conv_silu(x_ref, sc, j):
        sc[HALO:HALO + LB, :] = x_ref[...]
        acc = None
        for t in range(G_CONV):
            w = cw_ref[t:t + 1, j * W:(j + 1) * W]
            term = w * sc[HALO - (G_CONV - 1) + t:HALO - (G_CONV - 1) + t + LB, :]
            acc = term if acc is None else acc + term
        return acc * _sigmoid(acc)

    qc = conv_silu(q_ref, xq_sc, 0)
    kc = conv_silu(k_ref, xk_sc, 1)
    vc = conv_silu(v_ref, xv_sc, 2)

    gb = g_ref[...]
    gtb = gt_ref[...]
    gdec_cols = -jnp.exp(arow_ref[...]) * _softplus(gb + drow_ref[...])
    gdec_rows = -jnp.exp(acol_ref[...]) * _softplus(gtb + dcol_ref[...])
    beta_cols = _sigmoid(gb)
    row = _iota2((LB, LB), 0)
    col = _iota2((LB, LB), 1)
    shift = C.bit_length() - 1
    same = jnp.right_shift(row, shift) == jnp.right_shift(col, shift)
    tril = (same & (col <= row)).astype(BF16)
    triu = (same & (row <= col)).astype(BF16)
    gam_cols = _exact_left(tril, gdec_cols)
    gam_rows = _exact_right(gdec_rows, triu)

    nc = LB // C
    NB = G_HEADS * nc

    def per_head(x):
        return jnp.stack([x[:, h * d:(h + 1) * d] for h in range(G_HEADS)], axis=0).reshape(NB, C, d)

    q3 = per_head(qc)
    k3 = per_head(kc)
    v3 = per_head(vc)
    q3 = q3 * lax.rsqrt(jnp.sum(q3 * q3, axis=2, keepdims=True) + NORM_EPS) * (d ** -0.5)
    k3 = k3 * lax.rsqrt(jnp.sum(k3 * k3, axis=2, keepdims=True) + NORM_EPS)
    ga0 = 2 * G_HEADS
    gb0 = 3 * G_HEADS
    gam_c = jnp.stack([gam_cols[:, ga0 + h:ga0 + h + 1] for h in range(G_HEADS)], 0).reshape(NB, C, 1)
    beta = jnp.stack([beta_cols[:, gb0 + h:gb0 + h + 1] for h in range(G_HEADS)], 0).reshape(NB, C, 1)
    gam_r = jnp.stack([gam_rows[ga0 + h:ga0 + h + 1, c * C:(c + 1) * C]
                       for h in range(G_HEADS) for c in range(nc)], 0)
    gam_end = gam_c[:, C - 1:C, :]

    r64 = _iota2((1, C, C), 1)
    c64 = _iota2((1, C, C), 2)
    incl = c64 <= r64
    strict = c64 < r64

    decay = jnp.exp(jnp.where(incl, gam_c - gam_r, -jnp.inf))
    kk = _bdot3_nt(k3, k3)
    m = jnp.where(strict, beta * kk * decay, 0.0)
    tinv = jnp.broadcast_to((r64 == c64).astype(F32), (NB, C, C))
    s = 1
    while s < C:
        sh = s.bit_length() - 1
        same2s = jnp.right_shift(r64, sh + 1) == jnp.right_shift(c64, sh + 1)
        low_mask = same2s & ((jnp.right_shift(r64, sh) & 1) == 1) & ((jnp.right_shift(c64, sh) & 1) == 0)
        low = jnp.where(low_mask, m, 0.0).astype(BF16)
        tb = tinv.astype(BF16)
        tinv = tinv - _bdot(_bdot(tb, low).astype(BF16), tb)
        s *= 2
    eg = jnp.exp(gam_c)
    tb = tinv.astype(BF16)
    u = _bdot(tb, (v3 * beta).astype(BF16))
    w = _bdot(tb, (k3 * (beta * eg)).astype(BF16)).astype(BF16)
    qk = (_bdot_nt(q3.astype(BF16), k3.astype(BF16)) * decay).astype(BF16)
    q_dec = (q3 * eg).astype(BF16)
    k_dec = (k3 * jnp.exp(gam_end - gam_c)).astype(BF16)
    cdec = jnp.exp(gam_end)

    def chunk(x, c):
        return x.reshape((G_HEADS, nc) + x.shape[1:])[:, c]

    state = s_sc[...]
    o_chunks = []
    for c in range(nc):
        sb = state.astype(BF16)
        v_new = chunk(u, c) - _bdot(chunk(w, c), sb)
        vb = v_new.astype(BF16)
        o_chunks.append(_bdot(chunk(q_dec, c), sb) + _bdot(chunk(qk, c), vb))
        state = chunk(cdec, c) * state + _bdot_tn(chunk(k_dec, c), vb)
    s_sc[...] = state

    nw = nw_ref[...]
    o = jnp.concatenate(o_chunks, axis=1)
    o = o * lax.rsqrt(jnp.mean(o * o, axis=2, keepdims=True) + NORM_EPS) * nw
    zz = z_ref[...]
    y_ref[...] = jnp.concatenate([o[h] for h in range(G_HEADS)], axis=1) * (zz * _sigmoid(zz))


def _gdn_call(u, g, gt, conv_w, a_log, dt_bias, norm_w, B, S, col0):
    T = u.shape[0]
    LB = min(G_BLOCK, S)
    nb = S // LB
    W = G_WIDTH
    G = N_GATE_COLS
    pad = jnp.zeros((2 * G_HEADS,), F32)
    arow = jnp.concatenate([pad, a_log, jnp.zeros((G_HEADS,), F32)])
    drow = jnp.concatenate([pad, dt_bias, jnp.zeros((G_HEADS,), F32)])
    rowblk = lambda c: pl.BlockSpec((LB, W), lambda b, j, c=c: (b * nb + j, col0 + c))
    const = lambda b, j: (0, 0)
    return pl.pallas_call(
        _gdn_kernel,
        grid=(B, nb),
        in_specs=[rowblk(0), rowblk(1), rowblk(2), rowblk(3),
                  pl.BlockSpec((LB, G), lambda b, j: (b * nb + j, 0)),
                  pl.BlockSpec((G, LB), lambda b, j: (0, b * nb + j)),
                  pl.BlockSpec((G_CONV, 3 * W), const),
                  pl.BlockSpec((1, G), const), pl.BlockSpec((G, 1), const),
                  pl.BlockSpec((1, G), const), pl.BlockSpec((G, 1), const),
                  pl.BlockSpec((1, G_HEAD_DIM), const)],
        out_specs=pl.BlockSpec((LB, W), lambda b, j: (b * nb + j, 0)),
        out_shape=jax.ShapeDtypeStruct((T, W), F32),
        scratch_shapes=[pltpu.VMEM((LB + 8, W), F32), pltpu.VMEM((LB + 8, W), F32),
                        pltpu.VMEM((LB + 8, W), F32),
                        pltpu.VMEM((G_HEADS, G_HEAD_DIM, G_HEAD_DIM), F32)],
        compiler_params=_cparams(("parallel", "arbitrary")),
        name="gdn",
    )(u, u, u, u, g, gt, conv_w, arow.reshape(1, G), arow.reshape(G, 1),
      drow.reshape(1, G), drow.reshape(G, 1), norm_w.reshape(1, G_HEAD_DIM))


def _post_kernel(alpha, h_ref, ym_ref, ys_ref, yg_ref, wo_ref,
                 wrh_ref, wrl_ref, br_ref, g1_ref, b1_ref,
                 h1_ref, idx_ref, gate_ref, rank_ref, cnt_ref, cnt_sc):
    tm = h_ref.shape[0]
    E = br_ref.shape[1]

    @pl.when(pl.program_id(0) == 0)
    def _():
        cnt_sc[...] = jnp.zeros_like(cnt_sc)

    y = jnp.concatenate([ym_ref[...], ys_ref[...], yg_ref[...]], axis=1).astype(BF16)
    a = alpha * h_ref[...] + _dot(y, wo_ref[...])
    h1 = _layer_norm(a, g1_ref[...], b1_ref[...])
    h1h, h1l = _split2(h1)
    h1_ref[...] = h1

    wrh, wrl = wrh_ref[...], wrl_ref[...]
    logits = _dot(h1h, wrh) + (_dot(h1h, wrl) + _dot(h1l, wrh)) + br_ref[...]
    lane = _iota2((tm, E), 1).astype(F32)
    work = logits
    vals, idxs = [], []
    anyhot = jnp.zeros((tm, E), F32)
    for _ in range(TOP_K):
        mx = jnp.max(work, axis=1, keepdims=True)
        ix = jnp.min(jnp.where(work == mx, lane, float(E)), axis=1, keepdims=True)
        sel = lane == ix
        vals.append(mx)
        idxs.append(ix)
        anyhot = jnp.where(sel, 1.0, anyhot)
        work = jnp.where(sel, -jnp.inf, work)
    ex = [jnp.exp(v - vals[0]) for v in vals]
    tot = ex[0] + ex[1] + ex[2] + ex[3]
    gates = [e / tot for e in ex]

    r = _iota2((tm, tm), 0)
    c = _iota2((tm, tm), 1)
    before = (c < r).astype(BF16)
    pos = cnt_sc[...] + _dot(before, anyhot.astype(BF16))
    ranks = [jnp.sum(jnp.where(lane == ix, pos, 0.0), axis=1, keepdims=True) for ix in idxs]
    cnt_sc[...] = cnt_sc[...] + jnp.sum(anyhot, axis=0, keepdims=True)
    cnt_ref[...] = cnt_sc[...].astype(I32)

    kl = _iota2((tm, TOP_K), 1)

    def pack(cols):
        out = jnp.broadcast_to(cols[0], (tm, TOP_K))
        for k in range(1, TOP_K):
            out = jnp.where(kl == k, cols[k], out)
        return out

    idx_ref[...] = pack(idxs).astype(I32)
    gate_ref[...] = pack(gates)
    rank_ref[...] = pack(ranks).astype(I32)


def _post_call(alpha, h, ym, ys, yg, w_out, w_router, b_router, ln_g, ln_b):
    T, D = h.shape
    tm = POST_BLOCK
    E = w_router.shape[1]
    wrh, wrl = _split2(w_router)
    const = lambda i: (0, 0)
    rows = lambda w: pl.BlockSpec((tm, w), lambda i: (i, 0))
    return pl.pallas_call(
        functools.partial(_post_kernel, alpha),
        grid=(T // tm,),
        in_specs=[rows(D), rows(M_WIDTH), rows(SB_WIDTH), rows(G_WIDTH),
                  pl.BlockSpec(w_out.shape, const),
                  pl.BlockSpec((D, E), const), pl.BlockSpec((D, E), const),
                  pl.BlockSpec((1, E), const), pl.BlockSpec((1, D), const),
                  pl.BlockSpec((1, D), const)],
        out_specs=[rows(D), rows(TOP_K), rows(TOP_K), rows(TOP_K),
                   pl.BlockSpec((1, E), const)],
        out_shape=[jax.ShapeDtypeStruct((T, D), F32),
                   jax.ShapeDtypeStruct((T, TOP_K), I32), jax.ShapeDtypeStruct((T, TOP_K), F32),
                   jax.ShapeDtypeStruct((T, TOP_K), I32), jax.ShapeDtypeStruct((1, E), I32)],
        scratch_shapes=[pltpu.VMEM((1, E), F32)],
        compiler_params=_cparams(("arbitrary",)),
        name="post_mix",
    )(h, ym, ys, yg, w_out, wrh, wrl, b_router.reshape(1, E),
      ln_g.reshape(1, D), ln_b.reshape(1, D))


def _row_copy(src, s, dst, d, sem):
    return pltpu.make_async_copy(src.at[pl.ds(s, 1), :], dst.at[pl.ds(d, 1), :], sem)


def _dispatch_kernel(alpha, dest_ref, h1_ref, p_ref, wpg_ref, wpp_ref, xs_in_ref,
                     base_ref, xs_ref, xpk_sc, sem):
    del xs_in_ref
    tm, D = h1_ref.shape
    half = D // 2
    h1 = h1_ref[...]
    hb = h1.astype(BF16)
    bits = lax.bitcast_convert_type(hb.astype(F32), jnp.uint32)
    xpk_sc[...] = (bits[:, :half] >> 16) | (bits[:, half:] & jnp.uint32(0xFFFF0000))

    def issue(r, carry):
        for k in range(TOP_K):
            _row_copy(xpk_sc, r, xs_ref, dest_ref[r * TOP_K + k], sem).start()
        return carry

    lax.fori_loop(0, tm, issue, 0, unroll=DMA_UNROLL)
    ple = _sigmoid(_dot(hb, wpg_ref[...])) * _dot(p_ref[...].astype(BF16), wpp_ref[...])
    base_ref[...] = alpha * h1 + ple
    for k in range(TOP_K):
        pltpu.make_async_copy(xpk_sc, xs_ref.at[pl.ds(0, tm), :], sem).wait()


def _dispatch_call(alpha, dest_flat, h1, p, w_pg, w_pp, n_slots, xs0):
    T, D = h1.shape
    tm = ROW_BLOCK
    P = p.shape[1]
    const = lambda i: (0, 0)
    rows = lambda w: pl.BlockSpec((tm, w), lambda i: (i, 0))
    assert xs0.shape == (n_slots, D // 2)
    base, xs = pl.pallas_call(
        functools.partial(_dispatch_kernel, alpha),
        grid=(T // tm,),
        in_specs=[pl.BlockSpec((tm * TOP_K,), lambda i: (i,), memory_space=pltpu.SMEM),
                  rows(D), rows(P), pl.BlockSpec((D, D), const), pl.BlockSpec((P, D), const),
                  pl.BlockSpec(memory_space=pl.ANY)],
        out_specs=[rows(D), pl.BlockSpec(memory_space=pl.ANY)],
        out_shape=[jax.ShapeDtypeStruct((T, D), F32),
                   jax.ShapeDtypeStruct((n_slots, D // 2), jnp.uint32)],
        scratch_shapes=[pltpu.VMEM((tm, D // 2), jnp.uint32), pltpu.SemaphoreType.DMA],
        input_output_aliases={5: 1},
        compiler_params=_cparams(("arbitrary",)),
        name="ple_dispatch",
    )(dest_flat, h1, p, w_pg, w_pp, xs0)
    return base, xs


def _moe_kernel(be_ref, nb_ref, x_ref, wgu_ref, bgu_ref, wd_ref, bd_ref, y_ref, wgu_sc, wd_sc):
    i = pl.program_id(0)
    F = wd_ref.shape[2]
    CH = 128

    new_expert = jnp.logical_or(i == 0, be_ref[i] != be_ref[jnp.maximum(i - 1, 0)])

    @pl.when(jnp.logical_and(new_expert, i < nb_ref[0]))
    def _():
        def cast_gu(r, c):
            rows = pl.ds(pl.multiple_of(r * CH, CH), CH)
            wgu_sc[rows, :] = wgu_ref[0, 0, rows, :].astype(BF16)
            return c

        def cast_d(r, c):
            rows = pl.ds(pl.multiple_of(r * CH, CH), CH)
            wd_sc[rows, :] = wd_ref[0, 0, rows, :].astype(BF16)
            return c

        lax.fori_loop(0, wgu_ref.shape[2] // CH, cast_gu, 0)
        lax.fori_loop(0, F // CH, cast_d, 0)

    @pl.when(i < nb_ref[0])
    def _():
        xw = x_ref[...]
        lo = lax.bitcast_convert_type(xw << 16, F32)
        hi = lax.bitcast_convert_type(xw & jnp.uint32(0xFFFF0000), F32)
        xb = jnp.concatenate([lo, hi], axis=1).astype(BF16)
        h = _dot(xb, wgu_sc[...]) + bgu_ref[0, 0]
        gate = jnp.minimum(h[:, :F], SWIGLU_LIMIT)
        up = jnp.clip(h[:, F:], -SWIGLU_LIMIT, SWIGLU_LIMIT)
        act = (up + 1.0) * gate * _sigmoid(SWIGLU_ALPHA * gate)
        y_ref[...] = _dot(act.astype(BF16), wd_sc[...]) + bd_ref[0, 0]

    @pl.when(i >= nb_ref[0])
    def _():
        y_ref[...] = jnp.zeros_like(y_ref)


def _moe_call(layer, block_e, n_used, xs, w_gu, b_gu, w_down, b_down):
    P = xs.shape[0]
    L, E, D, F2 = w_gu.shape
    F = F2 // 2
    tm = MOE_BLOCK
    nblk = P // tm
    grid_spec = pltpu.PrefetchScalarGridSpec(
        num_scalar_prefetch=2,
        grid=(nblk,),
        in_specs=[pl.BlockSpec((tm, D // 2), lambda i, be, nb: (i, 0)),
                  pl.BlockSpec((1, 1, D, F2), lambda i, be, nb: (layer, be[i], 0, 0)),
                  pl.BlockSpec((1, 1, 1, F2), lambda i, be, nb: (layer, be[i], 0, 0)),
                  pl.BlockSpec((1, 1, F, D), lambda i, be, nb: (layer, be[i], 0, 0)),
                  pl.BlockSpec((1, 1, 1, D), lambda i, be, nb: (layer, be[i], 0, 0))],
        out_specs=pl.BlockSpec((tm, D), lambda i, be, nb: (i, 0)),
        scratch_shapes=[pltpu.VMEM((D, F2), BF16), pltpu.VMEM((F, D), BF16)],
    )
    return pl.pallas_call(
        _moe_kernel,
        grid_spec=grid_spec,
        out_shape=jax.ShapeDtypeStruct((P, D), F32),
        compiler_params=_cparams(("arbitrary",)),
        name="moe_experts",
    )(block_e, n_used, xs, w_gu, b_gu.reshape(L, E, 1, F2), w_down, b_down.reshape(L, E, 1, D))


def _combine_kernel(dest_ref, dnext_ref, base_ref, gate_ref, y_ref, g_ref, b_ref, o_ref, ybuf, sems):
    tm = base_ref.shape[0]
    i = pl.program_id(0)
    n = pl.num_programs(0)
    slot = i % 2

    def gather(idx_ref, sl):
        def issue(r, carry):
            for k in range(TOP_K):
                pltpu.make_async_copy(y_ref.at[pl.ds(idx_ref[r * TOP_K + k], 1), :],
                                      ybuf.at[sl, k, pl.ds(r, 1), :], sems.at[sl]).start()
            return carry
        lax.fori_loop(0, tm, issue, 0, unroll=DMA_UNROLL)

    @pl.when(i == 0)
    def _():
        gather(dest_ref, 0)

    @pl.when(i + 1 < n)
    def _():
        gather(dnext_ref, 1 - slot)

    for k in range(TOP_K):
        pltpu.make_async_copy(y_ref.at[pl.ds(0, tm), :], ybuf.at[slot, k], sems.at[slot]).wait()
    gate = gate_ref[...]
    acc = base_ref[...]
    for k in range(TOP_K):
        acc = acc + gate[:, k:k + 1] * ybuf[slot, k]
    o_ref[...] = _layer_norm(acc, g_ref[...], b_ref[...])


def _combine_call(dest_flat, base, gates, y, ln_g, ln_b):
    T, D = base.shape
    tm = ROW_BLOCK
    rows = lambda w: pl.BlockSpec((tm, w), lambda i: (i, 0))
    const = lambda i: (0, 0)
    nblk = T // tm
    return pl.pallas_call(
        _combine_kernel,
        grid=(nblk,),
        in_specs=[pl.BlockSpec((tm * TOP_K,), lambda i: (i,), memory_space=pltpu.SMEM),
                  pl.BlockSpec((tm * TOP_K,), lambda i: (jnp.minimum(i + 1, nblk - 1),),
                               memory_space=pltpu.SMEM),
                  rows(D), rows(TOP_K), pl.BlockSpec(memory_space=pl.ANY),
                  pl.BlockSpec((1, D), const), pl.BlockSpec((1, D), const)],
        out_specs=rows(D),
        out_shape=jax.ShapeDtypeStruct((T, D), F32),
        scratch_shapes=[pltpu.VMEM((2, TOP_K, tm, D), F32), pltpu.SemaphoreType.DMA((2,))],
        compiler_params=_cparams(("arbitrary",)),
        name="combine_ln",
    )(dest_flat, dest_flat, base, gates, y, ln_g.reshape(1, D), ln_b.reshape(1, D))


def _combine_inproj_kernel(dest_ref, dnext_ref, base_ref, gate_ref, y_ref, g_ref, b_ref,
                           w_ref, wgh_ref, wgl_ref, wgth_ref, wgtl_ref,
                           h_ref, u_ref, go_ref, gto_ref, ybuf, sems):
    tm = base_ref.shape[0]
    N = w_ref.shape[1]
    i = pl.program_id(0)
    n = pl.num_programs(0)
    slot = i % 2
    other = 1 - slot

    def row_copy(idx_ref, r, k, sl):
        return pltpu.make_async_copy(y_ref.at[pl.ds(idx_ref[r * TOP_K + k], 1), :],
                                     ybuf.at[sl, k, pl.ds(r, 1), :], sems.at[sl])

    @pl.when(i == 0)
    def _():
        def issue(r, carry):
            for k in range(TOP_K):
                row_copy(dest_ref, r, k, 0).start()
            return carry
        lax.fori_loop(0, tm, issue, 0, unroll=DMA_UNROLL)

    def wait_slot(sl):
        for k in range(TOP_K):
            pltpu.make_async_copy(y_ref.at[pl.ds(0, tm), :], ybuf.at[sl, k], sems.at[sl]).wait()

    wait_slot(slot)
    gate = gate_ref[...]
    acc = base_ref[...]
    for k in range(TOP_K):
        acc = acc + gate[:, k:k + 1] * ybuf[slot, k]
    h = _layer_norm(acc, g_ref[...], b_ref[...])
    h_ref[...] = h
    hh, hl = _split2(h)

    nch = N // IN_CHUNK
    rows_per = -(-tm // nch)
    for c in range(nch):
        cs = slice(c * IN_CHUNK, (c + 1) * IN_CHUNK)
        u_ref[:, cs] = _dot(hh, w_ref[:, cs])
        for r in range(c * rows_per, min((c + 1) * rows_per, tm)):
            for k in range(TOP_K):
                row_copy(dnext_ref, r, k, other).start()
    wgh, wgl = wgh_ref[...], wgl_ref[...]
    go_ref[...] = _dot(hh, wgh) + (_dot(hh, wgl) + _dot(hl, wgh))
    wgth, wgtl = wgth_ref[...], wgtl_ref[...]
    gto_ref[...] = _dot_nt(wgth, hh) + (_dot_nt(wgtl, hh) + _dot_nt(wgth, hl))

    @pl.when(i == n - 1)
    def _():
        wait_slot(other)


def _combine_inproj_call(dest_flat, base, gates, y, ln_g, ln_b, w_main, wg):
    T, D = base.shape
    N = w_main.shape[1]
    tm = ROW_BLOCK
    G = N_GATE_COLS
    wgh, wgl = _split2(wg)
    wgth, wgtl = _split2(wg.T)
    rows = lambda w: pl.BlockSpec((tm, w), lambda i: (i, 0))
    const = lambda i: (0, 0)
    nblk = T // tm
    return pl.pallas_call(
        _combine_inproj_kernel,
        grid=(nblk,),
        in_specs=[pl.BlockSpec((tm * TOP_K,), lambda i: (i,), memory_space=pltpu.SMEM),
                  pl.BlockSpec((tm * TOP_K,), lambda i: (jnp.minimum(i + 1, nblk - 1),),
                               memory_space=pltpu.SMEM),
                  rows(D), rows(TOP_K), pl.BlockSpec(memory_space=pl.ANY),
                  pl.BlockSpec((1, D), const), pl.BlockSpec((1, D), const),
                  pl.BlockSpec((D, N), const),
                  pl.BlockSpec((D, G), const), pl.BlockSpec((D, G), const),
                  pl.BlockSpec((G, D), const), pl.BlockSpec((G, D), const)],
        out_specs=[rows(D), rows(N), rows(G), pl.BlockSpec((G, tm), lambda i: (0, i))],
        out_shape=[jax.ShapeDtypeStruct((T, D), F32), jax.ShapeDtypeStruct((T, N), F32),
                   jax.ShapeDtypeStruct((T, G), F32), jax.ShapeDtypeStruct((G, T), F32)],
        scratch_shapes=[pltpu.VMEM((2, TOP_K, tm, D), F32), pltpu.SemaphoreType.DMA((2,))],
        compiler_params=_cparams(("arbitrary",)),
        name="combine_in_proj",
    )(dest_flat, dest_flat, base, gates, y, ln_g.reshape(1, D), ln_b.reshape(1, D),
      w_main, wgh, wgl, wgth, wgtl)


def _moe_ffn(alpha, layer, h1, p_i, idx, rank, counts, w_pg, w_pp, w_gu, b_gu, w_down, b_down,
             xs_buf):
    T, D = h1.shape
    E = w_gu.shape[1]
    A = T * TOP_K
    counts = counts.reshape(E)
    padded = (counts + MOE_BLOCK - 1) // MOE_BLOCK * MOE_BLOCK
    pad_end = jnp.cumsum(padded)
    pad_start = pad_end - padded
    n_blocks = -(-A // MOE_BLOCK) + E
    P = n_blocks * MOE_BLOCK
    experts = jnp.arange(E, dtype=I32)
    dest = jnp.sum(jnp.where(idx[:, :, None] == experts, pad_start, 0), axis=-1) + rank
    dest_flat = dest.reshape(A).astype(I32)
    blk_start = jnp.arange(n_blocks, dtype=I32) * MOE_BLOCK
    block_e = jnp.minimum(jnp.sum((pad_end[None, :] <= blk_start[:, None]).astype(I32), axis=1),
                          E - 1).astype(I32)
    n_used = (pad_end[-1] // MOE_BLOCK).astype(I32).reshape(1)
    base, xs = _dispatch_call(alpha, dest_flat, h1, p_i, w_pg, w_pp, P, xs_buf)
    y = _moe_call(layer, block_e, n_used, xs, w_gu, b_gu, w_down, b_down)
    return dest_flat, base, y, xs


def _layer(layer, h, u, g, gt, p_i, B, S, alpha, m_i_bias, m_f_bias, m_norm_w, sb_norm_w, g_conv_w,
           g_A_log, g_dt_bias, g_norm_w, w_out, ln1_g, ln1_b, w_router, b_router, w_gu, b_gu,
           w_down, b_down, w_pg, w_pp, xs_buf):
    ym = _mlstm_call(u, g, gt, m_i_bias, m_f_bias, m_norm_w, B, S)
    ys = _sb_call(u, sb_norm_w, B, S, (4 * M_WIDTH + 4 * G_WIDTH) // SB_WIDTH)
    yg = _gdn_call(u, g, gt, g_conv_w, g_A_log, g_dt_bias, g_norm_w, B, S, 4 * M_WIDTH // G_WIDTH)
    h1, idx, gates, rank, counts = _post_call(
        alpha, h, ym, ys, yg, w_out, w_router, b_router, ln1_g, ln1_b)
    dest_flat, base, y, xs = _moe_ffn(alpha, layer, h1, p_i, idx, rank, counts, w_pg, w_pp,
                                      w_gu, b_gu, w_down, b_down, xs_buf)
    return dest_flat, base, gates, y, xs


def kernel(x, p, ln0_g, ln0_b, w_in, m_i_bias, m_f_bias, m_norm_w, sb_norm_w, g_conv_w, g_A_log,
           g_dt_bias, g_norm_w, w_out, ln1_g, ln1_b, w_router, b_router, w_gu, b_gu, w_down,
           b_down, w_ple_gate, w_ple_proj, ln2_g, ln2_b):
    B, S, D = x.shape
    depth = w_in.shape[0]
    T = B * S
    alpha = (2 * depth) ** 0.25
    g0 = 4 * M_WIDTH
    g1 = g0 + 2 * M_HEADS
    s1 = g1 + 3 * SB_WIDTH
    g2 = s1 + 4 * G_WIDTH
    n_slots = (-(-T * TOP_K // MOE_BLOCK) + w_gu.shape[1]) * MOE_BLOCK
    xs_buf = jnp.zeros((n_slots, D // 2), jnp.uint32)
    h = _ln_call(x.reshape(T, D), ln0_g, ln0_b)
    pending = None
    for i in range(depth):
        w = w_in[i]
        w_main = jnp.concatenate([w[:, :g0], w[:, s1:g2], w[:, g1:s1]], axis=1).astype(BF16)
        wg = jnp.concatenate([w[:, g0:g1], w[:, g2:]], axis=1)
        if pending is None:
            u, g, gt = _inproj_call(h, w_main, wg)
        else:
            dest_flat, base, gates, y = pending
            h, u, g, gt = _combine_inproj_call(dest_flat, base, gates, y, ln2_g[i - 1], ln2_b[i - 1],
                                               w_main, wg)
        dest_flat, base, gates, y, xs_buf = _layer(
            i, h, u, g, gt, p[i].reshape(T, -1), B, S, alpha, m_i_bias[i], m_f_bias[i],
            m_norm_w[i], sb_norm_w[i], g_conv_w[i], g_A_log[i], g_dt_bias[i], g_norm_w[i],
            w_out[i].astype(BF16), ln1_g[i], ln1_b[i], w_router[i], b_router[i],
            w_gu, b_gu, w_down, b_down,
            w_ple_gate[i].astype(BF16), w_ple_proj[i].astype(BF16), xs_buf)
        pending = (dest_flat, base, gates, y)
    dest_flat, base, gates, y = pending
    h = _combine_call(dest_flat, base, gates, y, ln2_g[depth - 1], ln2_b[depth - 1])
    return h.reshape(B, S, D)
```

```python
import functools
import math

import jax
import jax.numpy as jnp
from jax import lax
from jax.experimental import pallas as pl
from jax.experimental.pallas import tpu as pltpu

F32 = jnp.float32
BF16 = jnp.bfloat16
I32 = jnp.int32

M_HEADS, M_HEAD_DIM = 4, 64
SB_HEADS, SB_HEAD_DIM = 4, 64
G_HEADS, G_HEAD_DIM = 4, 128
G_CONV = 4
M_WIDTH = M_HEADS * M_HEAD_DIM
SB_WIDTH = SB_HEADS * SB_HEAD_DIM
G_WIDTH = G_HEADS * G_HEAD_DIM
N_GATE_COLS = 16
TOP_K = 4
SWIGLU_LIMIT = 7.0
SWIGLU_ALPHA = 1.702
LN_EPS = 1e-5
NORM_EPS = 1e-6
LANES = 128

ROW_BLOCK = 256
POST_BLOCK = 512
IN_CHUNK = 256
M_CHUNK = 256
SB_BLOCK = 256
G_CHUNK = 64
G_BLOCK = 256
MOE_BLOCK = 512
DMA_UNROLL = 8
VMEM_LIMIT = 56 * 1024 * 1024


def _cparams(sem):
    return pltpu.CompilerParams(dimension_semantics=sem, vmem_limit_bytes=VMEM_LIMIT)


def _split3(x):
    hi = x.astype(BF16)
    r1 = x - hi.astype(F32)
    mid = r1.astype(BF16)
    lo = (r1 - mid.astype(F32)).astype(BF16)
    return hi, mid, lo


def _split2(x):
    hi = x.astype(BF16)
    lo = (x - hi.astype(F32)).astype(BF16)
    return hi, lo


def _dot(a, b):
    return jnp.dot(a, b, preferred_element_type=F32)


def _dot_nt(a, b):
    return lax.dot_general(a, b, (((1,), (1,)), ((), ())), preferred_element_type=F32)


def _dot_tn(a, b):
    return lax.dot_general(a, b, (((0,), (0,)), ((), ())), preferred_element_type=F32)


def _dot3(a, b):
    ah, al = _split2(a)
    bh, bl = _split2(b)
    return _dot(ah, bh) + (_dot(ah, bl) + _dot(al, bh))


def _dot3_nt(a, b):
    ah, al = _split2(a)
    bh, bl = _split2(b)
    return _dot_nt(ah, bh) + (_dot_nt(ah, bl) + _dot_nt(al, bh))


def _bdot(a, b):
    return lax.dot_general(a, b, (((2,), (1,)), ((0,), (0,))), preferred_element_type=F32)


def _bdot_nt(a, b):
    return lax.dot_general(a, b, (((2,), (2,)), ((0,), (0,))), preferred_element_type=F32)


def _bdot_tn(a, b):
    return lax.dot_general(a, b, (((1,), (1,)), ((0,), (0,))), preferred_element_type=F32)


def _bdot3(a, b):
    ah, al = _split2(a)
    bh, bl = _split2(b)
    return _bdot(ah, bh) + (_bdot(ah, bl) + _bdot(al, bh))


def _bdot3_nt(a, b):
    ah, al = _split2(a)
    bh, bl = _split2(b)
    return _bdot_nt(ah, bh) + (_bdot_nt(ah, bl) + _bdot_nt(al, bh))


def _exact_left(mask_bf16, x):
    hi, mid, lo = _split3(x)
    return _dot(mask_bf16, hi) + (_dot(mask_bf16, mid) + _dot(mask_bf16, lo))


def _exact_right(x, mask_bf16):
    hi, mid, lo = _split3(x)
    return _dot(hi, mask_bf16) + (_dot(mid, mask_bf16) + _dot(lo, mask_bf16))


def _softplus(x):
    return jnp.maximum(x, 0.0) + jnp.log1p(jnp.exp(-jnp.abs(x)))


def _log_sigmoid(x):
    return -_softplus(-x)


def _sigmoid(x):
    return 1.0 / (1.0 + jnp.exp(-x))


def _layer_norm(x, g, b):
    mu = jnp.mean(x, axis=-1, keepdims=True)
    xc = x - mu
    var = jnp.mean(xc * xc, axis=-1, keepdims=True)
    return xc * lax.rsqrt(var + LN_EPS) * g + b


def _iota2(shape, dim):
    return lax.broadcasted_iota(I32, shape, dim)


def _ln_kernel(x_ref, g_ref, b_ref, o_ref):
    o_ref[...] = _layer_norm(x_ref[...], g_ref[...], b_ref[...])


def _ln_call(x, g, b):
    T, D = x.shape
    tm = ROW_BLOCK
    return pl.pallas_call(
        _ln_kernel,
        grid=(T // tm,),
        in_specs=[pl.BlockSpec((tm, D), lambda i: (i, 0)),
                  pl.BlockSpec((1, D), lambda i: (0, 0)),
                  pl.BlockSpec((1, D), lambda i: (0, 0))],
        out_specs=pl.BlockSpec((tm, D), lambda i: (i, 0)),
        out_shape=jax.ShapeDtypeStruct((T, D), F32),
        compiler_params=_cparams(("parallel",)),
        name="embed_ln",
    )(x, g.reshape(1, D), b.reshape(1, D))


def _inproj_kernel(h_ref, w_ref, wgh_ref, wgl_ref, wgth_ref, wgtl_ref, u_ref, g_ref, gt_ref):
    h = h_ref[...]
    hh, hl = _split2(h)
    u_ref[...] = _dot(hh, w_ref[...])
    wgh, wgl = wgh_ref[...], wgl_ref[...]
    g_ref[...] = _dot(hh, wgh) + (_dot(hh, wgl) + _dot(hl, wgh))
    wgth, wgtl = wgth_ref[...], wgtl_ref[...]
    gt_ref[...] = _dot_nt(wgth, hh) + (_dot_nt(wgtl, hh) + _dot_nt(wgth, hl))


def _inproj_call(h, w_main, wg):
    T, D = h.shape
    N = w_main.shape[1]
    tm = ROW_BLOCK
    wgh, wgl = _split2(wg)
    wgt = wg.T
    wgth, wgtl = _split2(wgt)
    G = N_GATE_COLS
    const = lambda i: (0, 0)
    return pl.pallas_call(
        _inproj_kernel,
        grid=(T // tm,),
        in_specs=[pl.BlockSpec((tm, D), lambda i: (i, 0)),
                  pl.BlockSpec((D, N), const),
                  pl.BlockSpec((D, G), const), pl.BlockSpec((D, G), const),
                  pl.BlockSpec((G, D), const), pl.BlockSpec((G, D), const)],
        out_specs=[pl.BlockSpec((tm, N), lambda i: (i, 0)),
                   pl.BlockSpec((tm, G), lambda i: (i, 0)),
                   pl.BlockSpec((G, tm), lambda i: (0, i))],
        out_shape=[jax.ShapeDtypeStruct((T, N), F32),
                   jax.ShapeDtypeStruct((T, G), F32),
                   jax.ShapeDtypeStruct((G, T), F32)],
        compiler_params=_cparams(("parallel",)),
        name="in_proj",
    )(h, w_main, wgh, wgl, wgth, wgtl)


def _mlstm_kernel(q_ref, k_ref, v_ref, o_ref, g_ref, gt_ref, brow_ref, bcol_ref, nw_ref,
                  y_ref, c_sc, m_sc):
    L = q_ref.shape[0]
    d = M_HEAD_DIM

    @pl.when(pl.program_id(1) == 0)
    def _():
        c_sc[...] = jnp.zeros_like(c_sc)
        m_sc[...] = jnp.full_like(m_sc, -jnp.inf)

    row = _iota2((L, L), 0)
    col = _iota2((L, L), 1)
    causal = col <= row
    tril = causal.astype(BF16)
    triu = (row <= col).astype(BF16)

    gb = g_ref[...] + brow_ref[...]
    gtb = gt_ref[...] + bcol_ref[...]
    b_cols = _exact_left(tril, _log_sigmoid(gb))
    b_rows = _exact_right(_log_sigmoid(gtb), triu)

    ones_col = (_iota2((L, d), 1) == 0).astype(F32)
    H = range(M_HEADS)
    sl = [slice(hd * d, (hd + 1) * d) for hd in H]
    q = [q_ref[:, c].astype(BF16) for c in sl]
    kf = [k_ref[:, c] * (d ** -0.5) for c in sl]
    k = [x.astype(BF16) for x in kf]
    v_aug = [jnp.concatenate([v_ref[:, c], ones_col], axis=1).astype(BF16) for c in sl]
    bc = [b_cols[:, M_HEADS + hd:M_HEADS + hd + 1] for hd in H]
    lic = [gb[:, hd:hd + 1] for hd in H]
    src = [gtb[hd:hd + 1, :] - b_rows[M_HEADS + hd:M_HEADS + hd + 1, :] for hd in H]
    b_end = [x[L - 1:L, :] for x in bc]
    m_prev = [m_sc[hd] for hd in H]
    c_aug = [c_sc[hd] for hd in H]

    qk = [_dot_nt(q[h], k[h]) for h in H]
    qc = [_dot(q[h], c_aug[h].astype(BF16)) for h in H]
    log_w = [jnp.where(causal, bc[h] + src[h], -jnp.inf) for h in H]
    m_intra = [jnp.max(x, axis=1, keepdims=True) for x in log_w]
    carry_log = [bc[h] + m_prev[h] for h in H]
    m_t = [jnp.maximum(carry_log[h], m_intra[h]) for h in H]
    inter = [jnp.exp(carry_log[h] - m_t[h]) for h in H]
    s = [(qk[h] * jnp.exp(log_w[h] - m_t[h])).astype(BF16) for h in H]
    num_aug = [inter[h] * qc[h] + _dot(s[h], v_aug[h]) for h in H]
    hh = [num_aug[h][:, :d] / jnp.maximum(jnp.abs(num_aug[h][:, d:d + 1]), jnp.exp(-m_t[h]))
          for h in H]

    m_end = [jnp.max(b_end[h] + src[h], axis=1, keepdims=True) for h in H]
    m_new = [jnp.maximum(b_end[h] + m_prev[h], m_end[h]) for h in H]
    wk = [(jnp.exp(b_end[h] - bc[h] + lic[h] - m_new[h]) * kf[h]).astype(BF16) for h in H]
    kv = [_dot_tn(wk[h], v_aug[h]) for h in H]
    for h in H:
        c_sc[h] = jnp.exp(b_end[h] + m_prev[h] - m_new[h]) * c_aug[h] + kv[h]
        m_sc[h] = m_new[h]

    outs = [x * lax.rsqrt(jnp.mean(x * x, axis=1, keepdims=True) + NORM_EPS) for x in hh]
    hcat = jnp.concatenate(outs, axis=1) * nw_ref[...]
    y_ref[...] = _sigmoid(o_ref[...]) * hcat


def _mlstm_call(u, g, gt, i_bias, f_bias, norm_w, B, S):
    T = u.shape[0]
    L = min(M_CHUNK, S)
    nc = S // L
    W = M_WIDTH
    zeros = jnp.zeros((N_GATE_COLS - 2 * M_HEADS,), F32)
    bias = jnp.concatenate([i_bias, f_bias, zeros])
    G = N_GATE_COLS
    rowblk = lambda c: pl.BlockSpec((L, W), lambda b, j, c=c: (b * nc + j, c))
    const = lambda b, j: (0, 0)
    return pl.pallas_call(
        _mlstm_kernel,
        grid=(B, nc),
        in_specs=[rowblk(0), rowblk(1), rowblk(2), rowblk(3),
                  pl.BlockSpec((L, G), lambda b, j: (b * nc + j, 0)),
                  pl.BlockSpec((G, L), lambda b, j: (0, b * nc + j)),
                  pl.BlockSpec((1, G), const), pl.BlockSpec((G, 1), const),
                  pl.BlockSpec((1, W), const)],
        out_specs=pl.BlockSpec((L, W), lambda b, j: (b * nc + j, 0)),
        out_shape=jax.ShapeDtypeStruct((T, W), F32),
        scratch_shapes=[pltpu.VMEM((M_HEADS, M_HEAD_DIM, 2 * M_HEAD_DIM), F32),
                        pltpu.VMEM((M_HEADS, 1, 1), F32)],
        compiler_params=_cparams(("parallel", "arbitrary")),
        name="mlstm",
    )(u, u, u, u, g, gt, bias.reshape(1, G), bias.reshape(G, 1), norm_w.reshape(1, W))


def _sb_kernel(q_ref, k_ref, v_ref, nw_ref, y_ref, qb_sc, acc_sc, r_sc):
    tq = q_ref.shape[0]
    tk = tq
    d = SB_HEAD_DIM
    qi = pl.program_id(1)
    srow = _iota2((tq, tk), 0)
    scol = _iota2((tq, tk), 1)
    later = (scol < srow).astype(BF16)

    qb_sc[...] = (q_ref[...] * (d ** -0.5)).astype(BF16)
    acc_sc[...] = jnp.zeros_like(acc_sc)
    r_sc[...] = jnp.zeros_like(r_sc)

    def block(kb, diagonal):
        off = pl.multiple_of(kb * tk, tk)
        heads = range(SB_HEADS)
        cols = [slice(hd * d, (hd + 1) * d) for hd in heads]
        valid = scol < srow
        z = [_dot_nt(qb_sc[:, c], k_ref[pl.ds(off, tk), c].astype(BF16)) for c in cols]
        sp = [jnp.maximum(x, 0.0) + jnp.log(1.0 + jnp.exp(-jnp.abs(x))) for x in z]
        if diagonal:
            sp = [jnp.where(valid, x, 0.0) for x in sp]
        c2 = [_dot(jnp.concatenate(_split2(x), axis=0), later) for x in sp]
        cs = [x[:tq] + x[tq:] for x in c2]
        a = [jnp.exp(z[h] - sp[h] - cs[h]) for h in heads]
        if diagonal:
            a = [jnp.where(valid, x, 0.0) for x in a]
        pv = [_dot(a[h].astype(BF16), v_ref[pl.ds(off, tk), cols[h]].astype(BF16)) for h in heads]
        for h in heads:
            rest = r_sc[h]
            acc_sc[h] = acc_sc[h] + jnp.exp(-rest) * pv[h]
            r_sc[h] = rest + (cs[h][:, 0:1] + sp[h][:, 0:1])

    block(qi, True)

    def body(it, carry):
        block(qi - it, False)
        return carry

    lax.fori_loop(1, qi + 1, body, 0)
    outs = []
    for hd in range(SB_HEADS):
        o = acc_sc[hd]
        outs.append(o * lax.rsqrt(jnp.mean(o * o, axis=1, keepdims=True) + NORM_EPS))
    y_ref[...] = jnp.concatenate(outs, axis=1) * nw_ref[...]


def _sb_call(u, norm_w, B, S, col0):
    T = u.shape[0]
    tq = min(SB_BLOCK, S)
    nq = S // tq
    W = SB_WIDTH
    return pl.pallas_call(
        _sb_kernel,
        grid=(B, nq),
        in_specs=[pl.BlockSpec((tq, W), lambda b, i: (b * nq + i, col0)),
                  pl.BlockSpec((S, W), lambda b, i: (b, col0 + 1)),
                  pl.BlockSpec((S, W), lambda b, i: (b, col0 + 2)),
                  pl.BlockSpec((1, W), lambda b, i: (0, 0))],
        out_specs=pl.BlockSpec((tq, W), lambda b, i: (b * nq + i, 0)),
        out_shape=jax.ShapeDtypeStruct((T, W), F32),
        scratch_shapes=[pltpu.VMEM((tq, W), BF16),
                        pltpu.VMEM((SB_HEADS, tq, SB_HEAD_DIM), F32),
                        pltpu.VMEM((SB_HEADS, tq, 1), F32)],
        compiler_params=_cparams(("parallel", "arbitrary")),
        name="stickbreak",
    )(u, u, u, norm_w.reshape(1, W))


def _gdn_kernel(q_ref, k_ref, v_ref, z_ref, g_ref, gt_ref, cw_ref, arow_ref, acol_ref,
                drow_ref, dcol_ref, nw_ref, y_ref, xq_sc, xk_sc, xv_sc, s_sc):
    LB = q_ref.shape[0]
    C = G_CHUNK
    d = G_HEAD_DIM
    W = G_WIDTH
    HALO = 8

    @pl.when(pl.program_id(1) == 0)
    def _():
        s_sc[...] = jnp.zeros_like(s_sc)
        for sc in (xq_sc, xk_sc, xv_sc):
            sc[0:HALO, :] = jnp.zeros((HALO, W), F32)

    @pl.when(pl.program_id(1) != 0)
    def _():
        for sc in (xq_sc, xk_sc, xv_sc):
            sc[0:HALO, :] = sc[LB:LB + HALO, :]

    def conv_silu(x_ref, sc, j):
        sc[HALO:HALO + LB, :] = x_ref[...]
        acc = None
        for t in range(G_CONV):
            w = cw_ref[t:t + 1, j * W:(j + 1) * W]
            term = w * sc[HALO - (G_CONV - 1) + t:HALO - (G_CONV - 1) + t + LB, :]
            acc = term if acc is None else acc + term
        return acc * _sigmoid(acc)

    qc = conv_silu(q_ref, xq_sc, 0)
    kc = conv_silu(k_ref, xk_sc, 1)
    vc = conv_silu(v_ref, xv_sc, 2)

    gb = g_ref[...]
    gtb = gt_ref[...]
    gdec_cols = -jnp.exp(arow_ref[...]) * _softplus(gb + drow_ref[...])
    gdec_rows = -jnp.exp(acol_ref[...]) * _softplus(gtb + dcol_ref[...])
    beta_cols = _sigmoid(gb)
    row = _iota2((LB, LB), 0)
    col = _iota2((LB, LB), 1)
    shift = C.bit_length() - 1
    same = jnp.right_shift(row, shift) == jnp.right_shift(col, shift)
    tril = (same & (col <= row)).astype(BF16)
    triu = (same & (row <= col)).astype(BF16)
    gam_cols = _exact_left(tril, gdec_cols)
    gam_rows = _exact_right(gdec_rows, triu)

    nc = LB // C
    NB = G_HEADS * nc

    def per_head(x):
        return jnp.stack([x[:, h * d:(h + 1) * d] for h in range(G_HEADS)], axis=0).reshape(NB, C, d)

    q3 = per_head(qc)
    k3 = per_head(kc)
    v3 = per_head(vc)
    q3 = q3 * lax.rsqrt(jnp.sum(q3 * q3, axis=2, keepdims=True) + NORM_EPS) * (d ** -0.5)
    k3 = k3 * lax.rsqrt(jnp.sum(k3 * k3, axis=2, keepdims=True) + NORM_EPS)
    ga0 = 2 * G_HEADS
    gb0 = 3 * G_HEADS
    gam_c = jnp.stack([gam_cols[:, ga0 + h:ga0 + h + 1] for h in range(G_HEADS)], 0).reshape(NB, C, 1)
    beta = jnp.stack([beta_cols[:, gb0 + h:gb0 + h + 1] for h in range(G_HEADS)], 0).reshape(NB, C, 1)
    gam_r = jnp.stack([gam_rows[ga0 + h:ga0 + h + 1, c * C:(c + 1) * C]
                       for h in range(G_HEADS) for c in range(nc)], 0)
    gam_end = gam_c[:, C - 1:C, :]

    r64 = _iota2((1, C, C), 1)
    c64 = _iota2((1, C, C), 2)
    incl = c64 <= r64
    strict = c64 < r64

    decay = jnp.exp(jnp.where(incl, gam_c - gam_r, -jnp.inf))
    kk = _bdot3_nt(k3, k3)
    m = jnp.where(strict, beta * kk * decay, 0.0)
    tinv = jnp.broadcast_to((r64 == c64).astype(F32), (NB, C, C))
    s = 1
    while s < C:
        sh = s.bit_length() - 1
        same2s = jnp.right_shift(r64, sh + 1) == jnp.right_shift(c64, sh + 1)
        low_mask = same2s & ((jnp.right_shift(r64, sh) & 1) == 1) & ((jnp.right_shift(c64, sh) & 1) == 0)
        low = jnp.where(low_mask, m, 0.0).astype(BF16)
        tb = tinv.astype(BF16)
        tinv = tinv - _bdot(_bdot(tb, low).astype(BF16), tb)
        s *= 2
    eg = jnp.exp(gam_c)
    tb = tinv.astype(BF16)
    u = _bdot(tb, (v3 * beta).astype(BF16))
    w = _bdot(tb, (k3 * (beta * eg)).astype(BF16)).astype(BF16)
    qk = (_bdot_nt(q3.astype(BF16), k3.astype(BF16)) * decay).astype(BF16)
    q_dec = (q3 * eg).astype(BF16)
    k_dec = (k3 * jnp.exp(gam_end - gam_c)).astype(BF16)
    cdec = jnp.exp(gam_end)

    def chunk(x, c):
        return x.reshape((G_HEADS, nc) + x.shape[1:])[:, c]

    state = s_sc[...]
    o_chunks = []
    for c in range(nc):
        sb = state.astype(BF16)
        v_new = chunk(u, c) - _bdot(chunk(w, c), sb)
        vb = v_new.astype(BF16)
        o_chunks.append(_bdot(chunk(q_dec, c), sb) + _bdot(chunk(qk, c), vb))
        state = chunk(cdec, c) * state + _bdot_tn(chunk(k_dec, c), vb)
    s_sc[...] = state

    nw = nw_ref[...]
    o = jnp.concatenate(o_chunks, axis=1)
    o = o * lax.rsqrt(jnp.mean(o * o, axis=2, keepdims=True) + NORM_EPS) * nw
    zz = z_ref[...]
    y_ref[...] = jnp.concatenate([o[h] for h in range(G_HEADS)], axis=1) * (zz * _sigmoid(zz))


def _gdn_call(u, g, gt, conv_w, a_log, dt_bias, norm_w, B, S, col0):
    T = u.shape[0]
    LB = min(G_BLOCK, S)
    nb = S // LB
    W = G_WIDTH
    G = N_GATE_COLS
    pad = jnp.zeros((2 * G_HEADS,), F32)
    arow = jnp.concatenate([pad, a_log, jnp.zeros((G_HEADS,), F32)])
    drow = jnp.concatenate([pad, dt_bias, jnp.zeros((G_HEADS,), F32)])
    rowblk = lambda c: pl.BlockSpec((LB, W), lambda b, j, c=c: (b * nb + j, col0 + c))
    const = lambda b, j: (0, 0)
    return pl.pallas_call(
        _gdn_kernel,
        grid=(B, nb),
        in_specs=[rowblk(0), rowblk(1), rowblk(2), rowblk(3),
                  pl.BlockSpec((LB, G), lambda b, j: (b * nb + j, 0)),
                  pl.BlockSpec((G, LB), lambda b, j: (0, b * nb + j)),
                  pl.BlockSpec((G_CONV, 3 * W), const),
                  pl.BlockSpec((1, G), const), pl.BlockSpec((G, 1), const),
                  pl.BlockSpec((1, G), const), pl.BlockSpec((G, 1), const),
                  pl.BlockSpec((1, G_HEAD_DIM), const)],
        out_specs=pl.BlockSpec((LB, W), lambda b, j: (b * nb + j, 0)),
        out_shape=jax.ShapeDtypeStruct((T, W), F32),
        scratch_shapes=[pltpu.VMEM((LB + 8, W), F32), pltpu.VMEM((LB + 8, W), F32),
                        pltpu.VMEM((LB + 8, W), F32),
                        pltpu.VMEM((G_HEADS, G_HEAD_DIM, G_HEAD_DIM), F32)],
        compiler_params=_cparams(("parallel", "arbitrary")),
        name="gdn",
    )(u, u, u, u, g, gt, conv_w, arow.reshape(1, G), arow.reshape(G, 1),
      drow.reshape(1, G), drow.reshape(G, 1), norm_w.reshape(1, G_HEAD_DIM))


def _post_kernel(alpha, h_ref, ym_ref, ys_ref, yg_ref, wo_ref,
                 wrh_ref, wrl_ref, br_ref, g1_ref, b1_ref,
                 h1_ref, idx_ref, gate_ref, rank_ref, cnt_ref, cnt_sc):
    tm = h_ref.shape[0]
    E = br_ref.shape[1]

    @pl.when(pl.program_id(0) == 0)
    def _():
        cnt_sc[...] = jnp.zeros_like(cnt_sc)

    y = jnp.concatenate([ym_ref[...], ys_ref[...], yg_ref[...]], axis=1).astype(BF16)
    a = alpha * h_ref[...] + _dot(y, wo_ref[...])
    h1 = _layer_norm(a, g1_ref[...], b1_ref[...])
    h1h, h1l = _split2(h1)
    h1_ref[...] = h1

    wrh, wrl = wrh_ref[...], wrl_ref[...]
    logits = _dot(h1h, wrh) + (_dot(h1h, wrl) + _dot(h1l, wrh)) + br_ref[...]
    lane = _iota2((tm, E), 1).astype(F32)
    work = logits
    vals, idxs = [], []
    anyhot = jnp.zeros((tm, E), F32)
    for _ in range(TOP_K):
        mx = jnp.max(work, axis=1, keepdims=True)
        ix = jnp.min(jnp.where(work == mx, lane, float(E)), axis=1, keepdims=True)
        sel = lane == ix
        vals.append(mx)
        idxs.append(ix)
        anyhot = jnp.where(sel, 1.0, anyhot)
        work = jnp.where(sel, -jnp.inf, work)
    ex = [jnp.exp(v - vals[0]) for v in vals]
    tot = ex[0] + ex[1] + ex[2] + ex[3]
    gates = [e / tot for e in ex]

    r = _iota2((tm, tm), 0)
    c = _iota2((tm, tm), 1)
    before = (c < r).astype(BF16)
    pos = cnt_sc[...] + _dot(before, anyhot.astype(BF16))
    ranks = [jnp.sum(jnp.where(lane == ix, pos, 0.0), axis=1, keepdims=True) for ix in idxs]
    cnt_sc[...] = cnt_sc[...] + jnp.sum(anyhot, axis=0, keepdims=True)
    cnt_ref[...] = cnt_sc[...].astype(I32)

    kl = _iota2((tm, TOP_K), 1)

    def pack(cols):
        out = jnp.broadcast_to(cols[0], (tm, TOP_K))
        for k in range(1, TOP_K):
            out = jnp.where(kl == k, cols[k], out)
        return out

    idx_ref[...] = pack(idxs).astype(I32)
    gate_ref[...] = pack(gates)
    rank_ref[...] = pack(ranks).astype(I32)


def _post_call(alpha, h, ym, ys, yg, w_out, w_router, b_router, ln_g, ln_b):
    T, D = h.shape
    tm = POST_BLOCK
    E = w_router.shape[1]
    wrh, wrl = _split2(w_router)
    const = lambda i: (0, 0)
    rows = lambda w: pl.BlockSpec((tm, w), lambda i: (i, 0))
    return pl.pallas_call(
        functools.partial(_post_kernel, alpha),
        grid=(T // tm,),
        in_specs=[rows(D), rows(M_WIDTH), rows(SB_WIDTH), rows(G_WIDTH),
                  pl.BlockSpec(w_out.shape, const),
                  pl.BlockSpec((D, E), const), pl.BlockSpec((D, E), const),
                  pl.BlockSpec((1, E), const), pl.BlockSpec((1, D), const),
                  pl.BlockSpec((1, D), const)],
        out_specs=[rows(D), rows(TOP_K), rows(TOP_K), rows(TOP_K),
                   pl.BlockSpec((1, E), const)],
        out_shape=[jax.ShapeDtypeStruct((T, D), F32),
                   jax.ShapeDtypeStruct((T, TOP_K), I32), jax.ShapeDtypeStruct((T, TOP_K), F32),
                   jax.ShapeDtypeStruct((T, TOP_K), I32), jax.ShapeDtypeStruct((1, E), I32)],
        scratch_shapes=[pltpu.VMEM((1, E), F32)],
        compiler_params=_cparams(("arbitrary",)),
        name="post_mix",
    )(h, ym, ys, yg, w_out, wrh, wrl, b_router.reshape(1, E),
      ln_g.reshape(1, D), ln_b.reshape(1, D))


def _to_token_tiles(ref, x):
    for j in range(ref.shape[-2]):
        ref[:, j, :] = x[:, j * LANES:(j + 1) * LANES]


def _from_token_tiles(ref):
    return jnp.concatenate([ref[:, j, :] for j in range(ref.shape[-2])], axis=1)


def _dispatch_kernel(alpha, dest_ref, h1_ref, p_ref, wpg_ref, wpp_ref, xs_in_ref,
                     base_ref, xs_ref, xpk_sc, sem):
    del xs_in_ref
    tm, D = h1_ref.shape
    half = D // 2
    h1 = h1_ref[...]
    hb = h1.astype(BF16)
    bits = lax.bitcast_convert_type(hb.astype(F32), jnp.uint32)
    _to_token_tiles(xpk_sc, (bits[:, :half] >> 16) | (bits[:, half:] & jnp.uint32(0xFFFF0000)))

    pb = p_ref[...].astype(BF16)
    nch = D // IN_CHUNK
    rows_per = -(-tm // nch)
    for c in range(nch):
        cs = slice(c * IN_CHUNK, (c + 1) * IN_CHUNK)
        ple = _sigmoid(_dot(hb, wpg_ref[:, cs])) * _dot(pb, wpp_ref[:, cs])
        base_ref[:, cs] = alpha * h1[:, cs] + ple
        for r in range(c * rows_per, min((c + 1) * rows_per, tm)):
            for k in range(TOP_K):
                pltpu.make_async_copy(xpk_sc.at[r], xs_ref.at[dest_ref[r * TOP_K + k]], sem).start()
    for k in range(TOP_K):
        pltpu.make_async_copy(xpk_sc, xs_ref.at[pl.ds(0, tm)], sem).wait()


def _dispatch_call(alpha, dest_flat, h1, p, w_pg, w_pp, xs0):
    T, D = h1.shape
    tm = ROW_BLOCK
    P = p.shape[1]
    const = lambda i: (0, 0)
    rows = lambda w: pl.BlockSpec((tm, w), lambda i: (i, 0))
    base, xs = pl.pallas_call(
        functools.partial(_dispatch_kernel, alpha),
        grid=(T // tm,),
        in_specs=[pl.BlockSpec((tm * TOP_K,), lambda i: (i,), memory_space=pltpu.SMEM),
                  rows(D), rows(P), pl.BlockSpec((D, D), const), pl.BlockSpec((P, D), const),
                  pl.BlockSpec(memory_space=pl.ANY)],
        out_specs=[rows(D), pl.BlockSpec(memory_space=pl.ANY)],
        out_shape=[jax.ShapeDtypeStruct((T, D), F32),
                   jax.ShapeDtypeStruct(xs0.shape, jnp.uint32)],
        scratch_shapes=[pltpu.VMEM((tm,) + xs0.shape[1:], jnp.uint32), pltpu.SemaphoreType.DMA],
        input_output_aliases={5: 1},
        compiler_params=_cparams(("arbitrary",)),
        name="ple_dispatch",
    )(dest_flat, h1, p, w_pg, w_pp, xs0)
    return base, xs


def _moe_kernel(be_ref, nb_ref, x_ref, wgu_ref, bgu_ref, wd_ref, bd_ref, y_ref, wgu_sc, wd_sc):
    i = pl.program_id(0)
    F = wd_ref.shape[2]
    CH = 128

    new_expert = jnp.logical_or(i == 0, be_ref[i] != be_ref[jnp.maximum(i - 1, 0)])

    @pl.when(jnp.logical_and(new_expert, i < nb_ref[0]))
    def _():
        def cast_gu(r, c):
            rows = pl.ds(pl.multiple_of(r * CH, CH), CH)
            wgu_sc[rows, :] = wgu_ref[0, 0, rows, :].astype(BF16)
            return c

        def cast_d(r, c):
            rows = pl.ds(pl.multiple_of(r * CH, CH), CH)
            wd_sc[rows, :] = wd_ref[0, 0, rows, :].astype(BF16)
            return c

        lax.fori_loop(0, wgu_ref.shape[2] // CH, cast_gu, 0)
        lax.fori_loop(0, F // CH, cast_d, 0)

    @pl.when(i < nb_ref[0])
    def _():
        xw = _from_token_tiles(x_ref)
        lo = lax.bitcast_convert_type(xw << 16, F32)
        hi = lax.bitcast_convert_type(xw & jnp.uint32(0xFFFF0000), F32)
        xb = jnp.concatenate([lo, hi], axis=1).astype(BF16)
        h = _dot(xb, wgu_sc[...]) + bgu_ref[0, 0]
        gate = jnp.minimum(h[:, :F], SWIGLU_LIMIT)
        up = jnp.clip(h[:, F:], -SWIGLU_LIMIT, SWIGLU_LIMIT)
        act = (up + 1.0) * gate * _sigmoid(SWIGLU_ALPHA * gate)
        y_ref[...] = _dot(act.astype(BF16), wd_sc[...]) + bd_ref[0, 0]

    @pl.when(i >= nb_ref[0])
    def _():
        y_ref[...] = jnp.zeros_like(y_ref)


def _moe_call(layer, block_e, n_used, xs, w_gu, b_gu, w_down, b_down):
    P = xs.shape[0]
    L, E, D, F2 = w_gu.shape
    F = F2 // 2
    tm = MOE_BLOCK
    nblk = P // tm
    grid_spec = pltpu.PrefetchScalarGridSpec(
        num_scalar_prefetch=2,
        grid=(nblk,),
        in_specs=[pl.BlockSpec((tm,) + xs.shape[1:], lambda i, be, nb: (i, 0, 0)),
                  pl.BlockSpec((1, 1, D, F2), lambda i, be, nb: (layer, be[i], 0, 0)),
                  pl.BlockSpec((1, 1, 1, F2), lambda i, be, nb: (layer, be[i], 0, 0)),
                  pl.BlockSpec((1, 1, F, D), lambda i, be, nb: (layer, be[i], 0, 0)),
                  pl.BlockSpec((1, 1, 1, D), lambda i, be, nb: (layer, be[i], 0, 0))],
        out_specs=pl.BlockSpec((tm, D), lambda i, be, nb: (i, 0)),
        scratch_shapes=[pltpu.VMEM((D, F2), BF16), pltpu.VMEM((F, D), BF16)],
    )
    return pl.pallas_call(
        _moe_kernel,
        grid_spec=grid_spec,
        out_shape=jax.ShapeDtypeStruct((P, D), F32),
        compiler_params=_cparams(("arbitrary",)),
        name="moe_experts",
    )(block_e, n_used, xs, w_gu, b_gu.reshape(L, E, 1, F2), w_down, b_down.reshape(L, E, 1, D))


def _combine_kernel(dest_ref, dnext_ref, base_ref, gate_ref, y_ref, g_ref, b_ref, o_ref, ybuf, sems):
    tm = base_ref.shape[0]
    i = pl.program_id(0)
    n = pl.num_programs(0)
    slot = i % 2

    def gather(idx_ref, sl):
        def issue(r, carry):
            for k in range(TOP_K):
                pltpu.make_async_copy(y_ref.at[pl.ds(idx_ref[r * TOP_K + k], 1), :],
                                      ybuf.at[sl, k, pl.ds(r, 1), :], sems.at[sl]).start()
            return carry
        lax.fori_loop(0, tm, issue, 0, unroll=DMA_UNROLL)

    @pl.when(i == 0)
    def _():
        gather(dest_ref, 0)

    @pl.when(i + 1 < n)
    def _():
        gather(dnext_ref, 1 - slot)

    for k in range(TOP_K):
        pltpu.make_async_copy(y_ref.at[pl.ds(0, tm), :], ybuf.at[slot, k], sems.at[slot]).wait()
    gate = gate_ref[...]
    acc = base_ref[...]
    for k in range(TOP_K):
        acc = acc + gate[:, k:k + 1] * ybuf[slot, k]
    o_ref[...] = _layer_norm(acc, g_ref[...], b_ref[...])


def _combine_call(dest_flat, base, gates, y, ln_g, ln_b):
    T, D = base.shape
    tm = ROW_BLOCK
    rows = lambda w: pl.BlockSpec((tm, w), lambda i: (i, 0))
    const = lambda i: (0, 0)
    nblk = T // tm
    return pl.pallas_call(
        _combine_kernel,
        grid=(nblk,),
        in_specs=[pl.BlockSpec((tm * TOP_K,), lambda i: (i,), memory_space=pltpu.SMEM),
                  pl.BlockSpec((tm * TOP_K,), lambda i: (jnp.minimum(i + 1, nblk - 1),),
                               memory_space=pltpu.SMEM),
                  rows(D), rows(TOP_K), pl.BlockSpec(memory_space=pl.ANY),
                  pl.BlockSpec((1, D), const), pl.BlockSpec((1, D), const)],
        out_specs=rows(D),
        out_shape=jax.ShapeDtypeStruct((T, D), F32),
        scratch_shapes=[pltpu.VMEM((2, TOP_K, tm, D), F32), pltpu.SemaphoreType.DMA((2,))],
        compiler_params=_cparams(("arbitrary",)),
        name="combine_ln",
    )(dest_flat, dest_flat, base, gates, y, ln_g.reshape(1, D), ln_b.reshape(1, D))


def _combine_inproj_kernel(dest_ref, dnext_ref, base_ref, gate_ref, y_ref, g_ref, b_ref,
                           w_ref, wgh_ref, wgl_ref, wgth_ref, wgtl_ref,
                           h_ref, u_ref, go_ref, gto_ref, ybuf, sems):
    tm = base_ref.shape[0]
    N = w_ref.shape[1]
    i = pl.program_id(0)
    n = pl.num_programs(0)
    slot = i % 2
    other = 1 - slot

    def row_copy(idx_ref, r, k, sl):
        return pltpu.make_async_copy(y_ref.at[pl.ds(idx_ref[r * TOP_K + k], 1), :],
                                     ybuf.at[sl, k, pl.ds(r, 1), :], sems.at[sl])

    @pl.when(i == 0)
    def _():
        def issue(r, carry):
            for k in range(TOP_K):
                row_copy(dest_ref, r, k, 0).start()
            return carry
        lax.fori_loop(0, tm, issue, 0, unroll=DMA_UNROLL)

    def wait_slot(sl):
        for k in range(TOP_K):
            pltpu.make_async_copy(y_ref.at[pl.ds(0, tm), :], ybuf.at[sl, k], sems.at[sl]).wait()

    wait_slot(slot)
    gate = gate_ref[...]
    acc = base_ref[...]
    for k in range(TOP_K):
        acc = acc + gate[:, k:k + 1] * ybuf[slot, k]
    h = _layer_norm(acc, g_ref[...], b_ref[...])
    h_ref[...] = h
    hh, hl = _split2(h)

    nch = N // IN_CHUNK
    rows_per = -(-tm // nch)
    for c in range(nch):
        cs = slice(c * IN_CHUNK, (c + 1) * IN_CHUNK)
        u_ref[:, cs] = _dot(hh, w_ref[:, cs])
        for r in range(c * rows_per, min((c + 1) * rows_per, tm)):
            for k in range(TOP_K):
                row_copy(dnext_ref, r, k, other).start()
    wgh, wgl = wgh_ref[...], wgl_ref[...]
    go_ref[...] = _dot(hh, wgh) + (_dot(hh, wgl) + _dot(hl, wgh))
    wgth, wgtl = wgth_ref[...], wgtl_ref[...]
    gto_ref[...] = _dot_nt(wgth, hh) + (_dot_nt(wgtl, hh) + _dot_nt(wgth, hl))

    @pl.when(i == n - 1)
    def _():
        wait_slot(other)


def _combine_inproj_call(dest_flat, base, gates, y, ln_g, ln_b, w_main, wg):
    T, D = base.shape
    N = w_main.shape[1]
    tm = ROW_BLOCK
    G = N_GATE_COLS
    wgh, wgl = _split2(wg)
    wgth, wgtl = _split2(wg.T)
    rows = lambda w: pl.BlockSpec((tm, w), lambda i: (i, 0))
    const = lambda i: (0, 0)
    nblk = T // tm
    return pl.pallas_call(
        _combine_inproj_kernel,
        grid=(nblk,),
        in_specs=[pl.BlockSpec((tm * TOP_K,), lambda i: (i,), memory_space=pltpu.SMEM),
                  pl.BlockSpec((tm * TOP_K,), lambda i: (jnp.minimum(i + 1, nblk - 1),),
                               memory_space=pltpu.SMEM),
                  rows(D), rows(TOP_K), pl.BlockSpec(memory_space=pl.ANY),
                  pl.BlockSpec((1, D), const), pl.BlockSpec((1, D), const),
                  pl.BlockSpec((D, N), const),
                  pl.BlockSpec((D, G), const), pl.BlockSpec((D, G), const),
                  pl.BlockSpec((G, D), const), pl.BlockSpec((G, D), const)],
        out_specs=[rows(D), rows(N), rows(G), pl.BlockSpec((G, tm), lambda i: (0, i))],
        out_shape=[jax.ShapeDtypeStruct((T, D), F32), jax.ShapeDtypeStruct((T, N), F32),
                   jax.ShapeDtypeStruct((T, G), F32), jax.ShapeDtypeStruct((G, T), F32)],
        scratch_shapes=[pltpu.VMEM((2, TOP_K, tm, D), F32), pltpu.SemaphoreType.DMA((2,))],
        compiler_params=_cparams(("arbitrary",)),
        name="combine_in_proj",
    )(dest_flat, dest_flat, base, gates, y, ln_g.reshape(1, D), ln_b.reshape(1, D),
      w_main, wgh, wgl, wgth, wgtl)


def _moe_ffn(alpha, layer, h1, p_i, idx, rank, counts, w_pg, w_pp, w_gu, b_gu, w_down, b_down,
             xs_buf):
    T, D = h1.shape
    E = w_gu.shape[1]
    A = T * TOP_K
    counts = counts.reshape(E)
    padded = (counts + MOE_BLOCK - 1) // MOE_BLOCK * MOE_BLOCK
    pad_end = jnp.cumsum(padded)
    pad_start = pad_end - padded
    n_blocks = -(-A // MOE_BLOCK) + E
    P = n_blocks * MOE_BLOCK
    experts = jnp.arange(E, dtype=I32)
    dest = jnp.sum(jnp.where(idx[:, :, None] == experts, pad_start, 0), axis=-1) + rank
    dest_flat = dest.reshape(A).astype(I32)
    blk_start = jnp.arange(n_blocks, dtype=I32) * MOE_BLOCK
    block_e = jnp.minimum(jnp.sum((pad_end[None, :] <= blk_start[:, None]).astype(I32), axis=1),
                          E - 1).astype(I32)
    n_used = (pad_end[-1] // MOE_BLOCK).astype(I32).reshape(1)
    assert xs_buf.shape[0] == P
    base, xs = _dispatch_call(alpha, dest_flat, h1, p_i, w_pg, w_pp, xs_buf)
    y = _moe_call(layer, block_e, n_used, xs, w_gu, b_gu, w_down, b_down)
    return dest_flat, base, y, xs


def _layer(layer, h, u, g, gt, p_i, B, S, alpha, m_i_bias, m_f_bias, m_norm_w, sb_norm_w, g_conv_w,
           g_A_log, g_dt_bias, g_norm_w, w_out, ln1_g, ln1_b, w_router, b_router, w_gu, b_gu,
           w_down, b_down, w_pg, w_pp, xs_buf):
    ym = _mlstm_call(u, g, gt, m_i_bias, m_f_bias, m_norm_w, B, S)
    ys = _sb_call(u, sb_norm_w, B, S, (4 * M_WIDTH + 4 * G_WIDTH) // SB_WIDTH)
    yg = _gdn_call(u, g, gt, g_conv_w, g_A_log, g_dt_bias, g_norm_w, B, S, 4 * M_WIDTH // G_WIDTH)
    h1, idx, gates, rank, counts = _post_call(
        alpha, h, ym, ys, yg, w_out, w_router, b_router, ln1_g, ln1_b)
    dest_flat, base, y, xs = _moe_ffn(alpha, layer, h1, p_i, idx, rank, counts, w_pg, w_pp,
                                      w_gu, b_gu, w_down, b_down, xs_buf)
    return dest_flat, base, gates, y, xs


def kernel(x, p, ln0_g, ln0_b, w_in, m_i_bias, m_f_bias, m_norm_w, sb_norm_w, g_conv_w, g_A_log,
           g_dt_bias, g_norm_w, w_out, ln1_g, ln1_b, w_router, b_router, w_gu, b_gu, w_down,
           b_down, w_ple_gate, w_ple_proj, ln2_g, ln2_b):
    B, S, D = x.shape
    depth = w_in.shape[0]
    T = B * S
    alpha = (2 * depth) ** 0.25
    g0 = 4 * M_WIDTH
    g1 = g0 + 2 * M_HEADS
    s1 = g1 + 3 * SB_WIDTH
    g2 = s1 + 4 * G_WIDTH
    n_slots = (-(-T * TOP_K // MOE_BLOCK) + w_gu.shape[1]) * MOE_BLOCK
    xs_buf = jnp.zeros((n_slots, D // 2 // LANES, LANES), jnp.uint32)
    h = _ln_call(x.reshape(T, D), ln0_g, ln0_b)
    pending = None
    for i in range(depth):
        w = w_in[i]
        w_main = jnp.concatenate([w[:, :g0], w[:, s1:g2], w[:, g1:s1]], axis=1).astype(BF16)
        wg = jnp.concatenate([w[:, g0:g1], w[:, g2:]], axis=1)
        if pending is None:
            u, g, gt = _inproj_call(h, w_main, wg)
        else:
            dest_flat, base, gates, y = pending
            h, u, g, gt = _combine_inproj_call(dest_flat, base, gates, y, ln2_g[i - 1], ln2_b[i - 1],
                                               w_main, wg)
        dest_flat, base, gates, y, xs_buf = _layer(
            i, h, u, g, gt, p[i].reshape(T, -1), B, S, alpha, m_i_bias[i], m_f_bias[i],
            m_norm_w[i], sb_norm_w[i], g_conv_w[i], g_A_log[i], g_dt_bias[i], g_norm_w[i],
            w_out[i].astype(BF16), ln1_g[i], ln1_b[i], w_router[i], b_router[i],
            w_gu, b_gu, w_down, b_down,
            w_ple_gate[i].astype(BF16), w_ple_proj[i].astype(BF16), xs_buf)
        pending = (dest_flat, base, gates, y)
    dest_flat, base, gates, y = pending
    h = _combine_call(dest_flat, base, gates, y, ln2_g[depth - 1], ln2_b[depth - 1])
    return h.reshape(B, S, D)
```

```python
import functools
import math

import jax
import jax.numpy as jnp
from jax import lax
from jax.experimental import pallas as pl
from jax.experimental.pallas import tpu as pltpu

F32 = jnp.float32
BF16 = jnp.bfloat16
I32 = jnp.int32

M_HEADS, M_HEAD_DIM = 4, 64
SB_HEADS, SB_HEAD_DIM = 4, 64
G_HEADS, G_HEAD_DIM = 4, 128
G_CONV = 4
M_WIDTH = M_HEADS * M_HEAD_DIM
SB_WIDTH = SB_HEADS * SB_HEAD_DIM
G_WIDTH = G_HEADS * G_HEAD_DIM
N_GATE_COLS = 16
TOP_K = 4
SWIGLU_LIMIT = 7.0
SWIGLU_ALPHA = 1.702
LN_EPS = 1e-5
NORM_EPS = 1e-6
LANES = 128

ROW_BLOCK = 256
POST_BLOCK = 512
IN_CHUNK = 256
M_CHUNK = 256
SB_BLOCK = 256
G_CHUNK = 64
G_BLOCK = 256
MOE_BLOCK = 512
DMA_UNROLL = 8
DMA_THREADS = 2
VMEM_LIMIT = 56 * 1024 * 1024


def _cparams(sem):
    return pltpu.CompilerParams(dimension_semantics=sem, vmem_limit_bytes=VMEM_LIMIT)


def _split3(x):
    hi = x.astype(BF16)
    r1 = x - hi.astype(F32)
    mid = r1.astype(BF16)
    lo = (r1 - mid.astype(F32)).astype(BF16)
    return hi, mid, lo


def _split2(x):
    hi = x.astype(BF16)
    lo = (x - hi.astype(F32)).astype(BF16)
    return hi, lo


def _dot(a, b):
    return jnp.dot(a, b, preferred_element_type=F32)


def _dot_nt(a, b):
    return lax.dot_general(a, b, (((1,), (1,)), ((), ())), preferred_element_type=F32)


def _dot_tn(a, b):
    return lax.dot_general(a, b, (((0,), (0,)), ((), ())), preferred_element_type=F32)


def _dot3(a, b):
    ah, al = _split2(a)
    bh, bl = _split2(b)
    return _dot(ah, bh) + (_dot(ah, bl) + _dot(al, bh))


def _dot3_nt(a, b):
    ah, al = _split2(a)
    bh, bl = _split2(b)
    return _dot_nt(ah, bh) + (_dot_nt(ah, bl) + _dot_nt(al, bh))


def _bdot(a, b):
    return lax.dot_general(a, b, (((2,), (1,)), ((0,), (0,))), preferred_element_type=F32)


def _bdot_nt(a, b):
    return lax.dot_general(a, b, (((2,), (2,)), ((0,), (0,))), preferred_element_type=F32)


def _bdot_tn(a, b):
    return lax.dot_general(a, b, (((1,), (1,)), ((0,), (0,))), preferred_element_type=F32)


def _bdot3(a, b):
    ah, al = _split2(a)
    bh, bl = _split2(b)
    return _bdot(ah, bh) + (_bdot(ah, bl) + _bdot(al, bh))


def _bdot3_nt(a, b):
    ah, al = _split2(a)
    bh, bl = _split2(b)
    return _bdot_nt(ah, bh) + (_bdot_nt(ah, bl) + _bdot_nt(al, bh))


def _exact_left(mask_bf16, x):
    hi, mid, lo = _split3(x)
    return _dot(mask_bf16, hi) + (_dot(mask_bf16, mid) + _dot(mask_bf16, lo))


def _exact_right(x, mask_bf16):
    hi, mid, lo = _split3(x)
    return _dot(hi, mask_bf16) + (_dot(mid, mask_bf16) + _dot(lo, mask_bf16))


def _softplus(x):
    return jnp.maximum(x, 0.0) + jnp.log1p(jnp.exp(-jnp.abs(x)))


def _log_sigmoid(x):
    return -_softplus(-x)


def _sigmoid(x):
    return 1.0 / (1.0 + jnp.exp(-x))


def _layer_norm(x, g, b):
    mu = jnp.mean(x, axis=-1, keepdims=True)
    xc = x - mu
    var = jnp.mean(xc * xc, axis=-1, keepdims=True)
    return xc * lax.rsqrt(var + LN_EPS) * g + b


def _iota2(shape, dim):
    return lax.broadcasted_iota(I32, shape, dim)


def _ln_kernel(x_ref, g_ref, b_ref, o_ref):
    o_ref[...] = _layer_norm(x_ref[...], g_ref[...], b_ref[...])


def _ln_call(x, g, b):
    T, D = x.shape
    tm = ROW_BLOCK
    return pl.pallas_call(
        _ln_kernel,
        grid=(T // tm,),
        in_specs=[pl.BlockSpec((tm, D), lambda i: (i, 0)),
                  pl.BlockSpec((1, D), lambda i: (0, 0)),
                  pl.BlockSpec((1, D), lambda i: (0, 0))],
        out_specs=pl.BlockSpec((tm, D), lambda i: (i, 0)),
        out_shape=jax.ShapeDtypeStruct((T, D), F32),
        compiler_params=_cparams(("parallel",)),
        name="embed_ln",
    )(x, g.reshape(1, D), b.reshape(1, D))


def _inproj_kernel(h_ref, w_ref, wgh_ref, wgl_ref, wgth_ref, wgtl_ref, u_ref, g_ref, gt_ref):
    h = h_ref[...]
    hh, hl = _split2(h)
    u_ref[...] = _dot(hh, w_ref[...])
    wgh, wgl = wgh_ref[...], wgl_ref[...]
    g_ref[...] = _dot(hh, wgh) + (_dot(hh, wgl) + _dot(hl, wgh))
    wgth, wgtl = wgth_ref[...], wgtl_ref[...]
    gt_ref[...] = _dot_nt(wgth, hh) + (_dot_nt(wgtl, hh) + _dot_nt(wgth, hl))


def _inproj_call(h, w_main, wg):
    T, D = h.shape
    N = w_main.shape[1]
    tm = ROW_BLOCK
    wgh, wgl = _split2(wg)
    wgt = wg.T
    wgth, wgtl = _split2(wgt)
    G = N_GATE_COLS
    const = lambda i: (0, 0)
    return pl.pallas_call(
        _inproj_kernel,
        grid=(T // tm,),
        in_specs=[pl.BlockSpec((tm, D), lambda i: (i, 0)),
                  pl.BlockSpec((D, N), const),
                  pl.BlockSpec((D, G), const), pl.BlockSpec((D, G), const),
                  pl.BlockSpec((G, D), const), pl.BlockSpec((G, D), const)],
        out_specs=[pl.BlockSpec((tm, N), lambda i: (i, 0)),
                   pl.BlockSpec((tm, G), lambda i: (i, 0)),
                   pl.BlockSpec((G, tm), lambda i: (0, i))],
        out_shape=[jax.ShapeDtypeStruct((T, N), F32),
                   jax.ShapeDtypeStruct((T, G), F32),
                   jax.ShapeDtypeStruct((G, T), F32)],
        compiler_params=_cparams(("parallel",)),
        name="in_proj",
    )(h, w_main, wgh, wgl, wgth, wgtl)


def _mlstm_kernel(q_ref, k_ref, v_ref, o_ref, g_ref, gt_ref, brow_ref, bcol_ref, nw_ref,
                  y_ref, c_sc, m_sc):
    L = q_ref.shape[0]
    d = M_HEAD_DIM

    @pl.when(pl.program_id(1) == 0)
    def _():
        c_sc[...] = jnp.zeros_like(c_sc)
        m_sc[...] = jnp.full_like(m_sc, -jnp.inf)

    row = _iota2((L, L), 0)
    col = _iota2((L, L), 1)
    causal = col <= row
    tril = causal.astype(BF16)
    triu = (row <= col).astype(BF16)

    gb = g_ref[...] + brow_ref[...]
    gtb = gt_ref[...] + bcol_ref[...]
    b_cols = _exact_left(tril, _log_sigmoid(gb))
    b_rows = _exact_right(_log_sigmoid(gtb), triu)

    ones_col = (_iota2((L, d), 1) == 0).astype(F32)
    H = range(M_HEADS)
    sl = [slice(hd * d, (hd + 1) * d) for hd in H]
    q = [q_ref[:, c].astype(BF16) for c in sl]
    kf = [k_ref[:, c] * (d ** -0.5) for c in sl]
    k = [x.astype(BF16) for x in kf]
    v_aug = [jnp.concatenate([v_ref[:, c], ones_col], axis=1).astype(BF16) for c in sl]
    bc = [b_cols[:, M_HEADS + hd:M_HEADS + hd + 1] for hd in H]
    lic = [gb[:, hd:hd + 1] for hd in H]
    src = [gtb[hd:hd + 1, :] - b_rows[M_HEADS + hd:M_HEADS + hd + 1, :] for hd in H]
    b_end = [x[L - 1:L, :] for x in bc]
    m_prev = [m_sc[hd] for hd in H]
    c_aug = [c_sc[hd] for hd in H]

    qk = [_dot_nt(q[h], k[h]) for h in H]
    qc = [_dot(q[h], c_aug[h].astype(BF16)) for h in H]
    log_w = [jnp.where(causal, bc[h] + src[h], -jnp.inf) for h in H]
    m_intra = [jnp.max(x, axis=1, keepdims=True) for x in log_w]
    carry_log = [bc[h] + m_prev[h] for h in H]
    m_t = [jnp.maximum(carry_log[h], m_intra[h]) for h in H]
    inter = [jnp.exp(carry_log[h] - m_t[h]) for h in H]
    s = [(qk[h] * jnp.exp(log_w[h] - m_t[h])).astype(BF16) for h in H]
    num_aug = [inter[h] * qc[h] + _dot(s[h], v_aug[h]) for h in H]
    hh = [num_aug[h][:, :d] / jnp.maximum(jnp.abs(num_aug[h][:, d:d + 1]), jnp.exp(-m_t[h]))
          for h in H]

    m_end = [jnp.max(b_end[h] + src[h], axis=1, keepdims=True) for h in H]
    m_new = [jnp.maximum(b_end[h] + m_prev[h], m_end[h]) for h in H]
    wk = [(jnp.exp(b_end[h] - bc[h] + lic[h] - m_new[h]) * kf[h]).astype(BF16) for h in H]
    kv = [_dot_tn(wk[h], v_aug[h]) for h in H]
    for h in H:
        c_sc[h] = jnp.exp(b_end[h] + m_prev[h] - m_new[h]) * c_aug[h] + kv[h]
        m_sc[h] = m_new[h]

    outs = [x * lax.rsqrt(jnp.mean(x * x, axis=1, keepdims=True) + NORM_EPS) for x in hh]
    hcat = jnp.concatenate(outs, axis=1) * nw_ref[...]
    y_ref[...] = _sigmoid(o_ref[...]) * hcat


def _mlstm_call(u, g, gt, i_bias, f_bias, norm_w, B, S):
    T = u.shape[0]
    L = min(M_CHUNK, S)
    nc = S // L
    W = M_WIDTH
    zeros = jnp.zeros((N_GATE_COLS - 2 * M_HEADS,), F32)
    bias = jnp.concatenate([i_bias, f_bias, zeros])
    G = N_GATE_COLS
    rowblk = lambda c: pl.BlockSpec((L, W), lambda b, j, c=c: (b * nc + j, c))
    const = lambda b, j: (0, 0)
    return pl.pallas_call(
        _mlstm_kernel,
        grid=(B, nc),
        in_specs=[rowblk(0), rowblk(1), rowblk(2), rowblk(3),
                  pl.BlockSpec((L, G), lambda b, j: (b * nc + j, 0)),
                  pl.BlockSpec((G, L), lambda b, j: (0, b * nc + j)),
                  pl.BlockSpec((1, G), const), pl.BlockSpec((G, 1), const),
                  pl.BlockSpec((1, W), const)],
        out_specs=pl.BlockSpec((L, W), lambda b, j: (b * nc + j, 0)),
        out_shape=jax.ShapeDtypeStruct((T, W), F32),
        scratch_shapes=[pltpu.VMEM((M_HEADS, M_HEAD_DIM, 2 * M_HEAD_DIM), F32),
                        pltpu.VMEM((M_HEADS, 1, 1), F32)],
        compiler_params=_cparams(("parallel", "arbitrary")),
        name="mlstm",
    )(u, u, u, u, g, gt, bias.reshape(1, G), bias.reshape(G, 1), norm_w.reshape(1, W))


def _sb_kernel(q_ref, k_ref, v_ref, nw_ref, y_ref, qb_sc, acc_sc, r_sc):
    tq = q_ref.shape[0]
    tk = tq
    d = SB_HEAD_DIM
    qi = pl.program_id(1)
    srow = _iota2((tq, tk), 0)
    scol = _iota2((tq, tk), 1)
    later = (scol < srow).astype(BF16)

    qb_sc[...] = (q_ref[...] * (d ** -0.5)).astype(BF16)
    acc_sc[...] = jnp.zeros_like(acc_sc)
    r_sc[...] = jnp.zeros_like(r_sc)

    def block(kb, diagonal):
        off = pl.multiple_of(kb * tk, tk)
        heads = range(SB_HEADS)
        cols = [slice(hd * d, (hd + 1) * d) for hd in heads]
        valid = scol < srow
        z = [_dot_nt(qb_sc[:, c], k_ref[pl.ds(off, tk), c].astype(BF16)) for c in cols]
        sp = [jnp.maximum(x, 0.0) + jnp.log(1.0 + jnp.exp(-jnp.abs(x))) for x in z]
        if diagonal:
            sp = [jnp.where(valid, x, 0.0) for x in sp]
        c2 = [_dot(jnp.concatenate(_split2(x), axis=0), later) for x in sp]
        cs = [x[:tq] + x[tq:] for x in c2]
        a = [jnp.exp(z[h] - sp[h] - cs[h]) for h in heads]
        if diagonal:
            a = [jnp.where(valid, x, 0.0) for x in a]
        pv = [_dot(a[h].astype(BF16), v_ref[pl.ds(off, tk), cols[h]].astype(BF16)) for h in heads]
        for h in heads:
            rest = r_sc[h]
            acc_sc[h] = acc_sc[h] + jnp.exp(-rest) * pv[h]
            r_sc[h] = rest + (cs[h][:, 0:1] + sp[h][:, 0:1])

    block(qi, True)

    def body(it, carry):
        block(qi - it, False)
        return carry

    lax.fori_loop(1, qi + 1, body, 0)
    outs = []
    for hd in range(SB_HEADS):
        o = acc_sc[hd]
        outs.append(o * lax.rsqrt(jnp.mean(o * o, axis=1, keepdims=True) + NORM_EPS))
    y_ref[...] = jnp.concatenate(outs, axis=1) * nw_ref[...]


def _sb_call(u, norm_w, B, S, col0):
    T = u.shape[0]
    tq = min(SB_BLOCK, S)
    nq = S // tq
    W = SB_WIDTH
    return pl.pallas_call(
        _sb_kernel,
        grid=(B, nq),
        in_specs=[pl.BlockSpec((tq, W), lambda b, i: (b * nq + i, col0)),
                  pl.BlockSpec((S, W), lambda b, i: (b, col0 + 1)),
                  pl.BlockSpec((S, W), lambda b, i: (b, col0 + 2)),
                  pl.BlockSpec((1, W), lambda b, i: (0, 0))],
        out_specs=pl.BlockSpec((tq, W), lambda b, i: (b * nq + i, 0)),
        out_shape=jax.ShapeDtypeStruct((T, W), F32),
        scratch_shapes=[pltpu.VMEM((tq, W), BF16),
                        pltpu.VMEM((SB_HEADS, tq, SB_HEAD_DIM), F32),
                        pltpu.VMEM((SB_HEADS, tq, 1), F32)],
        compiler_params=_cparams(("parallel", "arbitrary")),
        name="stickbreak",
    )(u, u, u, norm_w.reshape(1, W))


def _gdn_kernel(q_ref, k_ref, v_ref, z_ref, g_ref, gt_ref, cw_ref, arow_ref, acol_ref,
                drow_ref, dcol_ref, nw_ref, y_ref, xq_sc, xk_sc, xv_sc, s_sc):
    LB = q_ref.shape[0]
    C = G_CHUNK
    d = G_HEAD_DIM
    W = G_WIDTH
    HALO = 8

    @pl.when(pl.program_id(1) == 0)
    def _():
        s_sc[...] = jnp.zeros_like(s_sc)
        for sc in (xq_sc, xk_sc, xv_sc):
            sc[0:HALO, :] = jnp.zeros((HALO, W), F32)

    @pl.when(pl.program_id(1) != 0)
    def _():
        for sc in (xq_sc, xk_sc, xv_sc):
            sc[0:HALO, :] = sc[LB:LB + HALO, :]

    def conv_silu(x_ref, sc, j):
        sc[HALO:HALO + LB, :] = x_ref[...]
        acc = None
        for t in range(G_CONV):
            w = cw_ref[t:t + 1, j * W:(j + 1) * W]
            term = w * sc[HALO - (G_CONV - 1) + t:HALO - (G_CONV - 1) + t + LB, :]
            acc = term if acc is None else acc + term
        return acc * _sigmoid(acc)

    qc = conv_silu(q_ref, xq_sc, 0)
    kc = conv_silu(k_ref, xk_sc, 1)
    vc = conv_silu(v_ref, xv_sc, 2)

    gb = g_ref[...]
    gtb = gt_ref[...]
    gdec_cols = -jnp.exp(arow_ref[...]) * _softplus(gb + drow_ref[...])
    gdec_rows = -jnp.exp(acol_ref[...]) * _softplus(gtb + dcol_ref[...])
    beta_cols = _sigmoid(gb)
    row = _iota2((LB, LB), 0)
    col = _iota2((LB, LB), 1)
    shift = C.bit_length() - 1
    same = jnp.right_shift(row, shift) == jnp.right_shift(col, shift)
    tril = (same & (col <= row)).astype(BF16)
    triu = (same & (row <= col)).astype(BF16)
    gam_cols = _exact_left(tril, gdec_cols)
    gam_rows = _exact_right(gdec_rows, triu)

    nc = LB // C
    NB = G_HEADS * nc

    def per_head(x):
        return jnp.stack([x[:, h * d:(h + 1) * d] for h in range(G_HEADS)], axis=0).reshape(NB, C, d)

    q3 = per_head(qc)
    k3 = per_head(kc)
    v3 = per_head(vc)
    q3 = q3 * lax.rsqrt(jnp.sum(q3 * q3, axis=2, keepdims=True) + NORM_EPS) * (d ** -0.5)
    k3 = k3 * lax.rsqrt(jnp.sum(k3 * k3, axis=2, keepdims=True) + NORM_EPS)
    ga0 = 2 * G_HEADS
    gb0 = 3 * G_HEADS
    gam_c = jnp.stack([gam_cols[:, ga0 + h:ga0 + h + 1] for h in range(G_HEADS)], 0).reshape(NB, C, 1)
    beta = jnp.stack([beta_cols[:, gb0 + h:gb0 + h + 1] for h in range(G_HEADS)], 0).reshape(NB, C, 1)
    gam_r = jnp.stack([gam_rows[ga0 + h:ga0 + h + 1, c * C:(c + 1) * C]
                       for h in range(G_HEADS) for c in range(nc)], 0)
    gam_end = gam_c[:, C - 1:C, :]

    r64 = _iota2((1, C, C), 1)
    c64 = _iota2((1, C, C), 2)
    incl = c64 <= r64
    strict = c64 < r64

    decay = jnp.exp(jnp.where(incl, gam_c - gam_r, -jnp.inf))
    kk = _bdot3_nt(k3, k3)
    m = jnp.where(strict, beta * kk * decay, 0.0)
    tinv = jnp.broadcast_to((r64 == c64).astype(F32), (NB, C, C))
    s = 1
    while s < C:
        sh = s.bit_length() - 1
        same2s = jnp.right_shift(r64, sh + 1) == jnp.right_shift(c64, sh + 1)
        low_mask = same2s & ((jnp.right_shift(r64, sh) & 1) == 1) & ((jnp.right_shift(c64, sh) & 1) == 0)
        low = jnp.where(low_mask, m, 0.0).astype(BF16)
        tb = tinv.astype(BF16)
        tinv = tinv - _bdot(_bdot(tb, low).astype(BF16), tb)
        s *= 2
    eg = jnp.exp(gam_c)
    tb = tinv.astype(BF16)
    u = _bdot(tb, (v3 * beta).astype(BF16))
    w = _bdot(tb, (k3 * (beta * eg)).astype(BF16)).astype(BF16)
    qk = (_bdot_nt(q3.astype(BF16), k3.astype(BF16)) * decay).astype(BF16)
    q_dec = (q3 * eg).astype(BF16)
    k_dec = (k3 * jnp.exp(gam_end - gam_c)).astype(BF16)
    cdec = jnp.exp(gam_end)

    def chunk(x, c):
        return x.reshape((G_HEADS, nc) + x.shape[1:])[:, c]

    state = s_sc[...]
    o_chunks = []
    for c in range(nc):
        sb = state.astype(BF16)
        v_new = chunk(u, c) - _bdot(chunk(w, c), sb)
        vb = v_new.astype(BF16)
        o_chunks.append(_bdot(chunk(q_dec, c), sb) + _bdot(chunk(qk, c), vb))
        state = chunk(cdec, c) * state + _bdot_tn(chunk(k_dec, c), vb)
    s_sc[...] = state

    nw = nw_ref[...]
    o = jnp.concatenate(o_chunks, axis=1)
    o = o * lax.rsqrt(jnp.mean(o * o, axis=2, keepdims=True) + NORM_EPS) * nw
    zz = z_ref[...]
    y_ref[...] = jnp.concatenate([o[h] for h in range(G_HEADS)], axis=1) * (zz * _sigmoid(zz))


def _gdn_call(u, g, gt, conv_w, a_log, dt_bias, norm_w, B, S, col0):
    T = u.shape[0]
    LB = min(G_BLOCK, S)
    nb = S // LB
    W = G_WIDTH
    G = N_GATE_COLS
    pad = jnp.zeros((2 * G_HEADS,), F32)
    arow = jnp.concatenate([pad, a_log, jnp.zeros((G_HEADS,), F32)])
    drow = jnp.concatenate([pad, dt_bias, jnp.zeros((G_HEADS,), F32)])
    rowblk = lambda c: pl.BlockSpec((LB, W), lambda b, j, c=c: (b * nb + j, col0 + c))
    const = lambda b, j: (0, 0)
    return pl.pallas_call(
        _gdn_kernel,
        grid=(B, nb),
        in_specs=[rowblk(0), rowblk(1), rowblk(2), rowblk(3),
                  pl.BlockSpec((LB, G), lambda b, j: (b * nb + j, 0)),
                  pl.BlockSpec((G, LB), lambda b, j: (0, b * nb + j)),
                  pl.BlockSpec((G_CONV, 3 * W), const),
                  pl.BlockSpec((1, G), const), pl.BlockSpec((G, 1), const),
                  pl.BlockSpec((1, G), const), pl.BlockSpec((G, 1), const),
                  pl.BlockSpec((1, G_HEAD_DIM), const)],
        out_specs=pl.BlockSpec((LB, W), lambda b, j: (b * nb + j, 0)),
        out_shape=jax.ShapeDtypeStruct((T, W), F32),
        scratch_shapes=[pltpu.VMEM((LB + 8, W), F32), pltpu.VMEM((LB + 8, W), F32),
                        pltpu.VMEM((LB + 8, W), F32),
                        pltpu.VMEM((G_HEADS, G_HEAD_DIM, G_HEAD_DIM), F32)],
        compiler_params=_cparams(("parallel", "arbitrary")),
        name="gdn",
    )(u, u, u, u, g, gt, conv_w, arow.reshape(1, G), arow.reshape(G, 1),
      drow.reshape(1, G), drow.reshape(G, 1), norm_w.reshape(1, G_HEAD_DIM))


def _post_kernel(alpha, h_ref, ym_ref, ys_ref, yg_ref, wo_ref,
                 wrh_ref, wrl_ref, br_ref, g1_ref, b1_ref,
                 h1_ref, idx_ref, gate_ref, rank_ref, cnt_ref, cnt_sc):
    tm = h_ref.shape[0]
    E = br_ref.shape[1]

    @pl.when(pl.program_id(0) == 0)
    def _():
        cnt_sc[...] = jnp.zeros_like(cnt_sc)

    y = jnp.concatenate([ym_ref[...], ys_ref[...], yg_ref[...]], axis=1).astype(BF16)
    a = alpha * h_ref[...] + _dot(y, wo_ref[...])
    h1 = _layer_norm(a, g1_ref[...], b1_ref[...])
    h1h, h1l = _split2(h1)
    h1_ref[...] = h1

    wrh, wrl = wrh_ref[...], wrl_ref[...]
    logits = _dot(h1h, wrh) + (_dot(h1h, wrl) + _dot(h1l, wrh)) + br_ref[...]
    lane = _iota2((tm, E), 1).astype(F32)
    work = logits
    vals, idxs = [], []
    anyhot = jnp.zeros((tm, E), F32)
    for _ in range(TOP_K):
        mx = jnp.max(work, axis=1, keepdims=True)
        ix = jnp.min(jnp.where(work == mx, lane, float(E)), axis=1, keepdims=True)
        sel = lane == ix
        vals.append(mx)
        idxs.append(ix)
        anyhot = jnp.where(sel, 1.0, anyhot)
        work = jnp.where(sel, -jnp.inf, work)
    ex = [jnp.exp(v - vals[0]) for v in vals]
    tot = ex[0] + ex[1] + ex[2] + ex[3]
    gates = [e / tot for e in ex]

    r = _iota2((tm, tm), 0)
    c = _iota2((tm, tm), 1)
    before = (c < r).astype(BF16)
    pos = cnt_sc[...] + _dot(before, anyhot.astype(BF16))
    ranks = [jnp.sum(jnp.where(lane == ix, pos, 0.0), axis=1, keepdims=True) for ix in idxs]
    cnt_sc[...] = cnt_sc[...] + jnp.sum(anyhot, axis=0, keepdims=True)
    cnt_ref[...] = cnt_sc[...].astype(I32)

    kl = _iota2((tm, TOP_K), 1)

    def pack(cols):
        out = jnp.broadcast_to(cols[0], (tm, TOP_K))
        for k in range(1, TOP_K):
            out = jnp.where(kl == k, cols[k], out)
        return out

    idx_ref[...] = pack(idxs).astype(I32)
    gate_ref[...] = pack(gates)
    rank_ref[...] = pack(ranks).astype(I32)


def _post_call(alpha, h, ym, ys, yg, w_out, w_router, b_router, ln_g, ln_b):
    T, D = h.shape
    tm = POST_BLOCK
    E = w_router.shape[1]
    wrh, wrl = _split2(w_router)
    const = lambda i: (0, 0)
    rows = lambda w: pl.BlockSpec((tm, w), lambda i: (i, 0))
    return pl.pallas_call(
        functools.partial(_post_kernel, alpha),
        grid=(T // tm,),
        in_specs=[rows(D), rows(M_WIDTH), rows(SB_WIDTH), rows(G_WIDTH),
                  pl.BlockSpec(w_out.shape, const),
                  pl.BlockSpec((D, E), const), pl.BlockSpec((D, E), const),
                  pl.BlockSpec((1, E), const), pl.BlockSpec((1, D), const),
                  pl.BlockSpec((1, D), const)],
        out_specs=[rows(D), rows(TOP_K), rows(TOP_K), rows(TOP_K),
                   pl.BlockSpec((1, E), const)],
        out_shape=[jax.ShapeDtypeStruct((T, D), F32),
                   jax.ShapeDtypeStruct((T, TOP_K), I32), jax.ShapeDtypeStruct((T, TOP_K), F32),
                   jax.ShapeDtypeStruct((T, TOP_K), I32), jax.ShapeDtypeStruct((1, E), I32)],
        scratch_shapes=[pltpu.VMEM((1, E), F32)],
        compiler_params=_cparams(("arbitrary",)),
        name="post_mix",
    )(h, ym, ys, yg, w_out, wrh, wrl, b_router.reshape(1, E),
      ln_g.reshape(1, D), ln_b.reshape(1, D))


def _to_token_tiles(ref, x):
    for j in range(ref.shape[-2]):
        ref[:, j, :] = x[:, j * LANES:(j + 1) * LANES]


def _from_token_tiles(ref):
    return jnp.concatenate([ref[:, j, :] for j in range(ref.shape[-2])], axis=1)


def _dispatch_kernel(alpha, dest_ref, h1_ref, p_ref, wpg_ref, wpp_ref, xs_in_ref,
                     base_ref, xs_ref, xpk_sc, sem):
    del xs_in_ref
    tm, D = h1_ref.shape
    half = D // 2
    h1 = h1_ref[...]
    hb = h1.astype(BF16)
    bits = lax.bitcast_convert_type(hb.astype(F32), jnp.uint32)
    _to_token_tiles(xpk_sc, (bits[:, :half] >> 16) | (bits[:, half:] & jnp.uint32(0xFFFF0000)))

    pb = p_ref[...].astype(BF16)
    nch = D // IN_CHUNK
    rows_per = -(-tm // nch)
    for c in range(nch):
        cs = slice(c * IN_CHUNK, (c + 1) * IN_CHUNK)
        ple = _sigmoid(_dot(hb, wpg_ref[:, cs])) * _dot(pb, wpp_ref[:, cs])
        base_ref[:, cs] = alpha * h1[:, cs] + ple
        for r in range(c * rows_per, min((c + 1) * rows_per, tm)):
            for k in range(TOP_K):
                pltpu.make_async_copy(xpk_sc.at[r], xs_ref.at[dest_ref[r * TOP_K + k]],
                                      sem).start(priority=k % DMA_THREADS)
    for k in range(TOP_K):
        pltpu.make_async_copy(xpk_sc, xs_ref.at[pl.ds(0, tm)], sem).wait()


def _dispatch_call(alpha, dest_flat, h1, p, w_pg, w_pp, xs0):
    T, D = h1.shape
    tm = ROW_BLOCK
    P = p.shape[1]
    const = lambda i: (0, 0)
    rows = lambda w: pl.BlockSpec((tm, w), lambda i: (i, 0))
    base, xs = pl.pallas_call(
        functools.partial(_dispatch_kernel, alpha),
        grid=(T // tm,),
        in_specs=[pl.BlockSpec((tm * TOP_K,), lambda i: (i,), memory_space=pltpu.SMEM),
                  rows(D), rows(P), pl.BlockSpec((D, D), const), pl.BlockSpec((P, D), const),
                  pl.BlockSpec(memory_space=pl.ANY)],
        out_specs=[rows(D), pl.BlockSpec(memory_space=pl.ANY)],
        out_shape=[jax.ShapeDtypeStruct((T, D), F32),
                   jax.ShapeDtypeStruct(xs0.shape, jnp.uint32)],
        scratch_shapes=[pltpu.VMEM((tm,) + xs0.shape[1:], jnp.uint32), pltpu.SemaphoreType.DMA],
        input_output_aliases={5: 1},
        compiler_params=_cparams(("arbitrary",)),
        name="ple_dispatch",
    )(dest_flat, h1, p, w_pg, w_pp, xs0)
    return base, xs


def _moe_kernel(be_ref, nb_ref, x_ref, wgu_ref, bgu_ref, wd_ref, bd_ref, y_ref, wgu_sc, wd_sc):
    i = pl.program_id(0)
    F = wd_ref.shape[2]
    CH = 128

    new_expert = jnp.logical_or(i == 0, be_ref[i] != be_ref[jnp.maximum(i - 1, 0)])

    @pl.when(jnp.logical_and(new_expert, i < nb_ref[0]))
    def _():
        def cast_gu(r, c):
            rows = pl.ds(pl.multiple_of(r * CH, CH), CH)
            wgu_sc[rows, :] = wgu_ref[0, 0, rows, :].astype(BF16)
            return c

        def cast_d(r, c):
            rows = pl.ds(pl.multiple_of(r * CH, CH), CH)
            wd_sc[rows, :] = wd_ref[0, 0, rows, :].astype(BF16)
            return c

        lax.fori_loop(0, wgu_ref.shape[2] // CH, cast_gu, 0)
        lax.fori_loop(0, F // CH, cast_d, 0)

    @pl.when(i < nb_ref[0])
    def _():
        xw = _from_token_tiles(x_ref)
        lo = lax.bitcast_convert_type(xw << 16, F32)
        hi = lax.bitcast_convert_type(xw & jnp.uint32(0xFFFF0000), F32)
        xb = jnp.concatenate([lo, hi], axis=1).astype(BF16)
        h = _dot(xb, wgu_sc[...]) + bgu_ref[0, 0]
        gate = jnp.minimum(h[:, :F], SWIGLU_LIMIT)
        up = jnp.clip(h[:, F:], -SWIGLU_LIMIT, SWIGLU_LIMIT)
        act = (up + 1.0) * gate * _sigmoid(SWIGLU_ALPHA * gate)
        y_ref[...] = _dot(act.astype(BF16), wd_sc[...]) + bd_ref[0, 0]

    @pl.when(i >= nb_ref[0])
    def _():
        y_ref[...] = jnp.zeros_like(y_ref)


def _moe_call(layer, block_e, n_used, xs, w_gu, b_gu, w_down, b_down):
    P = xs.shape[0]
    L, E, D, F2 = w_gu.shape
    F = F2 // 2
    tm = MOE_BLOCK
    nblk = P // tm
    grid_spec = pltpu.PrefetchScalarGridSpec(
        num_scalar_prefetch=2,
        grid=(nblk,),
        in_specs=[pl.BlockSpec((tm,) + xs.shape[1:], lambda i, be, nb: (i, 0, 0)),
                  pl.BlockSpec((1, 1, D, F2), lambda i, be, nb: (layer, be[i], 0, 0)),
                  pl.BlockSpec((1, 1, 1, F2), lambda i, be, nb: (layer, be[i], 0, 0)),
                  pl.BlockSpec((1, 1, F, D), lambda i, be, nb: (layer, be[i], 0, 0)),
                  pl.BlockSpec((1, 1, 1, D), lambda i, be, nb: (layer, be[i], 0, 0))],
        out_specs=pl.BlockSpec((tm, D), lambda i, be, nb: (i, 0)),
        scratch_shapes=[pltpu.VMEM((D, F2), BF16), pltpu.VMEM((F, D), BF16)],
    )
    return pl.pallas_call(
        _moe_kernel,
        grid_spec=grid_spec,
        out_shape=jax.ShapeDtypeStruct((P, D), F32),
        compiler_params=_cparams(("arbitrary",)),
        name="moe_experts",
    )(block_e, n_used, xs, w_gu, b_gu.reshape(L, E, 1, F2), w_down, b_down.reshape(L, E, 1, D))


def _combine_kernel(dest_ref, dnext_ref, base_ref, gate_ref, y_ref, g_ref, b_ref, o_ref, ybuf, sems):
    tm = base_ref.shape[0]
    i = pl.program_id(0)
    n = pl.num_programs(0)
    slot = i % 2

    def gather(idx_ref, sl):
        def issue(r, carry):
            for k in range(TOP_K):
                pltpu.make_async_copy(y_ref.at[pl.ds(idx_ref[r * TOP_K + k], 1), :],
                                      ybuf.at[sl, k, pl.ds(r, 1), :],
                                      sems.at[sl]).start(priority=k % DMA_THREADS)
            return carry
        lax.fori_loop(0, tm, issue, 0, unroll=DMA_UNROLL)

    @pl.when(i == 0)
    def _():
        gather(dest_ref, 0)

    @pl.when(i + 1 < n)
    def _():
        gather(dnext_ref, 1 - slot)

    for k in range(TOP_K):
        pltpu.make_async_copy(y_ref.at[pl.ds(0, tm), :], ybuf.at[slot, k], sems.at[slot]).wait()
    gate = gate_ref[...]
    acc = base_ref[...]
    for k in range(TOP_K):
        acc = acc + gate[:, k:k + 1] * ybuf[slot, k]
    o_ref[...] = _layer_norm(acc, g_ref[...], b_ref[...])


def _combine_call(dest_flat, base, gates, y, ln_g, ln_b):
    T, D = base.shape
    tm = ROW_BLOCK
    rows = lambda w: pl.BlockSpec((tm, w), lambda i: (i, 0))
    const = lambda i: (0, 0)
    nblk = T // tm
    return pl.pallas_call(
        _combine_kernel,
        grid=(nblk,),
        in_specs=[pl.BlockSpec((tm * TOP_K,), lambda i: (i,), memory_space=pltpu.SMEM),
                  pl.BlockSpec((tm * TOP_K,), lambda i: (jnp.minimum(i + 1, nblk - 1),),
                               memory_space=pltpu.SMEM),
                  rows(D), rows(TOP_K), pl.BlockSpec(memory_space=pl.ANY),
                  pl.BlockSpec((1, D), const), pl.BlockSpec((1, D), const)],
        out_specs=rows(D),
        out_shape=jax.ShapeDtypeStruct((T, D), F32),
        scratch_shapes=[pltpu.VMEM((2, TOP_K, tm, D), F32), pltpu.SemaphoreType.DMA((2,))],
        compiler_params=_cparams(("arbitrary",)),
        name="combine_ln",
    )(dest_flat, dest_flat, base, gates, y, ln_g.reshape(1, D), ln_b.reshape(1, D))


def _combine_inproj_kernel(dest_ref, dnext_ref, base_ref, gate_ref, y_ref, g_ref, b_ref,
                           w_ref, wgh_ref, wgl_ref, wgth_ref, wgtl_ref,
                           h_ref, u_ref, go_ref, gto_ref, ybuf, sems):
    tm = base_ref.shape[0]
    N = w_ref.shape[1]
    i = pl.program_id(0)
    n = pl.num_programs(0)
    slot = i % 2
    other = 1 - slot

    def row_copy(idx_ref, r, k, sl):
        return pltpu.make_async_copy(y_ref.at[pl.ds(idx_ref[r * TOP_K + k], 1), :],
                                     ybuf.at[sl, k, pl.ds(r, 1), :], sems.at[sl])

    @pl.when(i == 0)
    def _():
        def issue(r, carry):
            for k in range(TOP_K):
                row_copy(dest_ref, r, k, 0).start(priority=k % DMA_THREADS)
            return carry
        lax.fori_loop(0, tm, issue, 0, unroll=DMA_UNROLL)

    def wait_slot(sl):
        for k in range(TOP_K):
            pltpu.make_async_copy(y_ref.at[pl.ds(0, tm), :], ybuf.at[sl, k], sems.at[sl]).wait()

    wait_slot(slot)
    gate = gate_ref[...]
    acc = base_ref[...]
    for k in range(TOP_K):
        acc = acc + gate[:, k:k + 1] * ybuf[slot, k]
    h = _layer_norm(acc, g_ref[...], b_ref[...])
    h_ref[...] = h
    hh, hl = _split2(h)

    nch = N // IN_CHUNK
    rows_per = -(-tm // nch)
    for c in range(nch):
        cs = slice(c * IN_CHUNK, (c + 1) * IN_CHUNK)
        u_ref[:, cs] = _dot(hh, w_ref[:, cs])
        for r in range(c * rows_per, min((c + 1) * rows_per, tm)):
            for k in range(TOP_K):
                row_copy(dnext_ref, r, k, other).start(priority=k % DMA_THREADS)
    wgh, wgl = wgh_ref[...], wgl_ref[...]
    go_ref[...] = _dot(hh, wgh) + (_dot(hh, wgl) + _dot(hl, wgh))
    wgth, wgtl = wgth_ref[...], wgtl_ref[...]
    gto_ref[...] = _dot_nt(wgth, hh) + (_dot_nt(wgtl, hh) + _dot_nt(wgth, hl))

    @pl.when(i == n - 1)
    def _():
        wait_slot(other)


def _combine_inproj_call(dest_flat, base, gates, y, ln_g, ln_b, w_main, wg):
    T, D = base.shape
    N = w_main.shape[1]
    tm = ROW_BLOCK
    G = N_GATE_COLS
    wgh, wgl = _split2(wg)
    wgth, wgtl = _split2(wg.T)
    rows = lambda w: pl.BlockSpec((tm, w), lambda i: (i, 0))
    const = lambda i: (0, 0)
    nblk = T // tm
    return pl.pallas_call(
        _combine_inproj_kernel,
        grid=(nblk,),
        in_specs=[pl.BlockSpec((tm * TOP_K,), lambda i: (i,), memory_space=pltpu.SMEM),
                  pl.BlockSpec((tm * TOP_K,), lambda i: (jnp.minimum(i + 1, nblk - 1),),
                               memory_space=pltpu.SMEM),
                  rows(D), rows(TOP_K), pl.BlockSpec(memory_space=pl.ANY),
                  pl.BlockSpec((1, D), const), pl.BlockSpec((1, D), const),
                  pl.BlockSpec((D, N), const),
                  pl.BlockSpec((D, G), const), pl.BlockSpec((D, G), const),
                  pl.BlockSpec((G, D), const), pl.BlockSpec((G, D), const)],
        out_specs=[rows(D), rows(N), rows(G), pl.BlockSpec((G, tm), lambda i: (0, i))],
        out_shape=[jax.ShapeDtypeStruct((T, D), F32), jax.ShapeDtypeStruct((T, N), F32),
                   jax.ShapeDtypeStruct((T, G), F32), jax.ShapeDtypeStruct((G, T), F32)],
        scratch_shapes=[pltpu.VMEM((2, TOP_K, tm, D), F32), pltpu.SemaphoreType.DMA((2,))],
        compiler_params=_cparams(("arbitrary",)),
        name="combine_in_proj",
    )(dest_flat, dest_flat, base, gates, y, ln_g.reshape(1, D), ln_b.reshape(1, D),
      w_main, wgh, wgl, wgth, wgtl)


def _moe_ffn(alpha, layer, h1, p_i, idx, rank, counts, w_pg, w_pp, w_gu, b_gu, w_down, b_down,
             xs_buf):
    T, D = h1.shape
    E = w_gu.shape[1]
    A = T * TOP_K
    counts = counts.reshape(E)
    padded = (counts + MOE_BLOCK - 1) // MOE_BLOCK * MOE_BLOCK
    pad_end = jnp.cumsum(padded)
    pad_start = pad_end - padded
    n_blocks = -(-A // MOE_BLOCK) + E
    P = n_blocks * MOE_BLOCK
    experts = jnp.arange(E, dtype=I32)
    dest = jnp.sum(jnp.where(idx[:, :, None] == experts, pad_start, 0), axis=-1) + rank
    dest_flat = dest.reshape(A).astype(I32)
    blk_start = jnp.arange(n_blocks, dtype=I32) * MOE_BLOCK
    block_e = jnp.minimum(jnp.sum((pad_end[None, :] <= blk_start[:, None]).astype(I32), axis=1),
                          E - 1).astype(I32)
    n_used = (pad_end[-1] // MOE_BLOCK).astype(I32).reshape(1)
    assert xs_buf.shape[0] == P
    base, xs = _dispatch_call(alpha, dest_flat, h1, p_i, w_pg, w_pp, xs_buf)
    y = _moe_call(layer, block_e, n_used, xs, w_gu, b_gu, w_down, b_down)
    return dest_flat, base, y, xs


def _layer(layer, h, u, g, gt, p_i, B, S, alpha, m_i_bias, m_f_bias, m_norm_w, sb_norm_w, g_conv_w,
           g_A_log, g_dt_bias, g_norm_w, w_out, ln1_g, ln1_b, w_router, b_router, w_gu, b_gu,
           w_down, b_down, w_pg, w_pp, xs_buf):
    ym = _mlstm_call(u, g, gt, m_i_bias, m_f_bias, m_norm_w, B, S)
    ys = _sb_call(u, sb_norm_w, B, S, (4 * M_WIDTH + 4 * G_WIDTH) // SB_WIDTH)
    yg = _gdn_call(u, g, gt, g_conv_w, g_A_log, g_dt_bias, g_norm_w, B, S, 4 * M_WIDTH // G_WIDTH)
    h1, idx, gates, rank, counts = _post_call(
        alpha, h, ym, ys, yg, w_out, w_router, b_router, ln1_g, ln1_b)
    dest_flat, base, y, xs = _moe_ffn(alpha, layer, h1, p_i, idx, rank, counts, w_pg, w_pp,
                                      w_gu, b_gu, w_down, b_down, xs_buf)
    return dest_flat, base, gates, y, xs


def kernel(x, p, ln0_g, ln0_b, w_in, m_i_bias, m_f_bias, m_norm_w, sb_norm_w, g_conv_w, g_A_log,
           g_dt_bias, g_norm_w, w_out, ln1_g, ln1_b, w_router, b_router, w_gu, b_gu, w_down,
           b_down, w_ple_gate, w_ple_proj, ln2_g, ln2_b):
    B, S, D = x.shape
    depth = w_in.shape[0]
    T = B * S
    alpha = (2 * depth) ** 0.25
    g0 = 4 * M_WIDTH
    g1 = g0 + 2 * M_HEADS
    s1 = g1 + 3 * SB_WIDTH
    g2 = s1 + 4 * G_WIDTH
    n_slots = (-(-T * TOP_K // MOE_BLOCK) + w_gu.shape[1]) * MOE_BLOCK
    xs_buf = jnp.zeros((n_slots, D // 2 // LANES, LANES), jnp.uint32)
    h = _ln_call(x.reshape(T, D), ln0_g, ln0_b)
    pending = None
    for i in range(depth):
        w = w_in[i]
        w_main = jnp.concatenate([w[:, :g0], w[:, s1:g2], w[:, g1:s1]], axis=1).astype(BF16)
        wg = jnp.concatenate([w[:, g0:g1], w[:, g2:]], axis=1)
        if pending is None:
            u, g, gt = _inproj_call(h, w_main, wg)
        else:
            dest_flat, base, gates, y = pending
            h, u, g, gt = _combine_inproj_call(dest_flat, base, gates, y, ln2_g[i - 1], ln2_b[i - 1],
                                               w_main, wg)
        dest_flat, base, gates, y, xs_buf = _layer(
            i, h, u, g, gt, p[i].reshape(T, -1), B, S, alpha, m_i_bias[i], m_f_bias[i],
            m_norm_w[i], sb_norm_w[i], g_conv_w[i], g_A_log[i], g_dt_bias[i], g_norm_w[i],
            w_out[i].astype(BF16), ln1_g[i], ln1_b[i], w_router[i], b_router[i],
            w_gu, b_gu, w_down, b_down,
            w_ple_gate[i].astype(BF16), w_ple_proj[i].astype(BF16), xs_buf)
        pending = (dest_flat, base, gates, y)
    dest_flat, base, gates, y = pending
    h = _combine_call(dest_flat, base, gates, y, ln2_g[depth - 1], ln2_b[depth - 1])
    return h.reshape(B, S, D)
```

```python
import functools
import math

import jax
import jax.numpy as jnp
from jax import lax
from jax.experimental import pallas as pl
from jax.experimental.pallas import tpu as pltpu

F32 = jnp.float32
BF16 = jnp.bfloat16
I32 = jnp.int32

M_HEADS, M_HEAD_DIM = 4, 64
SB_HEADS, SB_HEAD_DIM = 4, 64
G_HEADS, G_HEAD_DIM = 4, 128
G_CONV = 4
M_WIDTH = M_HEADS * M_HEAD_DIM
SB_WIDTH = SB_HEADS * SB_HEAD_DIM
G_WIDTH = G_HEADS * G_HEAD_DIM
N_GATE_COLS = 16
TOP_K = 4
SWIGLU_LIMIT = 7.0
SWIGLU_ALPHA = 1.702
LN_EPS = 1e-5
NORM_EPS = 1e-6
LANES = 128

ROW_BLOCK = 256
POST_BLOCK = 512
IN_CHUNK = 256
M_CHUNK = 256
SB_BLOCK = 256
G_CHUNK = 64
G_BLOCK = 256
MOE_BLOCK = 512
DMA_UNROLL = 8
DMA_THREADS = 2
VMEM_LIMIT = 56 * 1024 * 1024


def _cparams(sem):
    return pltpu.CompilerParams(dimension_semantics=sem, vmem_limit_bytes=VMEM_LIMIT)


def _split3(x):
    hi = x.astype(BF16)
    r1 = x - hi.astype(F32)
    mid = r1.astype(BF16)
    lo = (r1 - mid.astype(F32)).astype(BF16)
    return hi, mid, lo


def _split2(x):
    hi = x.astype(BF16)
    lo = (x - hi.astype(F32)).astype(BF16)
    return hi, lo


def _dot(a, b):
    return jnp.dot(a, b, preferred_element_type=F32)


def _dot_nt(a, b):
    return lax.dot_general(a, b, (((1,), (1,)), ((), ())), preferred_element_type=F32)


def _dot_tn(a, b):
    return lax.dot_general(a, b, (((0,), (0,)), ((), ())), preferred_element_type=F32)


def _dot3(a, b):
    ah, al = _split2(a)
    bh, bl = _split2(b)
    return _dot(ah, bh) + (_dot(ah, bl) + _dot(al, bh))


def _dot3_nt(a, b):
    ah, al = _split2(a)
    bh, bl = _split2(b)
    return _dot_nt(ah, bh) + (_dot_nt(ah, bl) + _dot_nt(al, bh))


def _bdot(a, b):
    return lax.dot_general(a, b, (((2,), (1,)), ((0,), (0,))), preferred_element_type=F32)


def _bdot_nt(a, b):
    return lax.dot_general(a, b, (((2,), (2,)), ((0,), (0,))), preferred_element_type=F32)


def _bdot_tn(a, b):
    return lax.dot_general(a, b, (((1,), (1,)), ((0,), (0,))), preferred_element_type=F32)


def _bdot3(a, b):
    ah, al = _split2(a)
    bh, bl = _split2(b)
    return _bdot(ah, bh) + (_bdot(ah, bl) + _bdot(al, bh))


def _bdot3_nt(a, b):
    ah, al = _split2(a)
    bh, bl = _split2(b)
    return _bdot_nt(ah, bh) + (_bdot_nt(ah, bl) + _bdot_nt(al, bh))


def _exact_left(mask_bf16, x):
    hi, mid, lo = _split3(x)
    return _dot(mask_bf16, hi) + (_dot(mask_bf16, mid) + _dot(mask_bf16, lo))


def _exact_right(x, mask_bf16):
    hi, mid, lo = _split3(x)
    return _dot(hi, mask_bf16) + (_dot(mid, mask_bf16) + _dot(lo, mask_bf16))


def _softplus(x):
    return jnp.maximum(x, 0.0) + jnp.log1p(jnp.exp(-jnp.abs(x)))


def _log_sigmoid(x):
    return -_softplus(-x)


def _sigmoid(x):
    return 1.0 / (1.0 + jnp.exp(-x))


def _layer_norm(x, g, b):
    mu = jnp.mean(x, axis=-1, keepdims=True)
    xc = x - mu
    var = jnp.mean(xc * xc, axis=-1, keepdims=True)
    return xc * lax.rsqrt(var + LN_EPS) * g + b


def _iota2(shape, dim):
    return lax.broadcasted_iota(I32, shape, dim)


def _ln_kernel(x_ref, g_ref, b_ref, o_ref):
    o_ref[...] = _layer_norm(x_ref[...], g_ref[...], b_ref[...])


def _ln_call(x, g, b):
    T, D = x.shape
    tm = ROW_BLOCK
    return pl.pallas_call(
        _ln_kernel,
        grid=(T // tm,),
        in_specs=[pl.BlockSpec((tm, D), lambda i: (i, 0)),
                  pl.BlockSpec((1, D), lambda i: (0, 0)),
                  pl.BlockSpec((1, D), lambda i: (0, 0))],
        out_specs=pl.BlockSpec((tm, D), lambda i: (i, 0)),
        out_shape=jax.ShapeDtypeStruct((T, D), F32),
        compiler_params=_cparams(("parallel",)),
        name="embed_ln",
    )(x, g.reshape(1, D), b.reshape(1, D))


def _inproj_kernel(h_ref, w_ref, wgh_ref, wgl_ref, wgth_ref, wgtl_ref, u_ref, g_ref, gt_ref):
    h = h_ref[...]
    hh, hl = _split2(h)
    u_ref[...] = _dot(hh, w_ref[...])
    wgh, wgl = wgh_ref[...], wgl_ref[...]
    g_ref[...] = _dot(hh, wgh) + (_dot(hh, wgl) + _dot(hl, wgh))
    wgth, wgtl = wgth_ref[...], wgtl_ref[...]
    gt_ref[...] = _dot_nt(wgth, hh) + (_dot_nt(wgtl, hh) + _dot_nt(wgth, hl))


def _inproj_call(h, w_main, wg):
    T, D = h.shape
    N = w_main.shape[1]
    tm = ROW_BLOCK
    wgh, wgl = _split2(wg)
    wgt = wg.T
    wgth, wgtl = _split2(wgt)
    G = N_GATE_COLS
    const = lambda i: (0, 0)
    return pl.pallas_call(
        _inproj_kernel,
        grid=(T // tm,),
        in_specs=[pl.BlockSpec((tm, D), lambda i: (i, 0)),
                  pl.BlockSpec((D, N), const),
                  pl.BlockSpec((D, G), const), pl.BlockSpec((D, G), const),
                  pl.BlockSpec((G, D), const), pl.BlockSpec((G, D), const)],
        out_specs=[pl.BlockSpec((tm, N), lambda i: (i, 0)),
                   pl.BlockSpec((tm, G), lambda i: (i, 0)),
                   pl.BlockSpec((G, tm), lambda i: (0, i))],
        out_shape=[jax.ShapeDtypeStruct((T, N), F32),
                   jax.ShapeDtypeStruct((T, G), F32),
                   jax.ShapeDtypeStruct((G, T), F32)],
        compiler_params=_cparams(("parallel",)),
        name="in_proj",
    )(h, w_main, wgh, wgl, wgth, wgtl)


def _mlstm_kernel(q_ref, k_ref, v_ref, o_ref, g_ref, gt_ref, brow_ref, bcol_ref, nw_ref,
                  y_ref, c_sc, m_sc):
    L = q_ref.shape[0]
    d = M_HEAD_DIM

    @pl.when(pl.program_id(1) == 0)
    def _():
        c_sc[...] = jnp.zeros_like(c_sc)
        m_sc[...] = jnp.full_like(m_sc, -jnp.inf)

    row = _iota2((L, L), 0)
    col = _iota2((L, L), 1)
    causal = col <= row
    tril = causal.astype(BF16)
    triu = (row <= col).astype(BF16)

    gb = g_ref[...] + brow_ref[...]
    gtb = gt_ref[...] + bcol_ref[...]
    b_cols = _exact_left(tril, _log_sigmoid(gb))
    b_rows = _exact_right(_log_sigmoid(gtb), triu)

    ones_col = (_iota2((L, d), 1) == 0).astype(F32)
    H = range(M_HEADS)
    sl = [slice(hd * d, (hd + 1) * d) for hd in H]
    q = [q_ref[:, c].astype(BF16) for c in sl]
    kf = [k_ref[:, c] * (d ** -0.5) for c in sl]
    k = [x.astype(BF16) for x in kf]
    v_aug = [jnp.concatenate([v_ref[:, c], ones_col], axis=1).astype(BF16) for c in sl]
    bc = [b_cols[:, M_HEADS + hd:M_HEADS + hd + 1] for hd in H]
    lic = [gb[:, hd:hd + 1] for hd in H]
    src = [gtb[hd:hd + 1, :] - b_rows[M_HEADS + hd:M_HEADS + hd + 1, :] for hd in H]
    b_end = [x[L - 1:L, :] for x in bc]
    m_prev = [m_sc[hd] for hd in H]
    c_aug = [c_sc[hd] for hd in H]

    qk = [_dot_nt(q[h], k[h]) for h in H]
    qc = [_dot(q[h], c_aug[h].astype(BF16)) for h in H]
    log_w = [jnp.where(causal, bc[h] + src[h], -jnp.inf) for h in H]
    m_intra = [jnp.max(x, axis=1, keepdims=True) for x in log_w]
    carry_log = [bc[h] + m_prev[h] for h in H]
    m_t = [jnp.maximum(carry_log[h], m_intra[h]) for h in H]
    inter = [jnp.exp(carry_log[h] - m_t[h]) for h in H]
    s = [(qk[h] * jnp.exp(log_w[h] - m_t[h])).astype(BF16) for h in H]
    num_aug = [inter[h] * qc[h] + _dot(s[h], v_aug[h]) for h in H]
    hh = [num_aug[h][:, :d] / jnp.maximum(jnp.abs(num_aug[h][:, d:d + 1]), jnp.exp(-m_t[h]))
          for h in H]

    m_end = [jnp.max(b_end[h] + src[h], axis=1, keepdims=True) for h in H]
    m_new = [jnp.maximum(b_end[h] + m_prev[h], m_end[h]) for h in H]
    wk = [(jnp.exp(b_end[h] - bc[h] + lic[h] - m_new[h]) * kf[h]).astype(BF16) for h in H]
    kv = [_dot_tn(wk[h], v_aug[h]) for h in H]
    for h in H:
        c_sc[h] = jnp.exp(b_end[h] + m_prev[h] - m_new[h]) * c_aug[h] + kv[h]
        m_sc[h] = m_new[h]

    outs = [x * lax.rsqrt(jnp.mean(x * x, axis=1, keepdims=True) + NORM_EPS) for x in hh]
    hcat = jnp.concatenate(outs, axis=1) * nw_ref[...]
    y_ref[...] = _sigmoid(o_ref[...]) * hcat


def _mlstm_call(u, g, gt, i_bias, f_bias, norm_w, B, S):
    T = u.shape[0]
    L = min(M_CHUNK, S)
    nc = S // L
    W = M_WIDTH
    zeros = jnp.zeros((N_GATE_COLS - 2 * M_HEADS,), F32)
    bias = jnp.concatenate([i_bias, f_bias, zeros])
    G = N_GATE_COLS
    rowblk = lambda c: pl.BlockSpec((L, W), lambda b, j, c=c: (b * nc + j, c))
    const = lambda b, j: (0, 0)
    return pl.pallas_call(
        _mlstm_kernel,
        grid=(B, nc),
        in_specs=[rowblk(0), rowblk(1), rowblk(2), rowblk(3),
                  pl.BlockSpec((L, G), lambda b, j: (b * nc + j, 0)),
                  pl.BlockSpec((G, L), lambda b, j: (0, b * nc + j)),
                  pl.BlockSpec((1, G), const), pl.BlockSpec((G, 1), const),
                  pl.BlockSpec((1, W), const)],
        out_specs=pl.BlockSpec((L, W), lambda b, j: (b * nc + j, 0)),
        out_shape=jax.ShapeDtypeStruct((T, W), F32),
        scratch_shapes=[pltpu.VMEM((M_HEADS, M_HEAD_DIM, 2 * M_HEAD_DIM), F32),
                        pltpu.VMEM((M_HEADS, 1, 1), F32)],
        compiler_params=_cparams(("parallel", "arbitrary")),
        name="mlstm",
    )(u, u, u, u, g, gt, bias.reshape(1, G), bias.reshape(G, 1), norm_w.reshape(1, W))


def _sb_kernel(q_ref, k_ref, v_ref, nw_ref, y_ref, qb_sc, acc_sc, r_sc):
    tq = q_ref.shape[0]
    tk = tq
    d = SB_HEAD_DIM
    qi = pl.program_id(1)
    srow = _iota2((tq, tk), 0)
    scol = _iota2((tq, tk), 1)
    later = (scol < srow).astype(BF16)

    qb_sc[...] = (q_ref[...] * (d ** -0.5)).astype(BF16)
    acc_sc[...] = jnp.zeros_like(acc_sc)
    r_sc[...] = jnp.zeros_like(r_sc)

    def block(kb, diagonal):
        off = pl.multiple_of(kb * tk, tk)
        heads = range(SB_HEADS)
        cols = [slice(hd * d, (hd + 1) * d) for hd in heads]
        valid = scol < srow
        z = [_dot_nt(qb_sc[:, c], k_ref[pl.ds(off, tk), c].astype(BF16)) for c in cols]
        sp = [jnp.maximum(x, 0.0) + jnp.log(1.0 + jnp.exp(-jnp.abs(x))) for x in z]
        if diagonal:
            sp = [jnp.where(valid, x, 0.0) for x in sp]
        c2 = [_dot(jnp.concatenate(_split2(x), axis=0), later) for x in sp]
        cs = [x[:tq] + x[tq:] for x in c2]
        a = [jnp.exp(z[h] - sp[h] - cs[h]) for h in heads]
        if diagonal:
            a = [jnp.where(valid, x, 0.0) for x in a]
        pv = [_dot(a[h].astype(BF16), v_ref[pl.ds(off, tk), cols[h]].astype(BF16)) for h in heads]
        for h in heads:
            rest = r_sc[h]
            acc_sc[h] = acc_sc[h] + jnp.exp(-rest) * pv[h]
            r_sc[h] = rest + (cs[h][:, 0:1] + sp[h][:, 0:1])

    block(qi, True)

    def body(it, carry):
        block(qi - it, False)
        return carry

    lax.fori_loop(1, qi + 1, body, 0)
    outs = []
    for hd in range(SB_HEADS):
        o = acc_sc[hd]
        outs.append(o * lax.rsqrt(jnp.mean(o * o, axis=1, keepdims=True) + NORM_EPS))
    y_ref[...] = jnp.concatenate(outs, axis=1) * nw_ref[...]


def _sb_call(u, norm_w, B, S, col0):
    T = u.shape[0]
    tq = min(SB_BLOCK, S)
    nq = S // tq
    W = SB_WIDTH
    return pl.pallas_call(
        _sb_kernel,
        grid=(B, nq),
        in_specs=[pl.BlockSpec((tq, W), lambda b, i: (b * nq + i, col0)),
                  pl.BlockSpec((S, W), lambda b, i: (b, col0 + 1)),
                  pl.BlockSpec((S, W), lambda b, i: (b, col0 + 2)),
                  pl.BlockSpec((1, W), lambda b, i: (0, 0))],
        out_specs=pl.BlockSpec((tq, W), lambda b, i: (b * nq + i, 0)),
        out_shape=jax.ShapeDtypeStruct((T, W), F32),
        scratch_shapes=[pltpu.VMEM((tq, W), BF16),
                        pltpu.VMEM((SB_HEADS, tq, SB_HEAD_DIM), F32),
                        pltpu.VMEM((SB_HEADS, tq, 1), F32)],
        compiler_params=_cparams(("parallel", "arbitrary")),
        name="stickbreak",
    )(u, u, u, norm_w.reshape(1, W))


def _gdn_kernel(q_ref, k_ref, v_ref, z_ref, g_ref, gt_ref, cw_ref, arow_ref, acol_ref,
                drow_ref, dcol_ref, nw_ref, y_ref, xq_sc, xk_sc, xv_sc, s_sc):
    LB = q_ref.shape[0]
    C = G_CHUNK
    d = G_HEAD_DIM
    W = G_WIDTH
    HALO = 8

    @pl.when(pl.program_id(1) == 0)
    def _():
        s_sc[...] = jnp.zeros_like(s_sc)
        for sc in (xq_sc, xk_sc, xv_sc):
            sc[0:HALO, :] = jnp.zeros((HALO, W), F32)

    @pl.when(pl.program_id(1) != 0)
    def _():
        for sc in (xq_sc, xk_sc, xv_sc):
            sc[0:HALO, :] = sc[LB:LB + HALO, :]

    def conv_silu(x_ref, sc, j):
        sc[HALO:HALO + LB, :] = x_ref[...]
        acc = None
        for t in range(G_CONV):
            w = cw_ref[t:t + 1, j * W:(j + 1) * W]
            term = w * sc[HALO - (G_CONV - 1) + t:HALO - (G_CONV - 1) + t + LB, :]
            acc = term if acc is None else acc + term
        return acc * _sigmoid(acc)

    qc = conv_silu(q_ref, xq_sc, 0)
    kc = conv_silu(k_ref, xk_sc, 1)
    vc = conv_silu(v_ref, xv_sc, 2)

    gb = g_ref[...]
    gtb = gt_ref[...]
    gdec_cols = -jnp.exp(arow_ref[...]) * _softplus(gb + drow_ref[...])
    gdec_rows = -jnp.exp(acol_ref[...]) * _softplus(gtb + dcol_ref[...])
    beta_cols = _sigmoid(gb)
    row = _iota2((LB, LB), 0)
    col = _iota2((LB, LB), 1)
    shift = C.bit_length() - 1
    same = jnp.right_shift(row, shift) == jnp.right_shift(col, shift)
    tril = (same & (col <= row)).astype(BF16)
    triu = (same & (row <= col)).astype(BF16)
    gam_cols = _exact_left(tril, gdec_cols)
    gam_rows = _exact_right(gdec_rows, triu)

    nc = LB // C
    NB = G_HEADS * nc

    def per_head(x):
        return jnp.stack([x[:, h * d:(h + 1) * d] for h in range(G_HEADS)], axis=0).reshape(NB, C, d)

    q3 = per_head(qc)
    k3 = per_head(kc)
    v3 = per_head(vc)
    q3 = q3 * lax.rsqrt(jnp.sum(q3 * q3, axis=2, keepdims=True) + NORM_EPS) * (d ** -0.5)
    k3 = k3 * lax.rsqrt(jnp.sum(k3 * k3, axis=2, keepdims=True) + NORM_EPS)
    ga0 = 2 * G_HEADS
    gb0 = 3 * G_HEADS
    gam_c = jnp.stack([gam_cols[:, ga0 + h:ga0 + h + 1] for h in range(G_HEADS)], 0).reshape(NB, C, 1)
    beta = jnp.stack([beta_cols[:, gb0 + h:gb0 + h + 1] for h in range(G_HEADS)], 0).reshape(NB, C, 1)
    gam_r = jnp.stack([gam_rows[ga0 + h:ga0 + h + 1, c * C:(c + 1) * C]
                       for h in range(G_HEADS) for c in range(nc)], 0)
    gam_end = gam_c[:, C - 1:C, :]

    r64 = _iota2((1, C, C), 1)
    c64 = _iota2((1, C, C), 2)
    incl = c64 <= r64
    strict = c64 < r64

    decay = jnp.exp(jnp.where(incl, gam_c - gam_r, -jnp.inf))
    kk = _bdot3_nt(k3, k3)
    m = jnp.where(strict, beta * kk * decay, 0.0)
    tinv = jnp.broadcast_to((r64 == c64).astype(F32), (NB, C, C))
    s = 1
    while s < C:
        sh = s.bit_length() - 1
        same2s = jnp.right_shift(r64, sh + 1) == jnp.right_shift(c64, sh + 1)
        low_mask = same2s & ((jnp.right_shift(r64, sh) & 1) == 1) & ((jnp.right_shift(c64, sh) & 1) == 0)
        low = jnp.where(low_mask, m, 0.0).astype(BF16)
        tb = tinv.astype(BF16)
        tinv = tinv - _bdot(_bdot(tb, low).astype(BF16), tb)
        s *= 2
    eg = jnp.exp(gam_c)
    tb = tinv.astype(BF16)
    u = _bdot(tb, (v3 * beta).astype(BF16))
    w = _bdot(tb, (k3 * (beta * eg)).astype(BF16)).astype(BF16)
    qk = (_bdot_nt(q3.astype(BF16), k3.astype(BF16)) * decay).astype(BF16)
    q_dec = (q3 * eg).astype(BF16)
    k_dec = (k3 * jnp.exp(gam_end - gam_c)).astype(BF16)
    cdec = jnp.exp(gam_end)

    def chunk(x, c):
        return x.reshape((G_HEADS, nc) + x.shape[1:])[:, c]

    state = s_sc[...]
    o_chunks = []
    for c in range(nc):
        sb = state.astype(BF16)
        v_new = chunk(u, c) - _bdot(chunk(w, c), sb)
        vb = v_new.astype(BF16)
        o_chunks.append(_bdot(chunk(q_dec, c), sb) + _bdot(chunk(qk, c), vb))
        state = chunk(cdec, c) * state + _bdot_tn(chunk(k_dec, c), vb)
    s_sc[...] = state

    nw = nw_ref[...]
    o = jnp.concatenate(o_chunks, axis=1)
    o = o * lax.rsqrt(jnp.mean(o * o, axis=2, keepdims=True) + NORM_EPS) * nw
    zz = z_ref[...]
    y_ref[...] = jnp.concatenate([o[h] for h in range(G_HEADS)], axis=1) * (zz * _sigmoid(zz))


def _gdn_call(u, g, gt, conv_w, a_log, dt_bias, norm_w, B, S, col0):
    T = u.shape[0]
    LB = min(G_BLOCK, S)
    nb = S // LB
    W = G_WIDTH
    G = N_GATE_COLS
    pad = jnp.zeros((2 * G_HEADS,), F32)
    arow = jnp.concatenate([pad, a_log, jnp.zeros((G_HEADS,), F32)])
    drow = jnp.concatenate([pad, dt_bias, jnp.zeros((G_HEADS,), F32)])
    rowblk = lambda c: pl.BlockSpec((LB, W), lambda b, j, c=c: (b * nb + j, col0 + c))
    const = lambda b, j: (0, 0)
    return pl.pallas_call(
        _gdn_kernel,
        grid=(B, nb),
        in_specs=[rowblk(0), rowblk(1), rowblk(2), rowblk(3),
                  pl.BlockSpec((LB, G), lambda b, j: (b * nb + j, 0)),
                  pl.BlockSpec((G, LB), lambda b, j: (0, b * nb + j)),
                  pl.BlockSpec((G_CONV, 3 * W), const),
                  pl.BlockSpec((1, G), const), pl.BlockSpec((G, 1), const),
                  pl.BlockSpec((1, G), const), pl.BlockSpec((G, 1), const),
                  pl.BlockSpec((1, G_HEAD_DIM), const)],
        out_specs=pl.BlockSpec((LB, W), lambda b, j: (b * nb + j, 0)),
        out_shape=jax.ShapeDtypeStruct((T, W), F32),
        scratch_shapes=[pltpu.VMEM((LB + 8, W), F32), pltpu.VMEM((LB + 8, W), F32),
                        pltpu.VMEM((LB + 8, W), F32),
                        pltpu.VMEM((G_HEADS, G_HEAD_DIM, G_HEAD_DIM), F32)],
        compiler_params=_cparams(("parallel", "arbitrary")),
        name="gdn",
    )(u, u, u, u, g, gt, conv_w, arow.reshape(1, G), arow.reshape(G, 1),
      drow.reshape(1, G), drow.reshape(G, 1), norm_w.reshape(1, G_HEAD_DIM))


def _post_kernel(alpha, h_ref, ym_ref, ys_ref, yg_ref, wo_ref,
                 wrh_ref, wrl_ref, br_ref, g1_ref, b1_ref,
                 h1_ref, idx_ref, gate_ref, rank_ref, cnt_ref, cnt_sc):
    tm = h_ref.shape[0]
    E = br_ref.shape[1]

    @pl.when(pl.program_id(0) == 0)
    def _():
        cnt_sc[...] = jnp.zeros_like(cnt_sc)

    y = jnp.concatenate([ym_ref[...], ys_ref[...], yg_ref[...]], axis=1).astype(BF16)
    a = alpha * h_ref[...] + _dot(y, wo_ref[...])
    h1 = _layer_norm(a, g1_ref[...], b1_ref[...])
    h1h, h1l = _split2(h1)
    h1_ref[...] = h1

    wrh, wrl = wrh_ref[...], wrl_ref[...]
    logits = _dot(h1h, wrh) + (_dot(h1h, wrl) + _dot(h1l, wrh)) + br_ref[...]
    lane = _iota2((tm, E), 1).astype(F32)
    work = logits
    vals, idxs = [], []
    anyhot = jnp.zeros((tm, E), F32)
    for _ in range(TOP_K):
        mx = jnp.max(work, axis=1, keepdims=True)
        ix = jnp.min(jnp.where(work == mx, lane, float(E)), axis=1, keepdims=True)
        sel = lane == ix
        vals.append(mx)
        idxs.append(ix)
        anyhot = jnp.where(sel, 1.0, anyhot)
        work = jnp.where(sel, -jnp.inf, work)
    ex = [jnp.exp(v - vals[0]) for v in vals]
    tot = ex[0] + ex[1] + ex[2] + ex[3]
    gates = [e / tot for e in ex]

    r = _iota2((tm, tm), 0)
    c = _iota2((tm, tm), 1)
    before = (c < r).astype(BF16)
    pos = cnt_sc[...] + _dot(before, anyhot.astype(BF16))
    ranks = [jnp.sum(jnp.where(lane == ix, pos, 0.0), axis=1, keepdims=True) for ix in idxs]
    cnt_sc[...] = cnt_sc[...] + jnp.sum(anyhot, axis=0, keepdims=True)
    cnt_ref[...] = cnt_sc[...].astype(I32)

    kl = _iota2((tm, TOP_K), 1)

    def pack(cols):
        out = jnp.broadcast_to(cols[0], (tm, TOP_K))
        for k in range(1, TOP_K):
            out = jnp.where(kl == k, cols[k], out)
        return out

    idx_ref[...] = pack(idxs).astype(I32)
    gate_ref[...] = pack(gates)
    rank_ref[...] = pack(ranks).astype(I32)


def _post_call(alpha, h, ym, ys, yg, w_out, w_router, b_router, ln_g, ln_b):
    T, D = h.shape
    tm = POST_BLOCK
    E = w_router.shape[1]
    wrh, wrl = _split2(w_router)
    const = lambda i: (0, 0)
    rows = lambda w: pl.BlockSpec((tm, w), lambda i: (i, 0))
    return pl.pallas_call(
        functools.partial(_post_kernel, alpha),
        grid=(T // tm,),
        in_specs=[rows(D), rows(M_WIDTH), rows(SB_WIDTH), rows(G_WIDTH),
                  pl.BlockSpec(w_out.shape, const),
                  pl.BlockSpec((D, E), const), pl.BlockSpec((D, E), const),
                  pl.BlockSpec((1, E), const), pl.BlockSpec((1, D), const),
                  pl.BlockSpec((1, D), const)],
        out_specs=[rows(D), rows(TOP_K), rows(TOP_K), rows(TOP_K),
                   pl.BlockSpec((1, E), const)],
        out_shape=[jax.ShapeDtypeStruct((T, D), F32),
                   jax.ShapeDtypeStruct((T, TOP_K), I32), jax.ShapeDtypeStruct((T, TOP_K), F32),
                   jax.ShapeDtypeStruct((T, TOP_K), I32), jax.ShapeDtypeStruct((1, E), I32)],
        scratch_shapes=[pltpu.VMEM((1, E), F32)],
        compiler_params=_cparams(("arbitrary",)),
        name="post_mix",
    )(h, ym, ys, yg, w_out, wrh, wrl, b_router.reshape(1, E),
      ln_g.reshape(1, D), ln_b.reshape(1, D))


def _pack_bf16_pairs(x):
    half = x.shape[1] // 2
    bits = lax.bitcast_convert_type(x.astype(BF16).astype(F32), jnp.uint32)
    return (bits[:, :half] >> 16) | (bits[:, half:] & jnp.uint32(0xFFFF0000))


def _unpack_bf16_pairs(w):
    lo = lax.bitcast_convert_type(w << 16, F32)
    hi = lax.bitcast_convert_type(w & jnp.uint32(0xFFFF0000), F32)
    return lo, hi


def _to_token_tiles(ref, x):
    for j in range(ref.shape[-2]):
        ref[:, j, :] = x[:, j * LANES:(j + 1) * LANES]


def _from_token_tiles(ref):
    return jnp.concatenate([ref[:, j, :] for j in range(ref.shape[-2])], axis=1)


def _dispatch_kernel(alpha, dest_ref, h1_ref, p_ref, wpg_ref, wpp_ref, xs_in_ref,
                     base_ref, xs_ref, xpk_sc, sem):
    del xs_in_ref
    tm, D = h1_ref.shape
    half = D // 2
    h1 = h1_ref[...]
    hb = h1.astype(BF16)
    _to_token_tiles(xpk_sc, _pack_bf16_pairs(h1))

    pb = p_ref[...].astype(BF16)
    nch = D // IN_CHUNK
    rows_per = -(-tm // nch)
    for c in range(nch):
        cs = slice(c * IN_CHUNK, (c + 1) * IN_CHUNK)
        ple = _sigmoid(_dot(hb, wpg_ref[:, cs])) * _dot(pb, wpp_ref[:, cs])
        base_ref[:, cs] = alpha * h1[:, cs] + ple
        for r in range(c * rows_per, min((c + 1) * rows_per, tm)):
            for k in range(TOP_K):
                pltpu.make_async_copy(xpk_sc.at[r], xs_ref.at[dest_ref[r * TOP_K + k]],
                                      sem).start(priority=k % DMA_THREADS)
    for k in range(TOP_K):
        pltpu.make_async_copy(xpk_sc, xs_ref.at[pl.ds(0, tm)], sem).wait()


def _dispatch_call(alpha, dest_flat, h1, p, w_pg, w_pp, xs0):
    T, D = h1.shape
    tm = ROW_BLOCK
    P = p.shape[1]
    const = lambda i: (0, 0)
    rows = lambda w: pl.BlockSpec((tm, w), lambda i: (i, 0))
    base, xs = pl.pallas_call(
        functools.partial(_dispatch_kernel, alpha),
        grid=(T // tm,),
        in_specs=[pl.BlockSpec((tm * TOP_K,), lambda i: (i,), memory_space=pltpu.SMEM),
                  rows(D), rows(P), pl.BlockSpec((D, D), const), pl.BlockSpec((P, D), const),
                  pl.BlockSpec(memory_space=pl.ANY)],
        out_specs=[rows(D), pl.BlockSpec(memory_space=pl.ANY)],
        out_shape=[jax.ShapeDtypeStruct((T, D), F32),
                   jax.ShapeDtypeStruct(xs0.shape, jnp.uint32)],
        scratch_shapes=[pltpu.VMEM((tm,) + xs0.shape[1:], jnp.uint32), pltpu.SemaphoreType.DMA],
        input_output_aliases={5: 1},
        compiler_params=_cparams(("arbitrary",)),
        name="ple_dispatch",
    )(dest_flat, h1, p, w_pg, w_pp, xs0)
    return base, xs


def _moe_kernel(be_ref, nb_ref, x_ref, wgu_ref, bgu_ref, wd_ref, bd_ref, y_ref, wgu_sc, wd_sc):
    i = pl.program_id(0)
    F = wd_ref.shape[2]
    CH = 128

    new_expert = jnp.logical_or(i == 0, be_ref[i] != be_ref[jnp.maximum(i - 1, 0)])

    @pl.when(jnp.logical_and(new_expert, i < nb_ref[0]))
    def _():
        def cast_gu(r, c):
            rows = pl.ds(pl.multiple_of(r * CH, CH), CH)
            wgu_sc[rows, :] = wgu_ref[0, 0, rows, :].astype(BF16)
            return c

        def cast_d(r, c):
            rows = pl.ds(pl.multiple_of(r * CH, CH), CH)
            wd_sc[rows, :] = wd_ref[0, 0, rows, :].astype(BF16)
            return c

        lax.fori_loop(0, wgu_ref.shape[2] // CH, cast_gu, 0)
        lax.fori_loop(0, F // CH, cast_d, 0)

    @pl.when(i < nb_ref[0])
    def _():
        xw = _from_token_tiles(x_ref)
        xb = jnp.concatenate(_unpack_bf16_pairs(xw), axis=1).astype(BF16)
        h = _dot(xb, wgu_sc[...]) + bgu_ref[0, 0]
        gate = jnp.minimum(h[:, :F], SWIGLU_LIMIT)
        up = jnp.clip(h[:, F:], -SWIGLU_LIMIT, SWIGLU_LIMIT)
        act = (up + 1.0) * gate * _sigmoid(SWIGLU_ALPHA * gate)
        y_ref[...] = _pack_bf16_pairs(_dot(act.astype(BF16), wd_sc[...]) + bd_ref[0, 0])

    @pl.when(i >= nb_ref[0])
    def _():
        y_ref[...] = jnp.zeros_like(y_ref)


def _moe_call(layer, block_e, n_used, xs, w_gu, b_gu, w_down, b_down):
    P = xs.shape[0]
    L, E, D, F2 = w_gu.shape
    F = F2 // 2
    tm = MOE_BLOCK
    nblk = P // tm
    grid_spec = pltpu.PrefetchScalarGridSpec(
        num_scalar_prefetch=2,
        grid=(nblk,),
        in_specs=[pl.BlockSpec((tm,) + xs.shape[1:], lambda i, be, nb: (i, 0, 0)),
                  pl.BlockSpec((1, 1, D, F2), lambda i, be, nb: (layer, be[i], 0, 0)),
                  pl.BlockSpec((1, 1, 1, F2), lambda i, be, nb: (layer, be[i], 0, 0)),
                  pl.BlockSpec((1, 1, F, D), lambda i, be, nb: (layer, be[i], 0, 0)),
                  pl.BlockSpec((1, 1, 1, D), lambda i, be, nb: (layer, be[i], 0, 0))],
        out_specs=pl.BlockSpec((tm, D // 2), lambda i, be, nb: (i, 0)),
        scratch_shapes=[pltpu.VMEM((D, F2), BF16), pltpu.VMEM((F, D), BF16)],
    )
    return pl.pallas_call(
        _moe_kernel,
        grid_spec=grid_spec,
        out_shape=jax.ShapeDtypeStruct((P, D // 2), jnp.uint32),
        compiler_params=_cparams(("arbitrary",)),
        name="moe_experts",
    )(block_e, n_used, xs, w_gu, b_gu.reshape(L, E, 1, F2), w_down, b_down.reshape(L, E, 1, D))


def _combine_kernel(dest_ref, dnext_ref, base_ref, gate_ref, y_ref, g_ref, b_ref, o_ref, ybuf, sems):
    tm = base_ref.shape[0]
    i = pl.program_id(0)
    n = pl.num_programs(0)
    slot = i % 2

    def gather(idx_ref, sl):
        def issue(r, carry):
            for k in range(TOP_K):
                pltpu.make_async_copy(y_ref.at[pl.ds(idx_ref[r * TOP_K + k], 1), :],
                                      ybuf.at[sl, k, pl.ds(r, 1), :],
                                      sems.at[sl]).start(priority=k % DMA_THREADS)
            return carry
        lax.fori_loop(0, tm, issue, 0, unroll=DMA_UNROLL)

    @pl.when(i == 0)
    def _():
        gather(dest_ref, 0)

    @pl.when(i + 1 < n)
    def _():
        gather(dnext_ref, 1 - slot)

    for k in range(TOP_K):
        pltpu.make_async_copy(y_ref.at[pl.ds(0, tm), :], ybuf.at[slot, k], sems.at[slot]).wait()
    gate = gate_ref[...]
    ffn_lo, ffn_hi = None, None
    for k in range(TOP_K):
        lo, hi = _unpack_bf16_pairs(ybuf[slot, k])
        gk = gate[:, k:k + 1]
        ffn_lo = gk * lo if ffn_lo is None else ffn_lo + gk * lo
        ffn_hi = gk * hi if ffn_hi is None else ffn_hi + gk * hi
    acc = base_ref[...] + jnp.concatenate([ffn_lo, ffn_hi], axis=1)
    o_ref[...] = _layer_norm(acc, g_ref[...], b_ref[...])


def _combine_call(dest_flat, base, gates, y, ln_g, ln_b):
    T, D = base.shape
    tm = ROW_BLOCK
    rows = lambda w: pl.BlockSpec((tm, w), lambda i: (i, 0))
    const = lambda i: (0, 0)
    nblk = T // tm
    return pl.pallas_call(
        _combine_kernel,
        grid=(nblk,),
        in_specs=[pl.BlockSpec((tm * TOP_K,), lambda i: (i,), memory_space=pltpu.SMEM),
                  pl.BlockSpec((tm * TOP_K,), lambda i: (jnp.minimum(i + 1, nblk - 1),),
                               memory_space=pltpu.SMEM),
                  rows(D), rows(TOP_K), pl.BlockSpec(memory_space=pl.ANY),
                  pl.BlockSpec((1, D), const), pl.BlockSpec((1, D), const)],
        out_specs=rows(D),
        out_shape=jax.ShapeDtypeStruct((T, D), F32),
        scratch_shapes=[pltpu.VMEM((2, TOP_K, tm, D // 2), jnp.uint32),
                        pltpu.SemaphoreType.DMA((2,))],
        compiler_params=_cparams(("arbitrary",)),
        name="combine_ln",
    )(dest_flat, dest_flat, base, gates, y, ln_g.reshape(1, D), ln_b.reshape(1, D))


def _combine_inproj_kernel(dest_ref, dnext_ref, base_ref, gate_ref, y_ref, g_ref, b_ref,
                           w_ref, wgh_ref, wgl_ref, wgth_ref, wgtl_ref,
                           h_ref, u_ref, go_ref, gto_ref, ybuf, sems):
    tm = base_ref.shape[0]
    N = w_ref.shape[1]
    i = pl.program_id(0)
    n = pl.num_programs(0)
    slot = i % 2
    other = 1 - slot

    def row_copy(idx_ref, r, k, sl):
        return pltpu.make_async_copy(y_ref.at[pl.ds(idx_ref[r * TOP_K + k], 1), :],
                                     ybuf.at[sl, k, pl.ds(r, 1), :], sems.at[sl])

    @pl.when(i == 0)
    def _():
        def issue(r, carry):
            for k in range(TOP_K):
                row_copy(dest_ref, r, k, 0).start(priority=k % DMA_THREADS)
            return carry
        lax.fori_loop(0, tm, issue, 0, unroll=DMA_UNROLL)

    def wait_slot(sl):
        for k in range(TOP_K):
            pltpu.make_async_copy(y_ref.at[pl.ds(0, tm), :], ybuf.at[sl, k], sems.at[sl]).wait()

    wait_slot(slot)
    gate = gate_ref[...]
    ffn_lo, ffn_hi = None, None
    for k in range(TOP_K):
        lo, hi = _unpack_bf16_pairs(ybuf[slot, k])
        gk = gate[:, k:k + 1]
        ffn_lo = gk * lo if ffn_lo is None else ffn_lo + gk * lo
        ffn_hi = gk * hi if ffn_hi is None else ffn_hi + gk * hi
    acc = base_ref[...] + jnp.concatenate([ffn_lo, ffn_hi], axis=1)
    h = _layer_norm(acc, g_ref[...], b_ref[...])
    h_ref[...] = h
    hh, hl = _split2(h)

    nch = N // IN_CHUNK
    rows_per = -(-tm // nch)
    for c in range(nch):
        cs = slice(c * IN_CHUNK, (c + 1) * IN_CHUNK)
        u_ref[:, cs] = _dot(hh, w_ref[:, cs])
        for r in range(c * rows_per, min((c + 1) * rows_per, tm)):
            for k in range(TOP_K):
                row_copy(dnext_ref, r, k, other).start(priority=k % DMA_THREADS)
    wgh, wgl = wgh_ref[...], wgl_ref[...]
    go_ref[...] = _dot(hh, wgh) + (_dot(hh, wgl) + _dot(hl, wgh))
    wgth, wgtl = wgth_ref[...], wgtl_ref[...]
    gto_ref[...] = _dot_nt(wgth, hh) + (_dot_nt(wgtl, hh) + _dot_nt(wgth, hl))

    @pl.when(i == n - 1)
    def _():
        wait_slot(other)


def _combine_inproj_call(dest_flat, base, gates, y, ln_g, ln_b, w_main, wg):
    T, D = base.shape
    N = w_main.shape[1]
    tm = ROW_BLOCK
    G = N_GATE_COLS
    wgh, wgl = _split2(wg)
    wgth, wgtl = _split2(wg.T)
    rows = lambda w: pl.BlockSpec((tm, w), lambda i: (i, 0))
    const = lambda i: (0, 0)
    nblk = T // tm
    return pl.pallas_call(
        _combine_inproj_kernel,
        grid=(nblk,),
        in_specs=[pl.BlockSpec((tm * TOP_K,), lambda i: (i,), memory_space=pltpu.SMEM),
                  pl.BlockSpec((tm * TOP_K,), lambda i: (jnp.minimum(i + 1, nblk - 1),),
                               memory_space=pltpu.SMEM),
                  rows(D), rows(TOP_K), pl.BlockSpec(memory_space=pl.ANY),
                  pl.BlockSpec((1, D), const), pl.BlockSpec((1, D), const),
                  pl.BlockSpec((D, N), const),
                  pl.BlockSpec((D, G), const), pl.BlockSpec((D, G), const),
                  pl.BlockSpec((G, D), const), pl.BlockSpec((G, D), const)],
        out_specs=[rows(D), rows(N), rows(G), pl.BlockSpec((G, tm), lambda i: (0, i))],
        out_shape=[jax.ShapeDtypeStruct((T, D), F32), jax.ShapeDtypeStruct((T, N), F32),
                   jax.ShapeDtypeStruct((T, G), F32), jax.ShapeDtypeStruct((G, T), F32)],
        scratch_shapes=[pltpu.VMEM((2, TOP_K, tm, D // 2), jnp.uint32),
                        pltpu.SemaphoreType.DMA((2,))],
        compiler_params=_cparams(("arbitrary",)),
        name="combine_in_proj",
    )(dest_flat, dest_flat, base, gates, y, ln_g.reshape(1, D), ln_b.reshape(1, D),
      w_main, wgh, wgl, wgth, wgtl)


def _moe_ffn(alpha, layer, h1, p_i, idx, rank, counts, w_pg, w_pp, w_gu, b_gu, w_down, b_down,
             xs_buf):
    T, D = h1.shape
    E = w_gu.shape[1]
    A = T * TOP_K
    counts = counts.reshape(E)
    padded = (counts + MOE_BLOCK - 1) // MOE_BLOCK * MOE_BLOCK
    pad_end = jnp.cumsum(padded)
    pad_start = pad_end - padded
    n_blocks = -(-A // MOE_BLOCK) + E
    P = n_blocks * MOE_BLOCK
    experts = jnp.arange(E, dtype=I32)
    dest = jnp.sum(jnp.where(idx[:, :, None] == experts, pad_start, 0), axis=-1) + rank
    dest_flat = dest.reshape(A).astype(I32)
    blk_start = jnp.arange(n_blocks, dtype=I32) * MOE_BLOCK
    block_e = jnp.minimum(jnp.sum((pad_end[None, :] <= blk_start[:, None]).astype(I32), axis=1),
                          E - 1).astype(I32)
    n_used = (pad_end[-1] // MOE_BLOCK).astype(I32).reshape(1)
    assert xs_buf.shape[0] == P
    base, xs = _dispatch_call(alpha, dest_flat, h1, p_i, w_pg, w_pp, xs_buf)
    y = _moe_call(layer, block_e, n_used, xs, w_gu, b_gu, w_down, b_down)
    return dest_flat, base, y, xs


def _layer(layer, h, u, g, gt, p_i, B, S, alpha, m_i_bias, m_f_bias, m_norm_w, sb_norm_w, g_conv_w,
           g_A_log, g_dt_bias, g_norm_w, w_out, ln1_g, ln1_b, w_router, b_router, w_gu, b_gu,
           w_down, b_down, w_pg, w_pp, xs_buf):
    ym = _mlstm_call(u, g, gt, m_i_bias, m_f_bias, m_norm_w, B, S)
    ys = _sb_call(u, sb_norm_w, B, S, (4 * M_WIDTH + 4 * G_WIDTH) // SB_WIDTH)
    yg = _gdn_call(u, g, gt, g_conv_w, g_A_log, g_dt_bias, g_norm_w, B, S, 4 * M_WIDTH // G_WIDTH)
    h1, idx, gates, rank, counts = _post_call(
        alpha, h, ym, ys, yg, w_out, w_router, b_router, ln1_g, ln1_b)
    dest_flat, base, y, xs = _moe_ffn(alpha, layer, h1, p_i, idx, rank, counts, w_pg, w_pp,
                                      w_gu, b_gu, w_down, b_down, xs_buf)
    return dest_flat, base, gates, y, xs


def kernel(x, p, ln0_g, ln0_b, w_in, m_i_bias, m_f_bias, m_norm_w, sb_norm_w, g_conv_w, g_A_log,
           g_dt_bias, g_norm_w, w_out, ln1_g, ln1_b, w_router, b_router, w_gu, b_gu, w_down,
           b_down, w_ple_gate, w_ple_proj, ln2_g, ln2_b):
    B, S, D = x.shape
    depth = w_in.shape[0]
    T = B * S
    alpha = (2 * depth) ** 0.25
    g0 = 4 * M_WIDTH
    g1 = g0 + 2 * M_HEADS
    s1 = g1 + 3 * SB_WIDTH
    g2 = s1 + 4 * G_WIDTH
    n_slots = (-(-T * TOP_K // MOE_BLOCK) + w_gu.shape[1]) * MOE_BLOCK
    xs_buf = jnp.zeros((n_slots, D // 2 // LANES, LANES), jnp.uint32)
    h = _ln_call(x.reshape(T, D), ln0_g, ln0_b)
    pending = None
    for i in range(depth):
        w = w_in[i]
        w_main = jnp.concatenate([w[:, :g0], w[:, s1:g2], w[:, g1:s1]], axis=1).astype(BF16)
        wg = jnp.concatenate([w[:, g0:g1], w[:, g2:]], axis=1)
        if pending is None:
            u, g, gt = _inproj_call(h, w_main, wg)
        else:
            dest_flat, base, gates, y = pending
            h, u, g, gt = _combine_inproj_call(dest_flat, base, gates, y, ln2_g[i - 1], ln2_b[i - 1],
                                               w_main, wg)
        dest_flat, base, gates, y, xs_buf = _layer(
            i, h, u, g, gt, p[i].reshape(T, -1), B, S, alpha, m_i_bias[i], m_f_bias[i],
            m_norm_w[i], sb_norm_w[i], g_conv_w[i], g_A_log[i], g_dt_bias[i], g_norm_w[i],
            w_out[i].astype(BF16), ln1_g[i], ln1_b[i], w_router[i], b_router[i],
            w_gu, b_gu, w_down, b_down,
            w_ple_gate[i].astype(BF16), w_ple_proj[i].astype(BF16), xs_buf)
        pending = (dest_flat, base, gates, y)
    dest_flat, base, gates, y = pending
    h = _combine_call(dest_flat, base, gates, y, ln2_g[depth - 1], ln2_b[depth - 1])
    return h.reshape(B, S, D)
```

```python
import functools
import math

import jax
import jax.numpy as jnp
from jax import lax
from jax.experimental import pallas as pl
from jax.experimental.pallas import tpu as pltpu

F32 = jnp.float32
BF16 = jnp.bfloat16
I32 = jnp.int32

M_HEADS, M_HEAD_DIM = 4, 64
SB_HEADS, SB_HEAD_DIM = 4, 64
G_HEADS, G_HEAD_DIM = 4, 128
G_CONV = 4
M_WIDTH = M_HEADS * M_HEAD_DIM
SB_WIDTH = SB_HEADS * SB_HEAD_DIM
G_WIDTH = G_HEADS * G_HEAD_DIM
N_GATE_COLS = 16
TOP_K = 4
SWIGLU_LIMIT = 7.0
SWIGLU_ALPHA = 1.702
LN_EPS = 1e-5
NORM_EPS = 1e-6
LANES = 128

ROW_BLOCK = 256
POST_BLOCK = 512
IN_CHUNK = 256
M_CHUNK = 256
SB_BLOCK = 256
G_CHUNK = 64
G_BLOCK = 256
MOE_BLOCK = 512
DMA_UNROLL = 8
DMA_THREADS = 2
GATHER_PRIORITY = 1
VMEM_LIMIT = 56 * 1024 * 1024


def _cparams(sem):
    return pltpu.CompilerParams(dimension_semantics=sem, vmem_limit_bytes=VMEM_LIMIT)


def _split3(x):
    hi = x.astype(BF16)
    r1 = x - hi.astype(F32)
    mid = r1.astype(BF16)
    lo = (r1 - mid.astype(F32)).astype(BF16)
    return hi, mid, lo


def _split2(x):
    hi = x.astype(BF16)
    lo = (x - hi.astype(F32)).astype(BF16)
    return hi, lo


def _dot(a, b):
    return jnp.dot(a, b, preferred_element_type=F32)


def _dot_nt(a, b):
    return lax.dot_general(a, b, (((1,), (1,)), ((), ())), preferred_element_type=F32)


def _dot_tn(a, b):
    return lax.dot_general(a, b, (((0,), (0,)), ((), ())), preferred_element_type=F32)


def _dot3(a, b):
    ah, al = _split2(a)
    bh, bl = _split2(b)
    return _dot(ah, bh) + (_dot(ah, bl) + _dot(al, bh))


def _dot3_nt(a, b):
    ah, al = _split2(a)
    bh, bl = _split2(b)
    return _dot_nt(ah, bh) + (_dot_nt(ah, bl) + _dot_nt(al, bh))


def _bdot(a, b):
    return lax.dot_general(a, b, (((2,), (1,)), ((0,), (0,))), preferred_element_type=F32)


def _bdot_nt(a, b):
    return lax.dot_general(a, b, (((2,), (2,)), ((0,), (0,))), preferred_element_type=F32)


def _bdot_tn(a, b):
    return lax.dot_general(a, b, (((1,), (1,)), ((0,), (0,))), preferred_element_type=F32)


def _bdot3(a, b):
    ah, al = _split2(a)
    bh, bl = _split2(b)
    return _bdot(ah, bh) + (_bdot(ah, bl) + _bdot(al, bh))


def _bdot3_nt(a, b):
    ah, al = _split2(a)
    bh, bl = _split2(b)
    return _bdot_nt(ah, bh) + (_bdot_nt(ah, bl) + _bdot_nt(al, bh))


def _exact_left(mask_bf16, x):
    hi, mid, lo = _split3(x)
    return _dot(mask_bf16, hi) + (_dot(mask_bf16, mid) + _dot(mask_bf16, lo))


def _exact_right(x, mask_bf16):
    hi, mid, lo = _split3(x)
    return _dot(hi, mask_bf16) + (_dot(mid, mask_bf16) + _dot(lo, mask_bf16))


def _softplus(x):
    return jnp.maximum(x, 0.0) + jnp.log1p(jnp.exp(-jnp.abs(x)))


def _log_sigmoid(x):
    return -_softplus(-x)


def _sigmoid(x):
    return 1.0 / (1.0 + jnp.exp(-x))


def _layer_norm(x, g, b):
    mu = jnp.mean(x, axis=-1, keepdims=True)
    xc = x - mu
    var = jnp.mean(xc * xc, axis=-1, keepdims=True)
    return xc * lax.rsqrt(var + LN_EPS) * g + b


def _iota2(shape, dim):
    return lax.broadcasted_iota(I32, shape, dim)


def _ln_kernel(x_ref, g_ref, b_ref, o_ref):
    o_ref[...] = _layer_norm(x_ref[...], g_ref[...], b_ref[...])


def _ln_call(x, g, b):
    T, D = x.shape
    tm = ROW_BLOCK
    return pl.pallas_call(
        _ln_kernel,
        grid=(T // tm,),
        in_specs=[pl.BlockSpec((tm, D), lambda i: (i, 0)),
                  pl.BlockSpec((1, D), lambda i: (0, 0)),
                  pl.BlockSpec((1, D), lambda i: (0, 0))],
        out_specs=pl.BlockSpec((tm, D), lambda i: (i, 0)),
        out_shape=jax.ShapeDtypeStruct((T, D), F32),
        compiler_params=_cparams(("parallel",)),
        name="embed_ln",
    )(x, g.reshape(1, D), b.reshape(1, D))


def _inproj_kernel(h_ref, w_ref, wgh_ref, wgl_ref, wgth_ref, wgtl_ref, u_ref, g_ref, gt_ref):
    h = h_ref[...]
    hh, hl = _split2(h)
    u_ref[...] = _dot(hh, w_ref[...])
    wgh, wgl = wgh_ref[...], wgl_ref[...]
    g_ref[...] = _dot(hh, wgh) + (_dot(hh, wgl) + _dot(hl, wgh))
    wgth, wgtl = wgth_ref[...], wgtl_ref[...]
    gt_ref[...] = _dot_nt(wgth, hh) + (_dot_nt(wgtl, hh) + _dot_nt(wgth, hl))


def _inproj_call(h, w_main, wg):
    T, D = h.shape
    N = w_main.shape[1]
    tm = ROW_BLOCK
    wgh, wgl = _split2(wg)
    wgt = wg.T
    wgth, wgtl = _split2(wgt)
    G = N_GATE_COLS
    const = lambda i: (0, 0)
    return pl.pallas_call(
        _inproj_kernel,
        grid=(T // tm,),
        in_specs=[pl.BlockSpec((tm, D), lambda i: (i, 0)),
                  pl.BlockSpec((D, N), const),
                  pl.BlockSpec((D, G), const), pl.BlockSpec((D, G), const),
                  pl.BlockSpec((G, D), const), pl.BlockSpec((G, D), const)],
        out_specs=[pl.BlockSpec((tm, N), lambda i: (i, 0)),
                   pl.BlockSpec((tm, G), lambda i: (i, 0)),
                   pl.BlockSpec((G, tm), lambda i: (0, i))],
        out_shape=[jax.ShapeDtypeStruct((T, N), F32),
                   jax.ShapeDtypeStruct((T, G), F32),
                   jax.ShapeDtypeStruct((G, T), F32)],
        compiler_params=_cparams(("parallel",)),
        name="in_proj",
    )(h, w_main, wgh, wgl, wgth, wgtl)


def _mlstm_kernel(q_ref, k_ref, v_ref, o_ref, g_ref, gt_ref, brow_ref, bcol_ref, nw_ref,
                  y_ref, c_sc, m_sc):
    L = q_ref.shape[0]
    d = M_HEAD_DIM

    @pl.when(pl.program_id(1) == 0)
    def _():
        c_sc[...] = jnp.zeros_like(c_sc)
        m_sc[...] = jnp.full_like(m_sc, -jnp.inf)

    row = _iota2((L, L), 0)
    col = _iota2((L, L), 1)
    causal = col <= row
    tril = causal.astype(BF16)
    triu = (row <= col).astype(BF16)

    gb = g_ref[...] + brow_ref[...]
    gtb = gt_ref[...] + bcol_ref[...]
    b_cols = _exact_left(tril, _log_sigmoid(gb))
    b_rows = _exact_right(_log_sigmoid(gtb), triu)

    ones_col = (_iota2((L, d), 1) == 0).astype(F32)
    H = range(M_HEADS)
    sl = [slice(hd * d, (hd + 1) * d) for hd in H]
    q = [q_ref[:, c].astype(BF16) for c in sl]
    kf = [k_ref[:, c] * (d ** -0.5) for c in sl]
    k = [x.astype(BF16) for x in kf]
    v_aug = [jnp.concatenate([v_ref[:, c], ones_col], axis=1).astype(BF16) for c in sl]
    bc = [b_cols[:, M_HEADS + hd:M_HEADS + hd + 1] for hd in H]
    lic = [gb[:, hd:hd + 1] for hd in H]
    src = [gtb[hd:hd + 1, :] - b_rows[M_HEADS + hd:M_HEADS + hd + 1, :] for hd in H]
    b_end = [x[L - 1:L, :] for x in bc]
    m_prev = [m_sc[hd] for hd in H]
    c_aug = [c_sc[hd] for hd in H]

    qk = [_dot_nt(q[h], k[h]) for h in H]
    qc = [_dot(q[h], c_aug[h].astype(BF16)) for h in H]
    log_w = [jnp.where(causal, bc[h] + src[h], -jnp.inf) for h in H]
    m_intra = [jnp.max(x, axis=1, keepdims=True) for x in log_w]
    carry_log = [bc[h] + m_prev[h] for h in H]
    m_t = [jnp.maximum(carry_log[h], m_intra[h]) for h in H]
    inter = [jnp.exp(carry_log[h] - m_t[h]) for h in H]
    s = [(qk[h] * jnp.exp(log_w[h] - m_t[h])).astype(BF16) for h in H]
    num_aug = [inter[h] * qc[h] + _dot(s[h], v_aug[h]) for h in H]
    hh = [num_aug[h][:, :d] / jnp.maximum(jnp.abs(num_aug[h][:, d:d + 1]), jnp.exp(-m_t[h]))
          for h in H]

    m_end = [jnp.max(b_end[h] + src[h], axis=1, keepdims=True) for h in H]
    m_new = [jnp.maximum(b_end[h] + m_prev[h], m_end[h]) for h in H]
    wk = [(jnp.exp(b_end[h] - bc[h] + lic[h] - m_new[h]) * kf[h]).astype(BF16) for h in H]
    kv = [_dot_tn(wk[h], v_aug[h]) for h in H]
    for h in H:
        c_sc[h] = jnp.exp(b_end[h] + m_prev[h] - m_new[h]) * c_aug[h] + kv[h]
        m_sc[h] = m_new[h]

    outs = [x * lax.rsqrt(jnp.mean(x * x, axis=1, keepdims=True) + NORM_EPS) for x in hh]
    hcat = jnp.concatenate(outs, axis=1) * nw_ref[...]
    y_ref[...] = _sigmoid(o_ref[...]) * hcat


def _mlstm_call(u, g, gt, i_bias, f_bias, norm_w, B, S):
    T = u.shape[0]
    L = min(M_CHUNK, S)
    nc = S // L
    W = M_WIDTH
    zeros = jnp.zeros((N_GATE_COLS - 2 * M_HEADS,), F32)
    bias = jnp.concatenate([i_bias, f_bias, zeros])
    G = N_GATE_COLS
    rowblk = lambda c: pl.BlockSpec((L, W), lambda b, j, c=c: (b * nc + j, c))
    const = lambda b, j: (0, 0)
    return pl.pallas_call(
        _mlstm_kernel,
        grid=(B, nc),
        in_specs=[rowblk(0), rowblk(1), rowblk(2), rowblk(3),
                  pl.BlockSpec((L, G), lambda b, j: (b * nc + j, 0)),
                  pl.BlockSpec((G, L), lambda b, j: (0, b * nc + j)),
                  pl.BlockSpec((1, G), const), pl.BlockSpec((G, 1), const),
                  pl.BlockSpec((1, W), const)],
        out_specs=pl.BlockSpec((L, W), lambda b, j: (b * nc + j, 0)),
        out_shape=jax.ShapeDtypeStruct((T, W), F32),
        scratch_shapes=[pltpu.VMEM((M_HEADS, M_HEAD_DIM, 2 * M_HEAD_DIM), F32),
                        pltpu.VMEM((M_HEADS, 1, 1), F32)],
        compiler_params=_cparams(("parallel", "arbitrary")),
        name="mlstm",
    )(u, u, u, u, g, gt, bias.reshape(1, G), bias.reshape(G, 1), norm_w.reshape(1, W))


def _sb_kernel(q_ref, k_ref, v_ref, nw_ref, y_ref, qb_sc, acc_sc, r_sc):
    tq = q_ref.shape[0]
    tk = tq
    d = SB_HEAD_DIM
    qi = pl.program_id(1)
    srow = _iota2((tq, tk), 0)
    scol = _iota2((tq, tk), 1)
    later = (scol < srow).astype(BF16)

    qb_sc[...] = (q_ref[...] * (d ** -0.5)).astype(BF16)
    acc_sc[...] = jnp.zeros_like(acc_sc)
    r_sc[...] = jnp.zeros_like(r_sc)

    def block(kb, diagonal):
        off = pl.multiple_of(kb * tk, tk)
        heads = range(SB_HEADS)
        cols = [slice(hd * d, (hd + 1) * d) for hd in heads]
        valid = scol < srow
        z = [_dot_nt(qb_sc[:, c], k_ref[pl.ds(off, tk), c].astype(BF16)) for c in cols]
        sp = [jnp.maximum(x, 0.0) + jnp.log(1.0 + jnp.exp(-jnp.abs(x))) for x in z]
        if diagonal:
            sp = [jnp.where(valid, x, 0.0) for x in sp]
        c2 = [_dot(jnp.concatenate(_split2(x), axis=0), later) for x in sp]
        cs = [x[:tq] + x[tq:] for x in c2]
        a = [jnp.exp(z[h] - sp[h] - cs[h]) for h in heads]
        if diagonal:
            a = [jnp.where(valid, x, 0.0) for x in a]
        pv = [_dot(a[h].astype(BF16), v_ref[pl.ds(off, tk), cols[h]].astype(BF16)) for h in heads]
        for h in heads:
            rest = r_sc[h]
            acc_sc[h] = acc_sc[h] + jnp.exp(-rest) * pv[h]
            r_sc[h] = rest + (cs[h][:, 0:1] + sp[h][:, 0:1])

    block(qi, True)

    def body(it, carry):
        block(qi - it, False)
        return carry

    lax.fori_loop(1, qi + 1, body, 0)
    outs = []
    for hd in range(SB_HEADS):
        o = acc_sc[hd]
        outs.append(o * lax.rsqrt(jnp.mean(o * o, axis=1, keepdims=True) + NORM_EPS))
    y_ref[...] = jnp.concatenate(outs, axis=1) * nw_ref[...]


def _sb_call(u, norm_w, B, S, col0):
    T = u.shape[0]
    tq = min(SB_BLOCK, S)
    nq = S // tq
    W = SB_WIDTH
    return pl.pallas_call(
        _sb_kernel,
        grid=(B, nq),
        in_specs=[pl.BlockSpec((tq, W), lambda b, i: (b * nq + i, col0)),
                  pl.BlockSpec((S, W), lambda b, i: (b, col0 + 1)),
                  pl.BlockSpec((S, W), lambda b, i: (b, col0 + 2)),
                  pl.BlockSpec((1, W), lambda b, i: (0, 0))],
        out_specs=pl.BlockSpec((tq, W), lambda b, i: (b * nq + i, 0)),
        out_shape=jax.ShapeDtypeStruct((T, W), F32),
        scratch_shapes=[pltpu.VMEM((tq, W), BF16),
                        pltpu.VMEM((SB_HEADS, tq, SB_HEAD_DIM), F32),
                        pltpu.VMEM((SB_HEADS, tq, 1), F32)],
        compiler_params=_cparams(("parallel", "arbitrary")),
        name="stickbreak",
    )(u, u, u, norm_w.reshape(1, W))


def _gdn_kernel(q_ref, k_ref, v_ref, z_ref, g_ref, gt_ref, cw_ref, arow_ref, acol_ref,
                drow_ref, dcol_ref, nw_ref, y_ref, xq_sc, xk_sc, xv_sc, s_sc):
    LB = q_ref.shape[0]
    C = G_CHUNK
    d = G_HEAD_DIM
    W = G_WIDTH
    HALO = 8

    @pl.when(pl.program_id(1) == 0)
    def _():
        s_sc[...] = jnp.zeros_like(s_sc)
        for sc in (xq_sc, xk_sc, xv_sc):
            sc[0:HALO, :] = jnp.zeros((HALO, W), F32)

    @pl.when(pl.program_id(1) != 0)
    def _():
        for sc in (xq_sc, xk_sc, xv_sc):
            sc[0:HALO, :] = sc[LB:LB + HALO, :]

    def conv_silu(x_ref, sc, j):
        sc[HALO:HALO + LB, :] = x_ref[...]
        acc = None
        for t in range(G_CONV):
            w = cw_ref[t:t + 1, j * W:(j + 1) * W]
            term = w * sc[HALO - (G_CONV - 1) + t:HALO - (G_CONV - 1) + t + LB, :]
            acc = term if acc is None else acc + term
        return acc * _sigmoid(acc)

    qc = conv_silu(q_ref, xq_sc, 0)
    kc = conv_silu(k_ref, xk_sc, 1)
    vc = conv_silu(v_ref, xv_sc, 2)

    gb = g_ref[...]
    gtb = gt_ref[...]
    gdec_cols = -jnp.exp(arow_ref[...]) * _softplus(gb + drow_ref[...])
    gdec_rows = -jnp.exp(acol_ref[...]) * _softplus(gtb + dcol_ref[...])
    beta_cols = _sigmoid(gb)
    row = _iota2((LB, LB), 0)
    col = _iota2((LB, LB), 1)
    shift = C.bit_length() - 1
    same = jnp.right_shift(row, shift) == jnp.right_shift(col, shift)
    tril = (same & (col <= row)).astype(BF16)
    triu = (same & (row <= col)).astype(BF16)
    gam_cols = _exact_left(tril, gdec_cols)
    gam_rows = _exact_right(gdec_rows, triu)

    nc = LB // C
    NB = G_HEADS * nc

    def per_head(x):
        return jnp.stack([x[:, h * d:(h + 1) * d] for h in range(G_HEADS)], axis=0).reshape(NB, C, d)

    q3 = per_head(qc)
    k3 = per_head(kc)
    v3 = per_head(vc)
    q3 = q3 * lax.rsqrt(jnp.sum(q3 * q3, axis=2, keepdims=True) + NORM_EPS) * (d ** -0.5)
    k3 = k3 * lax.rsqrt(jnp.sum(k3 * k3, axis=2, keepdims=True) + NORM_EPS)
    ga0 = 2 * G_HEADS
    gb0 = 3 * G_HEADS
    gam_c = jnp.stack([gam_cols[:, ga0 + h:ga0 + h + 1] for h in range(G_HEADS)], 0).reshape(NB, C, 1)
    beta = jnp.stack([beta_cols[:, gb0 + h:gb0 + h + 1] for h in range(G_HEADS)], 0).reshape(NB, C, 1)
    gam_r = jnp.stack([gam_rows[ga0 + h:ga0 + h + 1, c * C:(c + 1) * C]
                       for h in range(G_HEADS) for c in range(nc)], 0)
    gam_end = gam_c[:, C - 1:C, :]

    r64 = _iota2((1, C, C), 1)
    c64 = _iota2((1, C, C), 2)
    incl = c64 <= r64
    strict = c64 < r64

    decay = jnp.exp(jnp.where(incl, gam_c - gam_r, -jnp.inf))
    kk = _bdot3_nt(k3, k3)
    m = jnp.where(strict, beta * kk * decay, 0.0)
    tinv = jnp.broadcast_to((r64 == c64).astype(F32), (NB, C, C))
    s = 1
    while s < C:
        sh = s.bit_length() - 1
        same2s = jnp.right_shift(r64, sh + 1) == jnp.right_shift(c64, sh + 1)
        low_mask = same2s & ((jnp.right_shift(r64, sh) & 1) == 1) & ((jnp.right_shift(c64, sh) & 1) == 0)
        low = jnp.where(low_mask, m, 0.0).astype(BF16)
        tb = tinv.astype(BF16)
        tinv = tinv - _bdot(_bdot(tb, low).astype(BF16), tb)
        s *= 2
    eg = jnp.exp(gam_c)
    tb = tinv.astype(BF16)
    u = _bdot(tb, (v3 * beta).astype(BF16))
    w = _bdot(tb, (k3 * (beta * eg)).astype(BF16)).astype(BF16)
    qk = (_bdot_nt(q3.astype(BF16), k3.astype(BF16)) * decay).astype(BF16)
    q_dec = (q3 * eg).astype(BF16)
    k_dec = (k3 * jnp.exp(gam_end - gam_c)).astype(BF16)
    cdec = jnp.exp(gam_end)

    def chunk(x, c):
        return x.reshape((G_HEADS, nc) + x.shape[1:])[:, c]

    state = s_sc[...]
    o_chunks = []
    for c in range(nc):
        sb = state.astype(BF16)
        v_new = chunk(u, c) - _bdot(chunk(w, c), sb)
        vb = v_new.astype(BF16)
        o_chunks.append(_bdot(chunk(q_dec, c), sb) + _bdot(chunk(qk, c), vb))
        state = chunk(cdec, c) * state + _bdot_tn(chunk(k_dec, c), vb)
    s_sc[...] = state

    nw = nw_ref[...]
    o = jnp.concatenate(o_chunks, axis=1)
    o = o * lax.rsqrt(jnp.mean(o * o, axis=2, keepdims=True) + NORM_EPS) * nw
    zz = z_ref[...]
    y_ref[...] = jnp.concatenate([o[h] for h in range(G_HEADS)], axis=1) * (zz * _sigmoid(zz))


def _gdn_call(u, g, gt, conv_w, a_log, dt_bias, norm_w, B, S, col0):
    T = u.shape[0]
    LB = min(G_BLOCK, S)
    nb = S // LB
    W = G_WIDTH
    G = N_GATE_COLS
    pad = jnp.zeros((2 * G_HEADS,), F32)
    arow = jnp.concatenate([pad, a_log, jnp.zeros((G_HEADS,), F32)])
    drow = jnp.concatenate([pad, dt_bias, jnp.zeros((G_HEADS,), F32)])
    rowblk = lambda c: pl.BlockSpec((LB, W), lambda b, j, c=c: (b * nb + j, col0 + c))
    const = lambda b, j: (0, 0)
    return pl.pallas_call(
        _gdn_kernel,
        grid=(B, nb),
        in_specs=[rowblk(0), rowblk(1), rowblk(2), rowblk(3),
                  pl.BlockSpec((LB, G), lambda b, j: (b * nb + j, 0)),
                  pl.BlockSpec((G, LB), lambda b, j: (0, b * nb + j)),
                  pl.BlockSpec((G_CONV, 3 * W), const),
                  pl.BlockSpec((1, G), const), pl.BlockSpec((G, 1), const),
                  pl.BlockSpec((1, G), const), pl.BlockSpec((G, 1), const),
                  pl.BlockSpec((1, G_HEAD_DIM), const)],
        out_specs=pl.BlockSpec((LB, W), lambda b, j: (b * nb + j, 0)),
        out_shape=jax.ShapeDtypeStruct((T, W), F32),
        scratch_shapes=[pltpu.VMEM((LB + 8, W), F32), pltpu.VMEM((LB + 8, W), F32),
                        pltpu.VMEM((LB + 8, W), F32),
                        pltpu.VMEM((G_HEADS, G_HEAD_DIM, G_HEAD_DIM), F32)],
        compiler_params=_cparams(("parallel", "arbitrary")),
        name="gdn",
    )(u, u, u, u, g, gt, conv_w, arow.reshape(1, G), arow.reshape(G, 1),
      drow.reshape(1, G), drow.reshape(G, 1), norm_w.reshape(1, G_HEAD_DIM))


def _post_kernel(alpha, h_ref, ym_ref, ys_ref, yg_ref, wo_ref,
                 wrh_ref, wrl_ref, br_ref, g1_ref, b1_ref,
                 h1_ref, idx_ref, gate_ref, rank_ref, cnt_ref, cnt_sc):
    tm = h_ref.shape[0]
    E = br_ref.shape[1]

    @pl.when(pl.program_id(0) == 0)
    def _():
        cnt_sc[...] = jnp.zeros_like(cnt_sc)

    y = jnp.concatenate([ym_ref[...], ys_ref[...], yg_ref[...]], axis=1).astype(BF16)
    a = alpha * h_ref[...] + _dot(y, wo_ref[...])
    h1 = _layer_norm(a, g1_ref[...], b1_ref[...])
    h1h, h1l = _split2(h1)
    h1_ref[...] = h1

    wrh, wrl = wrh_ref[...], wrl_ref[...]
    logits = _dot(h1h, wrh) + (_dot(h1h, wrl) + _dot(h1l, wrh)) + br_ref[...]
    lane = _iota2((tm, E), 1).astype(F32)
    work = logits
    vals, idxs = [], []
    anyhot = jnp.zeros((tm, E), F32)
    for _ in range(TOP_K):
        mx = jnp.max(work, axis=1, keepdims=True)
        ix = jnp.min(jnp.where(work == mx, lane, float(E)), axis=1, keepdims=True)
        sel = lane == ix
        vals.append(mx)
        idxs.append(ix)
        anyhot = jnp.where(sel, 1.0, anyhot)
        work = jnp.where(sel, -jnp.inf, work)
    ex = [jnp.exp(v - vals[0]) for v in vals]
    tot = ex[0] + ex[1] + ex[2] + ex[3]
    gates = [e / tot for e in ex]

    r = _iota2((tm, tm), 0)
    c = _iota2((tm, tm), 1)
    before = (c < r).astype(BF16)
    pos = cnt_sc[...] + _dot(before, anyhot.astype(BF16))
    ranks = [jnp.sum(jnp.where(lane == ix, pos, 0.0), axis=1, keepdims=True) for ix in idxs]
    cnt_sc[...] = cnt_sc[...] + jnp.sum(anyhot, axis=0, keepdims=True)
    cnt_ref[...] = cnt_sc[...].astype(I32)

    kl = _iota2((tm, TOP_K), 1)

    def pack(cols):
        out = jnp.broadcast_to(cols[0], (tm, TOP_K))
        for k in range(1, TOP_K):
            out = jnp.where(kl == k, cols[k], out)
        return out

    idx_ref[...] = pack(idxs).astype(I32)
    gate_ref[...] = pack(gates)
    rank_ref[...] = pack(ranks).astype(I32)


def _post_call(alpha, h, ym, ys, yg, w_out, w_router, b_router, ln_g, ln_b):
    T, D = h.shape
    tm = POST_BLOCK
    E = w_router.shape[1]
    wrh, wrl = _split2(w_router)
    const = lambda i: (0, 0)
    rows = lambda w: pl.BlockSpec((tm, w), lambda i: (i, 0))
    return pl.pallas_call(
        functools.partial(_post_kernel, alpha),
        grid=(T // tm,),
        in_specs=[rows(D), rows(M_WIDTH), rows(SB_WIDTH), rows(G_WIDTH),
                  pl.BlockSpec(w_out.shape, const),
                  pl.BlockSpec((D, E), const), pl.BlockSpec((D, E), const),
                  pl.BlockSpec((1, E), const), pl.BlockSpec((1, D), const),
                  pl.BlockSpec((1, D), const)],
        out_specs=[rows(D), rows(TOP_K), rows(TOP_K), rows(TOP_K),
                   pl.BlockSpec((1, E), const)],
        out_shape=[jax.ShapeDtypeStruct((T, D), F32),
                   jax.ShapeDtypeStruct((T, TOP_K), I32), jax.ShapeDtypeStruct((T, TOP_K), F32),
                   jax.ShapeDtypeStruct((T, TOP_K), I32), jax.ShapeDtypeStruct((1, E), I32)],
        scratch_shapes=[pltpu.VMEM((1, E), F32)],
        compiler_params=_cparams(("arbitrary",)),
        name="post_mix",
    )(h, ym, ys, yg, w_out, wrh, wrl, b_router.reshape(1, E),
      ln_g.reshape(1, D), ln_b.reshape(1, D))


def _pack_bf16_pairs(x):
    half = x.shape[1] // 2
    bits = lax.bitcast_convert_type(x.astype(BF16).astype(F32), jnp.uint32)
    return (bits[:, :half] >> 16) | (bits[:, half:] & jnp.uint32(0xFFFF0000))


def _unpack_bf16_pairs(w):
    lo = lax.bitcast_convert_type(w << 16, F32)
    hi = lax.bitcast_convert_type(w & jnp.uint32(0xFFFF0000), F32)
    return lo, hi


def _to_token_tiles(ref, x):
    for j in range(ref.shape[-2]):
        ref[:, j, :] = x[:, j * LANES:(j + 1) * LANES]


def _from_token_tiles(ref):
    return jnp.concatenate([ref[:, j, :] for j in range(ref.shape[-2])], axis=1)


def _dispatch_kernel(alpha, dest_ref, h1_ref, p_ref, wpg_ref, wpp_ref, xs_in_ref,
                     base_ref, xs_ref, xpk_sc, sem):
    del xs_in_ref
    tm, D = h1_ref.shape
    half = D // 2
    h1 = h1_ref[...]
    hb = h1.astype(BF16)
    _to_token_tiles(xpk_sc, _pack_bf16_pairs(h1))

    pb = p_ref[...].astype(BF16)
    nch = D // IN_CHUNK
    rows_per = -(-tm // nch)
    for c in range(nch):
        cs = slice(c * IN_CHUNK, (c + 1) * IN_CHUNK)
        ple = _sigmoid(_dot(hb, wpg_ref[:, cs])) * _dot(pb, wpp_ref[:, cs])
        base_ref[:, cs] = alpha * h1[:, cs] + ple
        for r in range(c * rows_per, min((c + 1) * rows_per, tm)):
            for k in range(TOP_K):
                pltpu.make_async_copy(xpk_sc.at[r], xs_ref.at[dest_ref[r * TOP_K + k]],
                                      sem).start(priority=k % DMA_THREADS)
    for k in range(TOP_K):
        pltpu.make_async_copy(xpk_sc, xs_ref.at[pl.ds(0, tm)], sem).wait()


def _dispatch_call(alpha, dest_flat, h1, p, w_pg, w_pp, xs0):
    T, D = h1.shape
    tm = ROW_BLOCK
    P = p.shape[1]
    const = lambda i: (0, 0)
    rows = lambda w: pl.BlockSpec((tm, w), lambda i: (i, 0))
    base, xs = pl.pallas_call(
        functools.partial(_dispatch_kernel, alpha),
        grid=(T // tm,),
        in_specs=[pl.BlockSpec((tm * TOP_K,), lambda i: (i,), memory_space=pltpu.SMEM),
                  rows(D), rows(P), pl.BlockSpec((D, D), const), pl.BlockSpec((P, D), const),
                  pl.BlockSpec(memory_space=pl.ANY)],
        out_specs=[rows(D), pl.BlockSpec(memory_space=pl.ANY)],
        out_shape=[jax.ShapeDtypeStruct((T, D), F32),
                   jax.ShapeDtypeStruct(xs0.shape, jnp.uint32)],
        scratch_shapes=[pltpu.VMEM((tm,) + xs0.shape[1:], jnp.uint32), pltpu.SemaphoreType.DMA],
        input_output_aliases={5: 1},
        compiler_params=_cparams(("arbitrary",)),
        name="ple_dispatch",
    )(dest_flat, h1, p, w_pg, w_pp, xs0)
    return base, xs


def _moe_kernel(be_ref, nb_ref, x_ref, wgu_ref, bgu_ref, wd_ref, bd_ref, y_ref, wgu_sc, wd_sc):
    i = pl.program_id(0)
    F = wd_ref.shape[2]
    CH = 128

    new_expert = jnp.logical_or(i == 0, be_ref[i] != be_ref[jnp.maximum(i - 1, 0)])

    @pl.when(jnp.logical_and(new_expert, i < nb_ref[0]))
    def _():
        def cast_gu(r, c):
            rows = pl.ds(pl.multiple_of(r * CH, CH), CH)
            wgu_sc[rows, :] = wgu_ref[0, 0, rows, :].astype(BF16)
            return c

        def cast_d(r, c):
            rows = pl.ds(pl.multiple_of(r * CH, CH), CH)
            wd_sc[rows, :] = wd_ref[0, 0, rows, :].astype(BF16)
            return c

        lax.fori_loop(0, wgu_ref.shape[2] // CH, cast_gu, 0)
        lax.fori_loop(0, F // CH, cast_d, 0)

    @pl.when(i < nb_ref[0])
    def _():
        xw = _from_token_tiles(x_ref)
        xb = jnp.concatenate(_unpack_bf16_pairs(xw), axis=1).astype(BF16)
        h = _dot(xb, wgu_sc[...]) + bgu_ref[0, 0]
        gate = jnp.minimum(h[:, :F], SWIGLU_LIMIT)
        up = jnp.clip(h[:, F:], -SWIGLU_LIMIT, SWIGLU_LIMIT)
        act = (up + 1.0) * gate * _sigmoid(SWIGLU_ALPHA * gate)
        y_ref[...] = _pack_bf16_pairs(_dot(act.astype(BF16), wd_sc[...]) + bd_ref[0, 0])

    @pl.when(i >= nb_ref[0])
    def _():
        y_ref[...] = jnp.zeros_like(y_ref)


def _moe_call(layer, block_e, n_used, xs, w_gu, b_gu, w_down, b_down):
    P = xs.shape[0]
    L, E, D, F2 = w_gu.shape
    F = F2 // 2
    tm = MOE_BLOCK
    nblk = P // tm
    grid_spec = pltpu.PrefetchScalarGridSpec(
        num_scalar_prefetch=2,
        grid=(nblk,),
        in_specs=[pl.BlockSpec((tm,) + xs.shape[1:], lambda i, be, nb: (i, 0, 0)),
                  pl.BlockSpec((1, 1, D, F2), lambda i, be, nb: (layer, be[i], 0, 0)),
                  pl.BlockSpec((1, 1, 1, F2), lambda i, be, nb: (layer, be[i], 0, 0)),
                  pl.BlockSpec((1, 1, F, D), lambda i, be, nb: (layer, be[i], 0, 0)),
                  pl.BlockSpec((1, 1, 1, D), lambda i, be, nb: (layer, be[i], 0, 0))],
        out_specs=pl.BlockSpec((tm, D // 2), lambda i, be, nb: (i, 0)),
        scratch_shapes=[pltpu.VMEM((D, F2), BF16), pltpu.VMEM((F, D), BF16)],
    )
    return pl.pallas_call(
        _moe_kernel,
        grid_spec=grid_spec,
        out_shape=jax.ShapeDtypeStruct((P, D // 2), jnp.uint32),
        compiler_params=_cparams(("arbitrary",)),
        name="moe_experts",
    )(block_e, n_used, xs, w_gu, b_gu.reshape(L, E, 1, F2), w_down, b_down.reshape(L, E, 1, D))


def _combine_kernel(dest_ref, dnext_ref, base_ref, gate_ref, y_ref, g_ref, b_ref, o_ref, ybuf, sems):
    tm = base_ref.shape[0]
    i = pl.program_id(0)
    n = pl.num_programs(0)
    slot = i % 2

    def gather(idx_ref, sl):
        def issue(r, carry):
            for k in range(TOP_K):
                pltpu.make_async_copy(y_ref.at[pl.ds(idx_ref[r * TOP_K + k], 1), :],
                                      ybuf.at[sl, k, pl.ds(r, 1), :],
                                      sems.at[sl]).start(priority=GATHER_PRIORITY)
            return carry
        lax.fori_loop(0, tm, issue, 0, unroll=DMA_UNROLL)

    @pl.when(i == 0)
    def _():
        gather(dest_ref, 0)

    @pl.when(i + 1 < n)
    def _():
        gather(dnext_ref, 1 - slot)

    for k in range(TOP_K):
        pltpu.make_async_copy(y_ref.at[pl.ds(0, tm), :], ybuf.at[slot, k], sems.at[slot]).wait()
    gate = gate_ref[...]
    ffn_lo, ffn_hi = None, None
    for k in range(TOP_K):
        lo, hi = _unpack_bf16_pairs(ybuf[slot, k])
        gk = gate[:, k:k + 1]
        ffn_lo = gk * lo if ffn_lo is None else ffn_lo + gk * lo
        ffn_hi = gk * hi if ffn_hi is None else ffn_hi + gk * hi
    acc = base_ref[...] + jnp.concatenate([ffn_lo, ffn_hi], axis=1)
    o_ref[...] = _layer_norm(acc, g_ref[...], b_ref[...])


def _combine_call(dest_flat, base, gates, y, ln_g, ln_b):
    T, D = base.shape
    tm = ROW_BLOCK
    rows = lambda w: pl.BlockSpec((tm, w), lambda i: (i, 0))
    const = lambda i: (0, 0)
    nblk = T // tm
    return pl.pallas_call(
        _combine_kernel,
        grid=(nblk,),
        in_specs=[pl.BlockSpec((tm * TOP_K,), lambda i: (i,), memory_space=pltpu.SMEM),
                  pl.BlockSpec((tm * TOP_K,), lambda i: (jnp.minimum(i + 1, nblk - 1),),
                               memory_space=pltpu.SMEM),
                  rows(D), rows(TOP_K), pl.BlockSpec(memory_space=pl.ANY),
                  pl.BlockSpec((1, D), const), pl.BlockSpec((1, D), const)],
        out_specs=rows(D),
        out_shape=jax.ShapeDtypeStruct((T, D), F32),
        scratch_shapes=[pltpu.VMEM((2, TOP_K, tm, D // 2), jnp.uint32),
                        pltpu.SemaphoreType.DMA((2,))],
        compiler_params=_cparams(("arbitrary",)),
        name="combine_ln",
    )(dest_flat, dest_flat, base, gates, y, ln_g.reshape(1, D), ln_b.reshape(1, D))


def _combine_inproj_kernel(dest_ref, dnext_ref, base_ref, gate_ref, y_ref, g_ref, b_ref,
                           w_ref, wgh_ref, wgl_ref, wgth_ref, wgtl_ref,
                           h_ref, u_ref, go_ref, gto_ref, ybuf, sems):
    tm = base_ref.shape[0]
    N = w_ref.shape[1]
    i = pl.program_id(0)
    n = pl.num_programs(0)
    slot = i % 2
    other = 1 - slot

    def row_copy(idx_ref, r, k, sl):
        return pltpu.make_async_copy(y_ref.at[pl.ds(idx_ref[r * TOP_K + k], 1), :],
                                     ybuf.at[sl, k, pl.ds(r, 1), :], sems.at[sl])

    @pl.when(i == 0)
    def _():
        def issue(r, carry):
            for k in range(TOP_K):
                row_copy(dest_ref, r, k, 0).start(priority=GATHER_PRIORITY)
            return carry
        lax.fori_loop(0, tm, issue, 0, unroll=DMA_UNROLL)

    def wait_slot(sl):
        for k in range(TOP_K):
            pltpu.make_async_copy(y_ref.at[pl.ds(0, tm), :], ybuf.at[sl, k], sems.at[sl]).wait()

    wait_slot(slot)
    gate = gate_ref[...]
    ffn_lo, ffn_hi = None, None
    for k in range(TOP_K):
        lo, hi = _unpack_bf16_pairs(ybuf[slot, k])
        gk = gate[:, k:k + 1]
        ffn_lo = gk * lo if ffn_lo is None else ffn_lo + gk * lo
        ffn_hi = gk * hi if ffn_hi is None else ffn_hi + gk * hi
    acc = base_ref[...] + jnp.concatenate([ffn_lo, ffn_hi], axis=1)
    h = _layer_norm(acc, g_ref[...], b_ref[...])
    h_ref[...] = h
    hh, hl = _split2(h)

    nch = N // IN_CHUNK
    rows_per = -(-tm // nch)
    for c in range(nch):
        cs = slice(c * IN_CHUNK, (c + 1) * IN_CHUNK)
        u_ref[:, cs] = _dot(hh, w_ref[:, cs])
        for r in range(c * rows_per, min((c + 1) * rows_per, tm)):
            for k in range(TOP_K):
                row_copy(dnext_ref, r, k, other).start(priority=GATHER_PRIORITY)
    wgh, wgl = wgh_ref[...], wgl_ref[...]
    go_ref[...] = _dot(hh, wgh) + (_dot(hh, wgl) + _dot(hl, wgh))
    wgth, wgtl = wgth_ref[...], wgtl_ref[...]
    gto_ref[...] = _dot_nt(wgth, hh) + (_dot_nt(wgtl, hh) + _dot_nt(wgth, hl))

    @pl.when(i == n - 1)
    def _():
        wait_slot(other)


def _combine_inproj_call(dest_flat, base, gates, y, ln_g, ln_b, w_main, wg):
    T, D = base.shape
    N = w_main.shape[1]
    tm = ROW_BLOCK
    G = N_GATE_COLS
    wgh, wgl = _split2(wg)
    wgth, wgtl = _split2(wg.T)
    rows = lambda w: pl.BlockSpec((tm, w), lambda i: (i, 0))
    const = lambda i: (0, 0)
    nblk = T // tm
    return pl.pallas_call(
        _combine_inproj_kernel,
        grid=(nblk,),
        in_specs=[pl.BlockSpec((tm * TOP_K,), lambda i: (i,), memory_space=pltpu.SMEM),
                  pl.BlockSpec((tm * TOP_K,), lambda i: (jnp.minimum(i + 1, nblk - 1),),
                               memory_space=pltpu.SMEM),
                  rows(D), rows(TOP_K), pl.BlockSpec(memory_space=pl.ANY),
                  pl.BlockSpec((1, D), const), pl.BlockSpec((1, D), const),
                  pl.BlockSpec((D, N), const),
                  pl.BlockSpec((D, G), const), pl.BlockSpec((D, G), const),
                  pl.BlockSpec((G, D), const), pl.BlockSpec((G, D), const)],
        out_specs=[rows(D), rows(N), rows(G), pl.BlockSpec((G, tm), lambda i: (0, i))],
        out_shape=[jax.ShapeDtypeStruct((T, D), F32), jax.ShapeDtypeStruct((T, N), F32),
                   jax.ShapeDtypeStruct((T, G), F32), jax.ShapeDtypeStruct((G, T), F32)],
        scratch_shapes=[pltpu.VMEM((2, TOP_K, tm, D // 2), jnp.uint32),
                        pltpu.SemaphoreType.DMA((2,))],
        compiler_params=_cparams(("arbitrary",)),
        name="combine_in_proj",
    )(dest_flat, dest_flat, base, gates, y, ln_g.reshape(1, D), ln_b.reshape(1, D),
      w_main, wgh, wgl, wgth, wgtl)


def _moe_ffn(alpha, layer, h1, p_i, idx, rank, counts, w_pg, w_pp, w_gu, b_gu, w_down, b_down,
             xs_buf):
    T, D = h1.shape
    E = w_gu.shape[1]
    A = T * TOP_K
    counts = counts.reshape(E)
    padded = (counts + MOE_BLOCK - 1) // MOE_BLOCK * MOE_BLOCK
    pad_end = jnp.cumsum(padded)
    pad_start = pad_end - padded
    n_blocks = -(-A // MOE_BLOCK) + E
    P = n_blocks * MOE_BLOCK
    experts = jnp.arange(E, dtype=I32)
    dest = jnp.sum(jnp.where(idx[:, :, None] == experts, pad_start, 0), axis=-1) + rank
    dest_flat = dest.reshape(A).astype(I32)
    blk_start = jnp.arange(n_blocks, dtype=I32) * MOE_BLOCK
    block_e = jnp.minimum(jnp.sum((pad_end[None, :] <= blk_start[:, None]).astype(I32), axis=1),
                          E - 1).astype(I32)
    n_used = (pad_end[-1] // MOE_BLOCK).astype(I32).reshape(1)
    assert xs_buf.shape[0] == P
    base, xs = _dispatch_call(alpha, dest_flat, h1, p_i, w_pg, w_pp, xs_buf)
    y = _moe_call(layer, block_e, n_used, xs, w_gu, b_gu, w_down, b_down)
    return dest_flat, base, y, xs


def _layer(layer, h, u, g, gt, p_i, B, S, alpha, m_i_bias, m_f_bias, m_norm_w, sb_norm_w, g_conv_w,
           g_A_log, g_dt_bias, g_norm_w, w_out, ln1_g, ln1_b, w_router, b_router, w_gu, b_gu,
           w_down, b_down, w_pg, w_pp, xs_buf):
    ym = _mlstm_call(u, g, gt, m_i_bias, m_f_bias, m_norm_w, B, S)
    ys = _sb_call(u, sb_norm_w, B, S, (4 * M_WIDTH + 4 * G_WIDTH) // SB_WIDTH)
    yg = _gdn_call(u, g, gt, g_conv_w, g_A_log, g_dt_bias, g_norm_w, B, S, 4 * M_WIDTH // G_WIDTH)
    h1, idx, gates, rank, counts = _post_call(
        alpha, h, ym, ys, yg, w_out, w_router, b_router, ln1_g, ln1_b)
    dest_flat, base, y, xs = _moe_ffn(alpha, layer, h1, p_i, idx, rank, counts, w_pg, w_pp,
                                      w_gu, b_gu, w_down, b_down, xs_buf)
    return dest_flat, base, gates, y, xs


def kernel(x, p, ln0_g, ln0_b, w_in, m_i_bias, m_f_bias, m_norm_w, sb_norm_w, g_conv_w, g_A_log,
           g_dt_bias, g_norm_w, w_out, ln1_g, ln1_b, w_router, b_router, w_gu, b_gu, w_down,
           b_down, w_ple_gate, w_ple_proj, ln2_g, ln2_b):
    B, S, D = x.shape
    depth = w_in.shape[0]
    T = B * S
    alpha = (2 * depth) ** 0.25
    g0 = 4 * M_WIDTH
    g1 = g0 + 2 * M_HEADS
    s1 = g1 + 3 * SB_WIDTH
    g2 = s1 + 4 * G_WIDTH
    n_slots = (-(-T * TOP_K // MOE_BLOCK) + w_gu.shape[1]) * MOE_BLOCK
    xs_buf = jnp.zeros((n_slots, D // 2 // LANES, LANES), jnp.uint32)
    h = _ln_call(x.reshape(T, D), ln0_g, ln0_b)
    pending = None
    for i in range(depth):
        w = w_in[i]
        w_main = jnp.concatenate([w[:, :g0], w[:, s1:g2], w[:, g1:s1]], axis=1).astype(BF16)
        wg = jnp.concatenate([w[:, g0:g1], w[:, g2:]], axis=1)
        if pending is None:
            u, g, gt = _inproj_call(h, w_main, wg)
        else:
            dest_flat, base, gates, y = pending
            h, u, g, gt = _combine_inproj_call(dest_flat, base, gates, y, ln2_g[i - 1], ln2_b[i - 1],
                                               w_main, wg)
        dest_flat, base, gates, y, xs_buf = _layer(
            i, h, u, g, gt, p[i].reshape(T, -1), B, S, alpha, m_i_bias[i], m_f_bias[i],
            m_norm_w[i], sb_norm_w[i], g_conv_w[i], g_A_log[i], g_dt_bias[i], g_norm_w[i],
            w_out[i].astype(BF16), ln1_g[i], ln1_b[i], w_router[i], b_router[i],
            w_gu, b_gu, w_down, b_down,
            w_ple_gate[i].astype(BF16), w_ple_proj[i].astype(BF16), xs_buf)
        pending = (dest_flat, base, gates, y)
    dest_flat, base, gates, y = pending
    h = _combine_call(dest_flat, base, gates, y, ln2_g[depth - 1], ln2_b[depth - 1])
    return h.reshape(B, S, D)
```
